```python
import math
import jax
import jax.numpy as jnp
from jax import lax
import numpy as np

D_MODEL = 1024
BATCH = 8
SEQ = 4096
DEPTH = 4
DEC_BATCH = 1
DEC_SEQ = 16384
PAST_LEN = 128

GRID_W = 64
NA_HEADS = 4
NA_HEAD_DIM = 64
NA_WIDTH = NA_HEADS * NA_HEAD_DIM
NA_ROWS = 8
NA_COLS = 16
NA_QCOLS = 16
NA_KCOLS = 2 * NA_COLS
NA_NCB = GRID_W // NA_QCOLS
MLA_HEADS = 8
MLA_NOPE = 64
MLA_ROPE = 32
MLA_V = 64
MLA_Q_LORA = 384
MLA_KV_LORA = 256
MLA_WIDTH = MLA_HEADS * MLA_V
ROPE_BASE = 10000.0
Q_BLOCK = 128
SSM_GROUPS = 16
SSM_GC = 16
SSM_STATE = 64
SSM_WIDTH = SSM_GROUPS * SSM_GC
DT_MIN = 0.001
DT_MAX = 0.1
MIX_WIDTH = NA_WIDTH + MLA_WIDTH + SSM_WIDTH
N_IN = 3 * NA_WIDTH + MLA_Q_LORA + MLA_KV_LORA + MLA_ROPE + SSM_WIDTH
FFN_HIDDEN = -(-8 * D_MODEL // (3 * 256)) * 256
RMS_EPS = 1e-6
NEG_INF = -1e30

kernel_name = 'hybrid_bidir_na_mla_s5_encoder'


def _rms_norm(x, g):
    xf = x.astype(jnp.float32)
    y = xf * lax.rsqrt(jnp.mean(xf * xf, axis=-1, keepdims=True) + RMS_EPS)
    return (y * g.astype(jnp.float32)).astype(x.dtype)


def _rope_tables(length):
    inv = 1.0 / (ROPE_BASE ** (jnp.arange(0, MLA_ROPE, 2, dtype=jnp.float32) / MLA_ROPE))
    ang = jnp.arange(length, dtype=jnp.float32)[:, None] * inv[None, :]
    return jnp.cos(ang), jnp.sin(ang)


def _apply_rope(x, cos, sin):
    xf = x.astype(jnp.float32)
    half = xf.shape[-1] // 2
    x1, x2 = xf[..., :half], xf[..., half:]
    return jnp.concatenate([x1 * cos - x2 * sin, x1 * sin + x2 * cos], axis=-1).astype(x.dtype)


def _neighborhood_attention(q, k, v, rpb):
    b, length, h, d = q.shape
    rows = length // GRID_W
    kh = min(NA_ROWS, rows)
    r = np.arange(rows)
    row_idx = np.clip(r - kh // 2, 0, rows - kh)[:, None] + np.arange(kh)[None, :]
    dr_idx = row_idx - r[:, None] + (NA_ROWS - 1)
    cb = np.arange(NA_NCB)
    kc0 = np.clip(cb * NA_QCOLS - NA_COLS // 2, 0, GRID_W - NA_KCOLS)
    col_idx = kc0[:, None] + np.arange(NA_KCOLS)[None, :]
    qcol = cb[:, None] * NA_QCOLS + np.arange(NA_QCOLS)[None, :]
    cs = np.clip(qcol - NA_COLS // 2, 0, GRID_W - NA_COLS)[:, :, None]
    kcol = col_idx[:, None, :]
    col_mask = (kcol >= cs) & (kcol < cs + NA_COLS)
    dc_idx = np.clip(kcol - qcol[:, :, None], -(NA_COLS - 1), NA_COLS - 1) + (NA_COLS - 1)
    bias = rpb.astype(jnp.float32)[:, dr_idx[:, None, None, :, None], dc_idx[None, :, :, None, :]]
    kg = k.reshape(b, rows, GRID_W, h, d)[:, row_idx][:, :, :, col_idx]
    vg = v.reshape(b, rows, GRID_W, h, d)[:, row_idx][:, :, :, col_idx]
    qg = q.reshape(b, rows, NA_NCB, NA_QCOLS, h, d)
    s = jnp.einsum('brcqhd,brkcjhd->bhrcqkj', qg, kg).astype(jnp.float32) * (d ** -0.5)
    s = jnp.where(col_mask[:, :, None, :], s + bias[None], NEG_INF)
    p = jax.nn.softmax(s.reshape(s.shape[:-2] + (kh * NA_KCOLS,)), axis=-1).reshape(s.shape).astype(v.dtype)
    o = jnp.einsum('bhrcqkj,brkcjhd->brcqhd', p, vg)
    return o.reshape(b, length, h * d)


def _mla_attention(qn, qr, kn, kr, v):
    b, length, h, _ = qn.shape
    nb = length // Q_BLOCK
    scale = (MLA_NOPE + MLA_ROPE) ** -0.5

    def blocks(t):
        return jnp.moveaxis(t.reshape((b, nb, Q_BLOCK) + t.shape[2:]), 1, 0)

    def attend(qs):
        qn_i, qr_i = qs
        s = (jnp.einsum('bqhd,bkhd->bhqk', qn_i, kn)
             + jnp.einsum('bqhr,bkr->bhqk', qr_i, kr)).astype(jnp.float32) * scale
        p = jax.nn.softmax(s, axis=-1).astype(v.dtype)
        return jnp.einsum('bhqk,bkhd->bqhd', p, v)

    o = lax.map(attend, (blocks(qn), blocks(qr)))
    return jnp.moveaxis(o, 0, 1).reshape(b, length, h * MLA_V)


def _linear_recurrence_combine(left, right):
    a_l, b_l = left
    a_r, b_r = right
    return a_r * a_l, a_r * b_l + b_r


def _s5_direction(ug, a_re, a_im, b_re, b_im, c_re, c_im, log_dt, reverse):
    lam = lax.complex(a_re.astype(jnp.float32), a_im.astype(jnp.float32))
    dt = jnp.exp(log_dt.astype(jnp.float32))[:, None]
    lam_bar = jnp.exp(lam * dt)
    b_bar = ((lam_bar - 1.0) / lam)[:, :, None] * lax.complex(b_re.astype(jnp.float32), b_im.astype(jnp.float32))
    bu = jnp.einsum('gpc,blgc->blgp', b_bar, ug.astype(jnp.complex64))
    a = jnp.broadcast_to(lam_bar, bu.shape)
    _, states = lax.associative_scan(_linear_recurrence_combine, (a, bu), reverse=reverse, axis=1)
    c_mat = lax.complex(c_re.astype(jnp.float32), c_im.astype(jnp.float32))
    return jnp.einsum('gcp,blgp->blgc', c_mat, states).real


def _layer(x, c, ada_w, ada_b, norm1_g, w_in, na_q_g, na_k_g, na_rpb, mla_cq_g, mla_ckv_g,
           mla_w_uq, mla_w_ukv, mla_qn_g, mla_kn_g, mla_qr_g, mla_kr_g, ssm_a_re, ssm_a_im,
           ssm_b_re, ssm_b_im, ssm_c_re, ssm_c_im, ssm_log_dt, ssm_d, glu_w, glu_b, w_out,
           norm2_g, ffn_w_gate, ffn_w_up, ffn_w_down):
    b, length, _ = x.shape
    mod = jax.nn.silu(c) @ ada_w + ada_b
    shift1, scale1, gate1, shift2, scale2, gate2 = jnp.split(mod[:, None, :], 6, axis=-1)
    h = _rms_norm(x, norm1_g) * (1 + scale1) + shift1
    z = h @ w_in
    o1 = 3 * NA_WIDTH
    o2 = o1 + MLA_Q_LORA
    o3 = o2 + MLA_KV_LORA
    o4 = o3 + MLA_ROPE
    na_q, na_k, na_v, c_q, c_kv, k_rope, u = jnp.split(z, [NA_WIDTH, 2 * NA_WIDTH, o1, o2, o3, o4], axis=-1)

    hs = (b, length, NA_HEADS, NA_HEAD_DIM)
    y_na = _neighborhood_attention(_rms_norm(na_q.reshape(hs), na_q_g),
                                   _rms_norm(na_k.reshape(hs), na_k_g),
                                   na_v.reshape(hs), na_rpb)

    q = (_rms_norm(c_q, mla_cq_g) @ mla_w_uq).reshape(b, length, MLA_HEADS, MLA_NOPE + MLA_ROPE)
    kv = (_rms_norm(c_kv, mla_ckv_g) @ mla_w_ukv).reshape(b, length, MLA_HEADS, MLA_NOPE + MLA_V)
    cos, sin = _rope_tables(length)
    qn = _rms_norm(q[..., :MLA_NOPE], mla_qn_g)
    qr = _apply_rope(_rms_norm(q[..., MLA_NOPE:], mla_qr_g), cos[:, None, :], sin[:, None, :])
    kn = _rms_norm(kv[..., :MLA_NOPE], mla_kn_g)
    kr = _apply_rope(_rms_norm(k_rope, mla_kr_g), cos, sin)
    y_mla = _mla_attention(qn, qr, kn, kr, kv[..., MLA_NOPE:])

    ug = u.reshape(b, length, SSM_GROUPS, SSM_GC).astype(jnp.float32)
    ys = (_s5_direction(ug, ssm_a_re[0], ssm_a_im[0], ssm_b_re[0], ssm_b_im[0], ssm_c_re[0], ssm_c_im[0], ssm_log_dt[0], False)
          + _s5_direction(ug, ssm_a_re[1], ssm_a_im[1], ssm_b_re[1], ssm_b_im[1], ssm_c_re[1], ssm_c_im[1], ssm_log_dt[1], True)
          + ssm_d.astype(jnp.float32).reshape(SSM_GROUPS, SSM_GC) * ug)
    ys = jax.nn.gelu(ys.reshape(b, length, SSM_WIDTH)).astype(x.dtype)
    y_ssm = ys * jax.nn.sigmoid(ys @ glu_w + glu_b)

    mix = jnp.concatenate([y_na, y_mla, y_ssm], axis=-1) @ w_out
    x = x + gate1 * mix
    h2 = _rms_norm(x, norm2_g) * (1 + scale2) + shift2
    ffn = (jax.nn.silu(h2 @ ffn_w_gate) * (h2 @ ffn_w_up)) @ ffn_w_down
    return x + gate2 * ffn


def _trunk(x, c, params):
    for i in range(DEPTH):
        x = _layer(x, c, *[p[i] for p in params])
    return x


def setup_inputs(seed: int = 0) -> dict:
    key = jax.random.key(seed)
    ks = iter(jax.random.split(key, 40))
    f32 = jnp.float32
    d = D_MODEL

    def nrm(shape, scale):
        return scale * jax.random.normal(next(ks), shape, f32)

    def gain(shape):
        return 1.0 + 0.05 * jax.random.normal(next(ks), shape, f32)

    ssm_shape = (DEPTH, 2, SSM_GROUPS, SSM_STATE)
    return {
        'x_prompt': nrm((BATCH, SEQ, d), 1.0),
        'x_sample': nrm((DEC_BATCH, DEC_SEQ, d), 1.0),
        'c_prompt': nrm((BATCH, d), 1.0),
        'c_sample': nrm((DEC_BATCH, d), 1.0),
        'ada_w': nrm((DEPTH, d, 6 * d), 0.5 * d ** -0.5),
        'ada_b': nrm((DEPTH, 6 * d), 0.01),
        'norm1_g': gain((DEPTH, d)),
        'w_in': nrm((DEPTH, d, N_IN), d ** -0.5),
        'na_q_g': gain((DEPTH, NA_HEAD_DIM)),
        'na_k_g': gain((DEPTH, NA_HEAD_DIM)),
        'na_rpb': nrm((DEPTH, NA_HEADS, 2 * NA_ROWS - 1, 2 * NA_COLS - 1), 0.1),
        'mla_cq_g': gain((DEPTH, MLA_Q_LORA)),
        'mla_ckv_g': gain((DEPTH, MLA_KV_LORA)),
        'mla_w_uq': nrm((DEPTH, MLA_Q_LORA, MLA_HEADS * (MLA_NOPE + MLA_ROPE)), MLA_Q_LORA ** -0.5),
        'mla_w_ukv': nrm((DEPTH, MLA_KV_LORA, MLA_HEADS * (MLA_NOPE + MLA_V)), MLA_KV_LORA ** -0.5),
        'mla_qn_g': gain((DEPTH, MLA_NOPE)),
        'mla_kn_g': gain((DEPTH, MLA_NOPE)),
        'mla_qr_g': gain((DEPTH, MLA_ROPE)),
        'mla_kr_g': gain((DEPTH, MLA_ROPE)),
        'ssm_a_re': -0.5 + nrm(ssm_shape, 0.01),
        'ssm_a_im': jnp.pi * jnp.arange(SSM_STATE, dtype=f32) + nrm(ssm_shape, 0.01),
        'ssm_b_re': nrm(ssm_shape + (SSM_GC,), (2 * SSM_GC) ** -0.5),
        'ssm_b_im': nrm(ssm_shape + (SSM_GC,), (2 * SSM_GC) ** -0.5),
        'ssm_c_re': nrm((DEPTH, 2, SSM_GROUPS, SSM_GC, SSM_STATE), (2 * SSM_STATE) ** -0.5),
        'ssm_c_im': nrm((DEPTH, 2, SSM_GROUPS, SSM_GC, SSM_STATE), (2 * SSM_STATE) ** -0.5),
        'ssm_log_dt': jax.random.uniform(next(ks), (DEPTH, 2, SSM_GROUPS), f32, math.log(DT_MIN), math.log(DT_MAX)),
        'ssm_d': nrm((DEPTH, SSM_WIDTH), 1.0),
        'glu_w': nrm((DEPTH, SSM_WIDTH, SSM_WIDTH), SSM_WIDTH ** -0.5),
        'glu_b': nrm((DEPTH, SSM_WIDTH), 0.01),
        'w_out': nrm((DEPTH, MIX_WIDTH, d), MIX_WIDTH ** -0.5),
        'norm2_g': gain((DEPTH, d)),
        'ffn_w_gate': nrm((DEPTH, d, FFN_HIDDEN), d ** -0.5),
        'ffn_w_up': nrm((DEPTH, d, FFN_HIDDEN), d ** -0.5),
        'ffn_w_down': nrm((DEPTH, FFN_HIDDEN, d), FFN_HIDDEN ** -0.5),
    }


def reference(x_prompt, x_sample, c_prompt, c_sample, ada_w, ada_b, norm1_g, w_in, na_q_g, na_k_g,
              na_rpb, mla_cq_g, mla_ckv_g, mla_w_uq, mla_w_ukv, mla_qn_g, mla_kn_g, mla_qr_g,
              mla_kr_g, ssm_a_re, ssm_a_im, ssm_b_re, ssm_b_im, ssm_c_re, ssm_c_im, ssm_log_dt,
              ssm_d, glu_w, glu_b, w_out, norm2_g, ffn_w_gate, ffn_w_up, ffn_w_down):
    params = (ada_w, ada_b, norm1_g, w_in, na_q_g, na_k_g, na_rpb, mla_cq_g, mla_ckv_g,
              mla_w_uq, mla_w_ukv, mla_qn_g, mla_kn_g, mla_qr_g, mla_kr_g, ssm_a_re, ssm_a_im,
              ssm_b_re, ssm_b_im, ssm_c_re, ssm_c_im, ssm_log_dt, ssm_d, glu_w, glu_b, w_out,
              norm2_g, ffn_w_gate, ffn_w_up, ffn_w_down)
    y_prompt = _trunk(x_prompt, c_prompt, params)
    y_sample = _trunk(x_sample, c_sample, params)
    return (y_prompt, y_sample)
```

```python
import functools
import math

import numpy as np
import jax
import jax.numpy as jnp
from jax import lax
from jax.experimental import pallas as pl
from jax.experimental.pallas import tpu as pltpu

f32 = jnp.float32
bf16 = jnp.bfloat16

D_MODEL = 1024
GRID_W = 64
NA_HEADS = 4
NA_HEAD_DIM = 64
NA_WIDTH = NA_HEADS * NA_HEAD_DIM
NA_ROWS = 8
NA_COLS = 16
MLA_HEADS = 8
MLA_NOPE = 64
MLA_ROPE = 32
MLA_V = 64
MLA_Q_LORA = 384
MLA_KV_LORA = 256
ROPE_BASE = 10000.0
SSM_GROUPS = 16
SSM_GC = 16
SSM_STATE = 64
SSM_WIDTH = SSM_GROUPS * SSM_GC
FFN_HIDDEN = 2816
RMS_EPS = 1e-6
NEG_INF = -1e30

LANES = 128
MLA_PAD = 128
NA_QROWS = 4
NA_WROWS = NA_QROWS + NA_ROWS
TOKEN_TILE = 512
MLA_TQ = 256
MLA_TK = 512
VMEM_LIMIT = 48 * 1024 * 1024

_NT = (((1,), (1,)), ((), ()))


def _cparams(sem):
    return pltpu.CompilerParams(dimension_semantics=sem, vmem_limit_bytes=VMEM_LIMIT)


def _layer_spec(tail, layer):
    n = len(tail)
    return pl.BlockSpec((None,) + tuple(tail), lambda *_: (layer,) + (0,) * n)


def _mod_body(c_ref, w_ref, b_ref, o_ref):
    c = c_ref[...]
    s = c * jax.nn.sigmoid(c)
    o_ref[...] = jnp.dot(s, w_ref[...], precision=lax.Precision.HIGHEST, preferred_element_type=f32) + b_ref[...]


def _modulation(c_all, ada_w, ada_b):
    depth, d, n = ada_w.shape
    tn = 1536
    rows = c_all.shape[0]
    return pl.pallas_call(
        _mod_body,
        grid=(depth, n // tn),
        in_specs=[
            pl.BlockSpec((rows, d), lambda l, j: (0, 0)),
            pl.BlockSpec((None, d, tn), lambda l, j: (l, 0, j)),
            pl.BlockSpec((None, 1, tn), lambda l, j: (l, 0, j)),
        ],
        out_specs=pl.BlockSpec((None, rows, tn), lambda l, j: (l, 0, j)),
        out_shape=jax.ShapeDtypeStruct((depth, rows, n), f32),
        compiler_params=_cparams(("arbitrary", "arbitrary")),
        name="adaln_mod",
    )(c_all, ada_w, ada_b.reshape(depth, 1, n))


def _rope(x, cos, sin_lo, sin_hi):
    return x * cos + pltpu.roll(x, LANES - 16, 1) * sin_lo + pltpu.roll(x, 16, 1) * sin_hi


def _inproj_body(x_ref, mod_ref, g1_ref, wna_ref, wc_ref, wkr_ref, wut_ref, wuq_ref, wukvk_ref, wukvv_ref,
                 gqna_ref, gkna_ref, gcq_ref, gckv_ref, gq_ref, gk_ref, gkr_ref, invq_ref, invk_ref,
                 gna_ref, gqm_ref, cos_ref, sinlo_ref, sinhi_ref,
                 naq_ref, nak_ref, nav_ref, mq_ref, mk_ref, mv_ref, ut_ref):
    x = x_ref[...]
    shift1 = mod_ref[0:1, :]
    scale1 = mod_ref[1:2, :]
    ms = jnp.mean(x * x, axis=-1, keepdims=True)
    h = (x * lax.rsqrt(ms + RMS_EPS) * g1_ref[...]) * (1.0 + scale1) + shift1
    hb = h.astype(bf16)

    z = jnp.dot(hb, wna_ref[...], preferred_element_type=f32)
    q = z[:, :NA_WIDTH]
    k = z[:, NA_WIDTH:2 * NA_WIDTH]
    gna = gna_ref[...]
    ssq = jnp.dot((q * q).astype(bf16), gna, preferred_element_type=f32) * (1.0 / NA_HEAD_DIM)
    ssk = jnp.dot((k * k).astype(bf16), gna, preferred_element_type=f32) * (1.0 / NA_HEAD_DIM)
    naq_ref[...] = (q * lax.rsqrt(ssq + RMS_EPS) * gqna_ref[...]).astype(bf16)
    nak_ref[...] = (k * lax.rsqrt(ssk + RMS_EPS) * gkna_ref[...]).astype(bf16)
    nav_ref[...] = z[:, 2 * NA_WIDTH:].astype(bf16)

    zc = jnp.dot(hb, wc_ref[...], preferred_element_type=f32)
    cq = zc[:, :MLA_Q_LORA]
    ckv = zc[:, MLA_Q_LORA:]
    cqn = (cq * lax.rsqrt(jnp.mean(cq * cq, axis=-1, keepdims=True) + RMS_EPS) * gcq_ref[...]).astype(bf16)
    ckvn = (ckv * lax.rsqrt(jnp.mean(ckv * ckv, axis=-1, keepdims=True) + RMS_EPS) * gckv_ref[...]).astype(bf16)

    cos = cos_ref[...]
    sin_lo = sinlo_ref[...]
    sin_hi = sinhi_ref[...]
    gqm = gqm_ref[...]

    kr = jnp.dot(hb, wkr_ref[...], preferred_element_type=f32)
    kr_ms = jnp.sum(kr * kr, axis=-1, keepdims=True) * (1.0 / MLA_ROPE)
    kr = _rope(kr * lax.rsqrt(kr_ms + RMS_EPS) * gkr_ref[...], cos, sin_lo, sin_hi)

    qraw = jnp.dot(cqn, wuq_ref[...], preferred_element_type=f32)
    kraw = jnp.dot(ckvn, wukvk_ref[...], preferred_element_type=f32)
    for p in range(MLA_HEADS // 2):
        sl = slice(2 * p * MLA_PAD, 2 * (p + 1) * MLA_PAD)
        qs = qraw[:, sl]
        ss = jnp.dot((qs * qs).astype(bf16), gqm, preferred_element_type=f32) * invq_ref[:, sl]
        qn = qs * lax.rsqrt(ss + RMS_EPS) * gq_ref[:, sl]
        ks = kraw[:, sl]
        ss = jnp.dot((ks * ks).astype(bf16), gqm, preferred_element_type=f32) * invk_ref[:, sl]
        kn = ks * lax.rsqrt(ss + RMS_EPS) * gk_ref[:, sl]
        for hh in range(2):
            lo = (2 * p + hh) * MLA_PAD
            piece = qn[:, hh * MLA_PAD:(hh + 1) * MLA_PAD]
            mq_ref[:, lo:lo + MLA_PAD] = _rope(piece, cos, sin_lo, sin_hi).astype(bf16)
            mk_ref[:, lo:lo + MLA_PAD] = (kn[:, hh * MLA_PAD:(hh + 1) * MLA_PAD] + kr).astype(bf16)
    mv_ref[...] = jnp.dot(ckvn, wukvv_ref[...], preferred_element_type=f32).astype(bf16)

    ut = lax.dot_general(wut_ref[...], hb, _NT, preferred_element_type=f32)
    for c in range(ut_ref.shape[0]):
        ut_ref[c] = ut[:, c * LANES:(c + 1) * LANES]


def _inproj(x, mod, layer, P, rope_tabs):
    B, L, D = x.shape
    tm = TOKEN_TILE
    nt = L // tm
    cpt = tm // LANES
    tok = lambda w: pl.BlockSpec((None, tm, w), lambda b, i: (b, i, 0))
    pos = pl.BlockSpec((tm, LANES), lambda b, i: (i, 0))
    const2 = lambda a: pl.BlockSpec(a.shape, lambda b, i: (0, 0))
    lw = lambda name: _layer_spec(P[name].shape[1:], layer)
    names = ["g1", "w_na", "w_c", "w_kr", "w_ut", "w_uq", "w_ukv_k", "w_ukv_v",
             "gq_na", "gk_na", "g_cq", "g_ckv", "gq", "gk", "gkr"]
    consts = ["invq", "invk", "G_na", "G_qm"]
    in_specs = ([tok(D), pl.BlockSpec((None, 6, D), lambda b, i: (b, 0, 0))]
                + [lw(n) for n in names] + [const2(P[n]) for n in consts] + [pos, pos, pos])
    out_shapes = (
        jax.ShapeDtypeStruct((B, L, NA_WIDTH), bf16),
        jax.ShapeDtypeStruct((B, L, NA_WIDTH), bf16),
        jax.ShapeDtypeStruct((B, L, NA_WIDTH), bf16),
        jax.ShapeDtypeStruct((B, L, MLA_HEADS * MLA_PAD), bf16),
        jax.ShapeDtypeStruct((B, L, MLA_HEADS * MLA_PAD), bf16),
        jax.ShapeDtypeStruct((B, L, MLA_HEADS * MLA_V), bf16),
        jax.ShapeDtypeStruct((B * L // LANES, SSM_WIDTH, LANES), f32),
    )
    out_specs = (tok(NA_WIDTH), tok(NA_WIDTH), tok(NA_WIDTH), tok(MLA_HEADS * MLA_PAD), tok(MLA_HEADS * MLA_PAD),
                 tok(MLA_HEADS * MLA_V),
                 pl.BlockSpec((cpt, SSM_WIDTH, LANES), lambda b, i: (b * nt + i, 0, 0)))
    return pl.pallas_call(
        _inproj_body,
        grid=(B, nt),
        in_specs=in_specs,
        out_specs=out_specs,
        out_shape=out_shapes,
        compiler_params=_cparams(("parallel", "parallel")),
        name="in_proj",
    )(x, mod, *[P[n] for n in names], *[P[n] for n in consts], *rope_tabs)


def _na_body(q_ref, k_ref, v_ref, b_ref, o_ref, *, rows):
    i = pl.program_id(1)
    ws = jnp.clip(NA_QROWS * i - NA_ROWS // 2, 0, rows - NA_WROWS)
    start = pl.multiple_of(ws * GRID_W, GRID_W)
    kw = k_ref[pl.ds(start, NA_WROWS * GRID_W), :]
    vw = v_ref[pl.ds(start, NA_WROWS * GRID_W), :]
    q = q_ref[...]
    lane = lax.broadcasted_iota(jnp.int32, (1, NA_WIDTH), 1)
    out = jnp.zeros((NA_QROWS * GRID_W, NA_WIDTH), f32)
    for h in range(NA_HEADS):
        hm = (lane >= h * NA_HEAD_DIM) & (lane < (h + 1) * NA_HEAD_DIM)
        qh = jnp.where(hm, q, jnp.zeros_like(q))
        s = lax.dot_general(qh, kw, _NT, preferred_element_type=f32) + b_ref[h]
        m = jnp.max(s, axis=-1, keepdims=True)
        p = jnp.exp(s - m)
        l = jnp.sum(p, axis=-1, keepdims=True)
        o = jnp.dot(p.astype(bf16), vw, preferred_element_type=f32)
        out = jnp.where(hm, o / l, out)
    o_ref[...] = out.astype(bf16)


def _na_attention(q, k, v, bias, layer):
    B, L, W = q.shape
    rows = L // GRID_W
    nblk = rows // NA_QROWS
    tq = NA_QROWS * GRID_W

    def bias_map(b, i):
        variant = jnp.where(i == 0, 0, jnp.where(i == nblk - 1, 2, 1))
        return (layer, variant, 0, 0, 0)

    full = pl.BlockSpec((None, L, W), lambda b, i: (b, 0, 0))
    return pl.pallas_call(
        functools.partial(_na_body, rows=rows),
        grid=(B, nblk),
        in_specs=[pl.BlockSpec((None, tq, W), lambda b, i: (b, i, 0)), full, full,
                  pl.BlockSpec((None, None, NA_HEADS, tq, NA_WROWS * GRID_W), bias_map)],
        out_specs=pl.BlockSpec((None, tq, W), lambda b, i: (b, i, 0)),
        out_shape=jax.ShapeDtypeStruct((B, L, W), bf16),
        compiler_params=_cparams(("parallel", "arbitrary")),
        name="na_attn",
    )(q, k, v, bias)


def _na_bias_tables(rpb):
    rl = np.arange(NA_QROWS)[:, None]
    kl = np.arange(NA_WROWS)[None, :]
    dr, rvalid = [], []
    for off, lo in ((0, 0), (-NA_ROWS // 2, None), (-NA_ROWS, NA_ROWS // 2)):
        d = off + kl - rl
        if lo is None:
            ok = (d >= -NA_ROWS // 2) & (d < NA_ROWS // 2)
        else:
            ok = (kl >= lo) & (kl < lo + NA_ROWS) & (rl >= 0)
        dr.append(np.clip(d + NA_ROWS - 1, 0, 2 * NA_ROWS - 2))
        rvalid.append(ok)
    dr = np.stack(dr)
    rvalid = np.stack(rvalid)
    qc = np.arange(GRID_W)[:, None]
    kc = np.arange(GRID_W)[None, :]
    cs = np.clip(qc - NA_COLS // 2, 0, GRID_W - NA_COLS)
    cvalid = (kc >= cs) & (kc < cs + NA_COLS)
    dc = np.clip(kc - qc, -(NA_COLS - 1), NA_COLS - 1) + NA_COLS - 1
    dr_i = dr[:, :, None, :, None]
    dc_i = dc[None, None, :, None, :]
    valid = rvalid[:, :, None, :, None] & cvalid[None, None, :, None, :]
    b = rpb.astype(f32)[:, :, dr_i, dc_i]
    b = jnp.where(valid[None, None], b, NEG_INF)
    depth = rpb.shape[0]
    b = b.reshape(depth, NA_HEADS, 3, NA_QROWS * GRID_W, NA_WROWS * GRID_W)
    return jnp.transpose(b, (0, 2, 1, 3, 4))


def _mla_body(q_ref, k_ref, v_ref, o_ref, *, nk):
    q = q_ref[...]
    tq = q.shape[0]
    qs = (q[:, :MLA_PAD], q[:, MLA_PAD:])

    def step(j, carry):
        off = pl.multiple_of(j * MLA_TK, MLA_TK)
        kb = k_ref[pl.ds(off, MLA_TK), :]
        vb = v_ref[pl.ds(off, MLA_TK), :]
        new = []
        for hh in range(2):
            m, l, a = carry[3 * hh:3 * hh + 3]
            s = lax.dot_general(qs[hh], kb[:, hh * MLA_PAD:(hh + 1) * MLA_PAD], _NT, preferred_element_type=f32)
            mn = jnp.maximum(m, jnp.max(s, axis=-1, keepdims=True))
            al = jnp.exp(m - mn)
            p = jnp.exp(s - mn)
            l = al * l + jnp.sum(p, axis=-1, keepdims=True)
            a = al * a + jnp.dot(p.astype(bf16), vb, preferred_element_type=f32)
            new += [mn, l, a]
        return tuple(new)

    init = (jnp.full((tq, 1), -jnp.inf, f32), jnp.zeros((tq, 1), f32), jnp.zeros((tq, 2 * MLA_V), f32)) * 2
    m0, l0, a0, m1, l1, a1 = lax.fori_loop(0, nk, step, init)
    lane = lax.broadcasted_iota(jnp.int32, (1, 2 * MLA_V), 1)
    o_ref[...] = jnp.where(lane < MLA_V, a0 / l0, a1 / l1).astype(bf16)


def _mla_attention(mq, mk, mv):
    B, L, _ = mq.shape
    npair = MLA_HEADS // 2
    return pl.pallas_call(
        functools.partial(_mla_body, nk=L // MLA_TK),
        grid=(B, npair, L // MLA_TQ),
        in_specs=[pl.BlockSpec((None, MLA_TQ, 2 * MLA_PAD), lambda b, p, i: (b, i, p)),
                  pl.BlockSpec((None, L, 2 * MLA_PAD), lambda b, p, i: (b, 0, p)),
                  pl.BlockSpec((None, L, 2 * MLA_V), lambda b, p, i: (b, 0, p))],
        out_specs=pl.BlockSpec((None, MLA_TQ, 2 * MLA_V), lambda b, p, i: (b, i, p)),
        out_shape=jax.ShapeDtypeStruct((B, L, MLA_HEADS * MLA_V), bf16),
        compiler_params=_cparams(("parallel", "parallel", "arbitrary")),
        name="mla_attn",
    )(mq, mk, mv)


def _toeplitz_body(pf_ref, bb_ref, c_ref, o_ref):
    b = bb_ref[...]
    pfr, pfi, pbr, pbi = pf_ref[0], pf_ref[1], pf_ref[2], pf_ref[3]
    bfr, bfi, bbr, bbi = b[:, 0:1], b[:, 1:2], b[:, 2:3], b[:, 3:4]
    rhs = jnp.concatenate([pfr * bfr - pfi * bfi, pfr * bfi + pfi * bfr,
                           pbr * bbr - pbi * bbi, pbr * bbi + pbi * bbr], axis=0)
    kern = jnp.dot(c_ref[...], rhs, precision=lax.Precision.HIGHEST, preferred_element_type=f32)
    for c in range(SSM_GC):
        rowb = jnp.broadcast_to(kern[c:c + 1, :], (LANES, 2 * LANES))
        toe = pltpu.roll(rowb, LANES + 1, 1, stride=1, stride_axis=0)
        o_ref[:, c * LANES:(c + 1) * LANES] = toe[:, :LANES].astype(bf16)


def _toeplitz_gen(pf, bb, cm):
    dg = pf.shape[0]
    n = SSM_GC * LANES
    return pl.pallas_call(
        _toeplitz_body,
        grid=(dg, SSM_GC),
        in_specs=[pl.BlockSpec((None, 4, SSM_STATE, 2 * LANES), lambda g, c: (g, 0, 0, 0)),
                  pl.BlockSpec((None, None, SSM_STATE, 4), lambda g, c: (g, c, 0, 0)),
                  pl.BlockSpec((None, SSM_GC, 4 * SSM_STATE), lambda g, c: (g, 0, 0))],
        out_specs=pl.BlockSpec((None, LANES, n), lambda g, c: (g, c, 0)),
        out_shape=jax.ShapeDtypeStruct((dg, n, n), bf16),
        compiler_params=_cparams(("parallel", "arbitrary")),
        name="s5_toeplitz",
    )(pf, bb, cm)


def _ssm_body(u_ref, t_ref, wv_ref, wc_ref, a_ref, d_ref, o_ref,
              vre, vim, xfre, xfim, xbre, xbim, *, nb, nj):
    m = nb * nj
    u32 = jnp.concatenate([u_ref[:, c, :] for c in range(SSM_GC)], axis=1)
    ub = u32.astype(bf16)
    y = jnp.dot(ub, t_ref[...], preferred_element_type=f32)
    v = jnp.dot(ub, wv_ref[...], preferred_element_type=f32)
    vre[...] = v[:, :LANES]
    vim[...] = v[:, LANES:]
    are = a_ref[0:1, :]
    aim = a_ref[1:2, :]
    is_fwd = lax.broadcasted_iota(jnp.int32, (nb, LANES), 1) < SSM_STATE

    def rows(j):
        return pl.ds(j, nb, stride=nj) if nb > 1 else pl.ds(j, 1)

    def step(k, carry):
        xr, xi = carry
        rf, rb = rows(k), rows(nj - 1 - k)
        xfre[rf, :] = xr
        xfim[rf, :] = xi
        xbre[rb, :] = xr
        xbim[rb, :] = xi
        vr = jnp.where(is_fwd, vre[rf, :], vre[rb, :])
        vi = jnp.where(is_fwd, vim[rf, :], vim[rb, :])
        return are * xr - aim * xi + vr, are * xi + aim * xr + vi

    z = jnp.zeros((nb, LANES), f32)
    lax.fori_loop(0, nj, step, (z, z))
    fwd_m = lax.broadcasted_iota(jnp.int32, (m, LANES), 1) < SSM_STATE
    xin = jnp.concatenate([jnp.where(fwd_m, xfre[...], xbre[...]),
                           jnp.where(fwd_m, xfim[...], xbim[...])], axis=1).astype(bf16)
    y = y + jnp.dot(xin, wc_ref[...], preferred_element_type=f32) + u32 * d_ref[...]
    for c in range(SSM_GC):
        o_ref[:, c, :] = y[:, c * LANES:(c + 1) * LANES]


def _ssm(ut, layer, S, nb, nj):
    m = nb * nj
    n = SSM_GC * LANES
    u4 = ut.reshape(m, SSM_GROUPS, SSM_GC, LANES)
    gspec = lambda tail: pl.BlockSpec((None,) + tail, lambda g: (layer * SSM_GROUPS + g,) + (0,) * len(tail))
    io = pl.BlockSpec((m, None, SSM_GC, LANES), lambda g: (0, g, 0, 0))
    out = pl.pallas_call(
        functools.partial(_ssm_body, nb=nb, nj=nj),
        grid=(SSM_GROUPS,),
        in_specs=[io, gspec((n, n)), gspec((n, 4 * SSM_STATE)), gspec((4 * SSM_STATE, n)),
                  gspec((2, LANES)), gspec((1, n))],
        out_specs=io,
        out_shape=jax.ShapeDtypeStruct((m, SSM_GROUPS, SSM_GC, LANES), f32),
        scratch_shapes=[pltpu.VMEM((m, LANES), f32)] * 6,
        compiler_params=_cparams(("parallel",)),
        name="s5_conv",
    )(u4, S["toe"], S["wv"], S["wc"], S["a"], S["d"])
    return out.reshape(m, SSM_WIDTH, LANES)


def _ssm_tables(a_re, a_im, b_re, b_im, c_re, c_im, log_dt, ssm_d):
    depth = a_re.shape[0]
    dg = depth * SSM_GROUPS
    lam = lax.complex(a_re.astype(f32), a_im.astype(f32))
    dt = jnp.exp(log_dt.astype(f32))[..., None]
    lam_dt = lam * dt
    lam_bar = jnp.exp(lam_dt)
    b_bar = ((lam_bar - 1.0) / lam)[..., None] * lax.complex(b_re.astype(f32), b_im.astype(f32))
    cmat = lax.complex(c_re.astype(f32), c_im.astype(f32))
    steps = jnp.arange(LANES + 1, dtype=f32)
    pw = jnp.exp(lam_dt[..., None] * steps)
    pwf, pwb = pw[:, 0], pw[:, 1]

    e = np.arange(2 * LANES)
    f_ok = (e >= LANES - 1) & (e <= 2 * LANES - 2)
    b_ok = e <= LANES - 1
    pf_f = jnp.where(f_ok, pwf[..., np.clip(e - (LANES - 1), 0, LANES - 1)], 0.0)
    pf_b = jnp.where(b_ok, pwb[..., np.clip(LANES - 1 - e, 0, LANES - 1)], 0.0)
    pf = jnp.stack([pf_f.real, pf_f.imag, pf_b.real, pf_b.imag], axis=2)
    pf = pf.reshape(dg, 4, SSM_STATE, 2 * LANES)

    bbf, bbb = b_bar[:, 0], b_bar[:, 1]
    bb = jnp.stack([bbf.real, bbf.imag, bbb.real, bbb.imag], axis=-1)
    bb = jnp.transpose(bb, (0, 1, 3, 2, 4)).reshape(dg, SSM_GC, SSM_STATE, 4)

    cf, cb = cmat[:, 0], cmat[:, 1]
    cm = jnp.concatenate([cf.real, -cf.imag, cb.real, -cb.imag], axis=-1).reshape(dg, SSM_GC, 4 * SSM_STATE)

    s_idx = np.arange(LANES)
    cof = pwf[..., LANES - 1 - s_idx][..., None, :, :] * jnp.swapaxes(bbf, -1, -2)[..., None]
    cob = pwb[..., s_idx][..., None, :, :] * jnp.swapaxes(bbb, -1, -2)[..., None]
    cof = jnp.swapaxes(cof, -1, -2)
    cob = jnp.swapaxes(cob, -1, -2)
    wv = jnp.concatenate([cof.real, cob.real, cof.imag, cob.imag], axis=-1)
    wv = wv.reshape(dg, SSM_GC * LANES, 4 * SSM_STATE).astype(bf16)

    t_idx = np.arange(LANES)
    clf = cf[..., None] * pwf[..., None, :, :][..., t_idx + 1]
    clb = cb[..., None] * pwb[..., None, :, :][..., LANES - t_idx]
    to_rows = lambda a: jnp.transpose(a, (0, 1, 3, 2, 4)).reshape(dg, SSM_STATE, SSM_GC * LANES)
    wc = jnp.concatenate([to_rows(clf.real), to_rows(clb.real), to_rows(-clf.imag), to_rows(-clb.imag)], axis=1)
    wc = wc.astype(bf16)

    a128 = jnp.concatenate([pwf[..., LANES], pwb[..., LANES]], axis=-1)
    a = jnp.stack([a128.real, a128.imag], axis=2).reshape(dg, 2, LANES)
    d = jnp.repeat(ssm_d.astype(f32).reshape(dg, SSM_GC), LANES, axis=-1).reshape(dg, 1, SSM_GC * LANES)
    return dict(pf=pf, bb=bb, cm=cm, wv=wv, wc=wc, a=a, d=d)


def _outproj_body(x_ref, mod_ref, yna_ref, ymla_ref, yst_ref, gluw_ref, glub_ref, wo_ref, g2_ref,
                  x1_ref, h2_ref):
    ys = jnp.concatenate([yst_ref[c].T for c in range(yst_ref.shape[0])], axis=0)
    ys = 0.5 * ys * (1.0 + jnp.tanh(math.sqrt(2.0 / math.pi) * (ys + 0.044715 * (ys * ys * ys))))
    gate = jax.nn.sigmoid(jnp.dot(ys.astype(bf16), gluw_ref[...], preferred_element_type=f32) + glub_ref[...])
    yssm = (ys * gate).astype(bf16)
    o1 = NA_WIDTH
    o2 = NA_WIDTH + MLA_HEADS * MLA_V
    mix = (jnp.dot(yna_ref[...], wo_ref[0:o1, :], preferred_element_type=f32)
           + jnp.dot(ymla_ref[...], wo_ref[o1:o2, :], preferred_element_type=f32)
           + jnp.dot(yssm, wo_ref[o2:, :], preferred_element_type=f32))
    x1 = x_ref[...] + mod_ref[2:3, :] * mix
    x1_ref[...] = x1
    ms = jnp.mean(x1 * x1, axis=-1, keepdims=True)
    h2 = (x1 * lax.rsqrt(ms + RMS_EPS) * g2_ref[...]) * (1.0 + mod_ref[4:5, :]) + mod_ref[3:4, :]
    h2_ref[...] = h2.astype(bf16)


def _outproj(x, mod, yna, ymla, yst, layer, P):
    B, L, D = x.shape
    tm = TOKEN_TILE
    nt = L // tm
    cpt = tm // LANES
    tok = lambda w: pl.BlockSpec((None, tm, w), lambda b, i: (b, i, 0))
    lw = lambda name: _layer_spec(P[name].shape[1:], layer)
    return pl.pallas_call(
        _outproj_body,
        grid=(B, nt),
        in_specs=[tok(D), pl.BlockSpec((None, 6, D), lambda b, i: (b, 0, 0)), tok(NA_WIDTH), tok(MLA_HEADS * MLA_V),
                  pl.BlockSpec((cpt, SSM_WIDTH, LANES), lambda b, i: (b * nt + i, 0, 0)),
                  lw("glu_w"), lw("glu_b"), lw("w_out"), lw("g2")],
        out_specs=(tok(D), tok(D)),
        out_shape=(jax.ShapeDtypeStruct((B, L, D), f32), jax.ShapeDtypeStruct((B, L, D), bf16)),
        compiler_params=_cparams(("parallel", "parallel")),
        name="out_proj",
    )(x, mod, yna, ymla, yst, P["glu_w"], P["glu_b"], P["w_out"], P["g2"])


def _ffn_body(x1_ref, h2_ref, mod_ref, wg_ref, wu_ref, wd_ref, o_ref, act_ref):
    h2 = h2_ref[...]
    th = 256
    for c in range(FFN_HIDDEN // th):
        cs = slice(c * th, (c + 1) * th)
        g = jnp.dot(h2, wg_ref[:, cs], preferred_element_type=f32)
        u = jnp.dot(h2, wu_ref[:, cs], preferred_element_type=f32)
        act_ref[:, cs] = (g * jax.nn.sigmoid(g) * u).astype(bf16)
    ffn = jnp.dot(act_ref[...], wd_ref[...], preferred_element_type=f32)
    o_ref[...] = x1_ref[...] + mod_ref[5:6, :] * ffn


def _ffn(x1, h2, mod, layer, P):
    B, L, D = x1.shape
    tm = TOKEN_TILE
    tok = pl.BlockSpec((None, tm, D), lambda b, i: (b, i, 0))

    def wspec(name):
        tail = P[name].shape[1:]
        return pl.BlockSpec((None,) + tail, lambda b, i: (layer, 0, 0), pipeline_mode=pl.Buffered(1))

    return pl.pallas_call(
        _ffn_body,
        grid=(B, L // tm),
        in_specs=[tok, tok, pl.BlockSpec((None, 6, D), lambda b, i: (b, 0, 0)),
                  wspec("w_gate"), wspec("w_up"), wspec("w_down")],
        out_specs=tok,
        out_shape=jax.ShapeDtypeStruct((B, L, D), f32),
        scratch_shapes=[pltpu.VMEM((tm, FFN_HIDDEN), bf16)],
        compiler_params=_cparams(("parallel", "parallel")),
        name="ffn",
    )(x1, h2, mod, P["w_gate"], P["w_up"], P["w_down"])


def _prep_params(norm1_g, w_in, na_q_g, na_k_g, mla_cq_g, mla_ckv_g, mla_w_uq, mla_w_ukv, mla_qn_g, mla_kn_g,
                 mla_qr_g, mla_kr_g, glu_w, glu_b, w_out, norm2_g, ffn_w_gate, ffn_w_up, ffn_w_down):
    depth = w_in.shape[0]
    o1 = 3 * NA_WIDTH
    o2 = o1 + MLA_Q_LORA
    o3 = o2 + MLA_KV_LORA
    o4 = o3 + MLA_ROPE
    row = lambda a: a.astype(f32)[:, None, :]
    zeros = lambda *s: jnp.zeros(s, f32)
    P = {}
    P["g1"] = row(norm1_g)
    P["g2"] = row(norm2_g)
    P["w_na"] = w_in[:, :, :o1].astype(bf16)
    P["w_c"] = w_in[:, :, o1:o3].astype(bf16)
    P["w_kr"] = jnp.concatenate([zeros(depth, D_MODEL, MLA_NOPE), w_in[:, :, o3:o4],
                                 zeros(depth, D_MODEL, MLA_PAD - MLA_NOPE - MLA_ROPE)], axis=-1).astype(bf16)
    P["w_ut"] = jnp.swapaxes(w_in[:, :, o4:], 1, 2).astype(bf16)
    wq = mla_w_uq.reshape(depth, MLA_Q_LORA, MLA_HEADS, MLA_NOPE + MLA_ROPE)
    wq = jnp.concatenate([wq, zeros(depth, MLA_Q_LORA, MLA_HEADS, MLA_PAD - MLA_NOPE - MLA_ROPE)], axis=-1)
    P["w_uq"] = wq.reshape(depth, MLA_Q_LORA, MLA_HEADS * MLA_PAD).astype(bf16)
    wkv = mla_w_ukv.reshape(depth, MLA_KV_LORA, MLA_HEADS, MLA_NOPE + MLA_V)
    wk = jnp.concatenate([wkv[..., :MLA_NOPE], zeros(depth, MLA_KV_LORA, MLA_HEADS, MLA_PAD - MLA_NOPE)], axis=-1)
    P["w_ukv_k"] = wk.reshape(depth, MLA_KV_LORA, MLA_HEADS * MLA_PAD).astype(bf16)
    P["w_ukv_v"] = wkv[..., MLA_NOPE:].reshape(depth, MLA_KV_LORA, MLA_HEADS * MLA_V).astype(bf16)
    P["gq_na"] = row(jnp.tile(na_q_g, (1, NA_HEADS))) * (NA_HEAD_DIM ** -0.5)
    P["gk_na"] = row(jnp.tile(na_k_g, (1, NA_HEADS)))
    P["g_cq"] = row(mla_cq_g)
    P["g_ckv"] = row(mla_ckv_g)
    scale = (MLA_NOPE + MLA_ROPE) ** -0.5
    pad = MLA_PAD - MLA_NOPE - MLA_ROPE
    gq = jnp.concatenate([mla_qn_g, mla_qr_g, zeros(depth, pad)], axis=-1) * scale
    P["gq"] = row(jnp.tile(gq, (1, MLA_HEADS)))
    gk = jnp.concatenate([mla_kn_g, zeros(depth, MLA_PAD - MLA_NOPE)], axis=-1)
    P["gk"] = row(jnp.tile(gk, (1, MLA_HEADS)))
    P["gkr"] = row(jnp.concatenate([zeros(depth, MLA_NOPE), mla_kr_g, zeros(depth, pad)], axis=-1))
    invq = np.concatenate([np.full(MLA_NOPE, 1.0 / MLA_NOPE), np.full(MLA_ROPE, 1.0 / MLA_ROPE), np.zeros(pad)])
    invk = np.concatenate([np.full(MLA_NOPE, 1.0 / MLA_NOPE), np.zeros(MLA_PAD - MLA_NOPE)])
    P["invq"] = jnp.asarray(np.tile(invq, MLA_HEADS)[None, :], f32)
    P["invk"] = jnp.asarray(np.tile(invk, MLA_HEADS)[None, :], f32)
    lane = np.arange(NA_WIDTH)
    P["G_na"] = jnp.asarray((lane[:, None] // NA_HEAD_DIM) == (lane[None, :] // NA_HEAD_DIM), bf16)
    lane = np.arange(2 * MLA_PAD)
    grp = np.where(lane % MLA_PAD < MLA_NOPE, 0, np.where(lane % MLA_PAD < MLA_NOPE + MLA_ROPE, 1, 2))
    same = (lane[:, None] // MLA_PAD == lane[None, :] // MLA_PAD) & (grp[:, None] == grp[None, :]) & (grp[:, None] < 2)
    P["G_qm"] = jnp.asarray(same, bf16)
    P["glu_w"] = glu_w.astype(bf16)
    P["glu_b"] = row(glu_b)
    P["w_out"] = w_out.astype(bf16)
    P["w_gate"] = ffn_w_gate.astype(bf16)
    P["w_up"] = ffn_w_up.astype(bf16)
    P["w_down"] = ffn_w_down.astype(bf16)
    return P


def _rope_tables(length):
    inv = 1.0 / (ROPE_BASE ** (jnp.arange(0, MLA_ROPE, 2, dtype=f32) / MLA_ROPE))
    ang = jnp.arange(length, dtype=f32)[:, None] * inv[None, :]
    cos, sin = jnp.cos(ang), jnp.sin(ang)
    half = MLA_ROPE // 2
    z = lambda w: jnp.zeros((length, w), f32)
    pad = MLA_PAD - MLA_NOPE - MLA_ROPE
    cos_t = jnp.concatenate([jnp.ones((length, MLA_NOPE), f32), cos, cos, z(pad)], axis=-1)
    sin_lo = jnp.concatenate([z(MLA_NOPE), -sin, z(half), z(pad)], axis=-1)
    sin_hi = jnp.concatenate([z(MLA_NOPE), z(half), sin, z(pad)], axis=-1)
    return cos_t, sin_lo, sin_hi


def _trunk(x, mods, P, S, bias):
    B, L, _ = x.shape
    rope_tabs = _rope_tables(L)
    nj = L // LANES
    for layer in range(len(mods)):
        mod = mods[layer]
        naq, nak, nav, mq, mk, mv, ut = _inproj(x, mod, layer, P, rope_tabs)
        yna = _na_attention(naq, nak, nav, bias, layer)
        ymla = _mla_attention(mq, mk, mv)
        yst = _ssm(ut, layer, S, B, nj)
        x1, h2 = _outproj(x, mod, yna, ymla, yst, layer, P)
        x = _ffn(x1, h2, mod, layer, P)
    return x


def kernel(x_prompt, x_sample, c_prompt, c_sample, ada_w, ada_b, norm1_g, w_in, na_q_g, na_k_g, na_rpb, mla_cq_g,
           mla_ckv_g, mla_w_uq, mla_w_ukv, mla_qn_g, mla_kn_g, mla_qr_g, mla_kr_g, ssm_a_re, ssm_a_im, ssm_b_re,
           ssm_b_im, ssm_c_re, ssm_c_im, ssm_log_dt, ssm_d, glu_w, glu_b, w_out, norm2_g, ffn_w_gate, ffn_w_up,
           ffn_w_down):
    depth = w_in.shape[0]
    nbp, nbs = c_prompt.shape[0], c_sample.shape[0]
    rows = -(-(nbp + nbs) // 8) * 8
    c_all = jnp.concatenate([c_prompt, c_sample, jnp.zeros((rows - nbp - nbs, D_MODEL), f32)], axis=0)
    mod = _modulation(c_all, ada_w, ada_b).reshape(depth, rows, 6, D_MODEL)
    mods_p = [mod[l, :nbp] for l in range(depth)]
    mods_s = [mod[l, nbp:nbp + nbs] for l in range(depth)]

    P = _prep_params(norm1_g, w_in, na_q_g, na_k_g, mla_cq_g, mla_ckv_g, mla_w_uq, mla_w_ukv, mla_qn_g, mla_kn_g,
                     mla_qr_g, mla_kr_g, glu_w, glu_b, w_out, norm2_g, ffn_w_gate, ffn_w_up, ffn_w_down)
    S = _ssm_tables(ssm_a_re, ssm_a_im, ssm_b_re, ssm_b_im, ssm_c_re, ssm_c_im, ssm_log_dt, ssm_d)
    S["toe"] = _toeplitz_gen(S["pf"], S["bb"], S["cm"])
    bias = _na_bias_tables(na_rpb)

    y_prompt = _trunk(x_prompt, mods_p, P, S, bias)
    y_sample = _trunk(x_sample, mods_s, P, S, bias)
    return (y_prompt, y_sample)
```

```python
import functools
import math

import numpy as np
import jax
import jax.numpy as jnp
from jax import lax
from jax.experimental import pallas as pl
from jax.experimental.pallas import tpu as pltpu

f32 = jnp.float32
bf16 = jnp.bfloat16

D_MODEL = 1024
GRID_W = 64
NA_HEADS = 4
NA_HEAD_DIM = 64
NA_WIDTH = NA_HEADS * NA_HEAD_DIM
NA_ROWS = 8
NA_COLS = 16
MLA_HEADS = 8
MLA_NOPE = 64
MLA_ROPE = 32
MLA_V = 64
MLA_Q_LORA = 384
MLA_KV_LORA = 256
ROPE_BASE = 10000.0
SSM_GROUPS = 16
SSM_GC = 16
SSM_STATE = 64
SSM_WIDTH = SSM_GROUPS * SSM_GC
FFN_HIDDEN = 2816
RMS_EPS = 1e-6
NEG_INF = -1e30

LANES = 128
MLA_PAD = 128
NA_QROWS = 4
NA_WROWS = NA_QROWS + NA_ROWS
NA_BIAS_LANE0 = 200
TOKEN_TILE = 512
MLA_TQ = 256
MLA_TK = 1024
MLA_VROWS = MLA_V + 16
VMEM_LIMIT = 48 * 1024 * 1024

_NT = (((1,), (1,)), ((), ()))


def _cparams(sem):
    return pltpu.CompilerParams(dimension_semantics=sem, vmem_limit_bytes=VMEM_LIMIT)


def _layer_spec(tail, layer):
    n = len(tail)
    return pl.BlockSpec((None,) + tuple(tail), lambda *_: (layer,) + (0,) * n)


def _mod_body(c_ref, w_ref, b_ref, o_ref):
    c = c_ref[...]
    s = c * jax.nn.sigmoid(c)
    o_ref[...] = jnp.dot(s, w_ref[...], precision=lax.Precision.HIGHEST, preferred_element_type=f32) + b_ref[...]


def _modulation(c_all, ada_w, ada_b):
    depth, d, n = ada_w.shape
    tn = 1536
    rows = c_all.shape[0]
    return pl.pallas_call(
        _mod_body,
        grid=(depth, n // tn),
        in_specs=[
            pl.BlockSpec((rows, d), lambda l, j: (0, 0)),
            pl.BlockSpec((None, d, tn), lambda l, j: (l, 0, j)),
            pl.BlockSpec((None, 1, tn), lambda l, j: (l, 0, j)),
        ],
        out_specs=pl.BlockSpec((None, rows, tn), lambda l, j: (l, 0, j)),
        out_shape=jax.ShapeDtypeStruct((depth, rows, n), f32),
        compiler_params=_cparams(("arbitrary", "arbitrary")),
        name="adaln_mod",
    )(c_all, ada_w, ada_b.reshape(depth, 1, n))


def _rope(x, cos, sin_lo, sin_hi):
    return x * cos + pltpu.roll(x, LANES - 16, 1) * sin_lo + pltpu.roll(x, 16, 1) * sin_hi


def _inproj_body(x_ref, mod_ref, g1_ref, wna_ref, wc_ref, wkr_ref, wut_ref, wuq_ref, wukvk_ref, wukvvt_ref,
                 gqna_ref, gkna_ref, gcq_ref, gckv_ref, gq_ref, gk_ref, gkr_ref, invq_ref, invk_ref,
                 gna_ref, gqm_ref, cos_ref, sinlo_ref, sinhi_ref,
                 naq_ref, nak_ref, nav_ref, mq_ref, mk_ref, mvt_ref, ut_ref):
    x = x_ref[...]
    shift1 = mod_ref[0:1, :]
    scale1 = mod_ref[1:2, :]
    ms = jnp.mean(x * x, axis=-1, keepdims=True)
    h = (x * lax.rsqrt(ms + RMS_EPS) * g1_ref[...]) * (1.0 + scale1) + shift1
    hb = h.astype(bf16)

    z = jnp.dot(hb, wna_ref[...], preferred_element_type=f32)
    q = z[:, :NA_WIDTH]
    k = z[:, NA_WIDTH:2 * NA_WIDTH]
    gna = gna_ref[...]
    ssq = jnp.dot((q * q).astype(bf16), gna, preferred_element_type=f32) * (1.0 / NA_HEAD_DIM)
    ssk = jnp.dot((k * k).astype(bf16), gna, preferred_element_type=f32) * (1.0 / NA_HEAD_DIM)
    naq_ref[...] = (q * lax.rsqrt(ssq + RMS_EPS) * gqna_ref[...]).astype(bf16)
    nak_ref[...] = (k * lax.rsqrt(ssk + RMS_EPS) * gkna_ref[...]).astype(bf16)
    nav_ref[...] = z[:, 2 * NA_WIDTH:].astype(bf16)

    zc = jnp.dot(hb, wc_ref[...], preferred_element_type=f32)
    cq = zc[:, :MLA_Q_LORA]
    ckv = zc[:, MLA_Q_LORA:]
    cqn = (cq * lax.rsqrt(jnp.mean(cq * cq, axis=-1, keepdims=True) + RMS_EPS) * gcq_ref[...]).astype(bf16)
    ckvn = (ckv * lax.rsqrt(jnp.mean(ckv * ckv, axis=-1, keepdims=True) + RMS_EPS) * gckv_ref[...]).astype(bf16)

    cos = cos_ref[...]
    sin_lo = sinlo_ref[...]
    sin_hi = sinhi_ref[...]
    gqm = gqm_ref[...]

    kr = jnp.dot(hb, wkr_ref[...], preferred_element_type=f32)
    kr_ms = jnp.sum(kr * kr, axis=-1, keepdims=True) * (1.0 / MLA_ROPE)
    kr = _rope(kr * lax.rsqrt(kr_ms + RMS_EPS) * gkr_ref[...], cos, sin_lo, sin_hi)

    qraw = jnp.dot(cqn, wuq_ref[...], preferred_element_type=f32)
    kraw = jnp.dot(ckvn, wukvk_ref[...], preferred_element_type=f32)
    for p in range(MLA_HEADS // 2):
        sl = slice(2 * p * MLA_PAD, 2 * (p + 1) * MLA_PAD)
        qs = qraw[:, sl]
        ss = jnp.dot((qs * qs).astype(bf16), gqm, preferred_element_type=f32) * invq_ref[:, sl]
        qn = qs * lax.rsqrt(ss + RMS_EPS) * gq_ref[:, sl]
        ks = kraw[:, sl]
        ss = jnp.dot((ks * ks).astype(bf16), gqm, preferred_element_type=f32) * invk_ref[:, sl]
        kn = ks * lax.rsqrt(ss + RMS_EPS) * gk_ref[:, sl]
        for hh in range(2):
            lo = (2 * p + hh) * MLA_PAD
            piece = qn[:, hh * MLA_PAD:(hh + 1) * MLA_PAD]
            mq_ref[:, lo:lo + MLA_PAD] = _rope(piece, cos, sin_lo, sin_hi).astype(bf16)
            mk_ref[:, lo:lo + MLA_PAD] = (kn[:, hh * MLA_PAD:(hh + 1) * MLA_PAD] + kr).astype(bf16)
    mvt_ref[...] = lax.dot_general(wukvvt_ref[...], ckvn, _NT, preferred_element_type=f32).astype(bf16)

    ut = lax.dot_general(wut_ref[...], hb, _NT, preferred_element_type=f32)
    for c in range(ut_ref.shape[0]):
        ut_ref[c] = ut[:, c * LANES:(c + 1) * LANES]


def _inproj(x, mod, layer, P, rope_tabs):
    B, L, D = x.shape
    tm = TOKEN_TILE
    nt = L // tm
    cpt = tm // LANES
    tok = lambda w: pl.BlockSpec((None, tm, w), lambda b, i: (b, i, 0))
    pos = pl.BlockSpec((tm, LANES), lambda b, i: (i, 0))
    const2 = lambda a: pl.BlockSpec(a.shape, lambda b, i: (0, 0))
    lw = lambda name: _layer_spec(P[name].shape[1:], layer)
    names = ["g1", "w_na", "w_c", "w_kr", "w_ut", "w_uq", "w_ukv_k", "w_ukv_vt",
             "gq_na", "gk_na", "g_cq", "g_ckv", "gq", "gk", "gkr"]
    consts = ["invq", "invk", "G_na", "G_qm"]
    in_specs = ([tok(D), pl.BlockSpec((None, 6, D), lambda b, i: (b, 0, 0))]
                + [lw(n) for n in names] + [const2(P[n]) for n in consts] + [pos, pos, pos])
    out_shapes = (
        jax.ShapeDtypeStruct((B, L, NA_WIDTH), bf16),
        jax.ShapeDtypeStruct((B, L, NA_WIDTH), bf16),
        jax.ShapeDtypeStruct((B, L, NA_WIDTH), bf16),
        jax.ShapeDtypeStruct((B, L, MLA_HEADS * MLA_PAD), bf16),
        jax.ShapeDtypeStruct((B, L, MLA_HEADS * MLA_PAD), bf16),
        jax.ShapeDtypeStruct((B, MLA_HEADS * MLA_V, L), bf16),
        jax.ShapeDtypeStruct((B * L // LANES, SSM_WIDTH, LANES), f32),
    )
    out_specs = (tok(NA_WIDTH), tok(NA_WIDTH), tok(NA_WIDTH), tok(MLA_HEADS * MLA_PAD), tok(MLA_HEADS * MLA_PAD),
                 pl.BlockSpec((None, MLA_HEADS * MLA_V, tm), lambda b, i: (b, 0, i)),
                 pl.BlockSpec((cpt, SSM_WIDTH, LANES), lambda b, i: (b * nt + i, 0, 0)))
    return pl.pallas_call(
        _inproj_body,
        grid=(B, nt),
        in_specs=in_specs,
        out_specs=out_specs,
        out_shape=out_shapes,
        compiler_params=_cparams(("parallel", "parallel")),
        name="in_proj",
    )(x, mod, *[P[n] for n in names], *[P[n] for n in consts], *rope_tabs)


def _na_body(q_ref, k_ref, v_ref, b_ref, o_ref, *, rows):
    i = pl.program_id(1)
    ws = jnp.clip(NA_QROWS * i - NA_ROWS // 2, 0, rows - NA_WROWS)
    start = pl.multiple_of(ws * GRID_W, GRID_W)
    kw = k_ref[pl.ds(start, NA_WROWS * GRID_W), :]
    vw = v_ref[pl.ds(start, NA_WROWS * GRID_W), :]
    q = q_ref[...]
    lane = lax.broadcasted_iota(jnp.int32, (1, NA_WIDTH), 1)
    out = jnp.zeros((NA_QROWS * GRID_W, NA_WIDTH), f32)
    for h in range(NA_HEADS):
        hm = (lane >= h * NA_HEAD_DIM) & (lane < (h + 1) * NA_HEAD_DIM)
        qh = jnp.where(hm, q, jnp.zeros_like(q))
        s = lax.dot_general(qh, kw, _NT, preferred_element_type=f32) + b_ref[h]
        m = jnp.max(s, axis=-1, keepdims=True)
        p = jnp.exp(s - m)
        l = jnp.sum(p, axis=-1, keepdims=True)
        o = jnp.dot(p.astype(bf16), vw, preferred_element_type=f32)
        out = jnp.where(hm, o / l, out)
    o_ref[...] = out.astype(bf16)


def _na_attention(q, k, v, bias, layer):
    B, L, W = q.shape
    rows = L // GRID_W
    nblk = rows // NA_QROWS
    tq = NA_QROWS * GRID_W

    def bias_map(b, i):
        variant = jnp.where(i == 0, 0, jnp.where(i == nblk - 1, 2, 1))
        return (layer, variant, 0, 0, 0)

    full = pl.BlockSpec((None, L, W), lambda b, i: (b, 0, 0))
    return pl.pallas_call(
        functools.partial(_na_body, rows=rows),
        grid=(B, nblk),
        in_specs=[pl.BlockSpec((None, tq, W), lambda b, i: (b, i, 0)), full, full,
                  pl.BlockSpec((None, None, NA_HEADS, tq, NA_WROWS * GRID_W), bias_map)],
        out_specs=pl.BlockSpec((None, tq, W), lambda b, i: (b, i, 0)),
        out_shape=jax.ShapeDtypeStruct((B, L, W), bf16),
        compiler_params=_cparams(("parallel", "arbitrary")),
        name="na_attn",
    )(q, k, v, bias)


def _na_bias_body(r_ref, o_ref):
    r = r_ref[...]
    qc = lax.broadcasted_iota(jnp.int32, (GRID_W, LANES), 0)
    lane = lax.broadcasted_iota(jnp.int32, (GRID_W, LANES), 1)
    lo_half = lane < GRID_W
    kc = jnp.where(lo_half, lane, lane - GRID_W)
    cs = jnp.clip(qc - NA_COLS // 2, 0, GRID_W - NA_COLS)
    cvalid = (kc >= cs) & (kc < cs + NA_COLS)
    neg = jnp.full((GRID_W, LANES), NEG_INF, f32)
    tiles = []
    for dr in range(2 * NA_ROWS - 1):
        rowb = jnp.broadcast_to(r[dr:dr + 1, :], (GRID_W, 2 * LANES))
        base = 2 * LANES - (NA_COLS - 1) - NA_BIAS_LANE0
        t_lo = pltpu.roll(rowb, base, 1, stride=1, stride_axis=0)[:, :LANES]
        t_hi = pltpu.roll(rowb, base + GRID_W, 1, stride=1, stride_axis=0)[:, :LANES]
        tiles.append(jnp.where(cvalid, jnp.where(lo_half, t_lo, t_hi), neg))
    for v, (off, lo) in enumerate(((0, 0), (-(NA_ROWS // 2), None), (-NA_ROWS, NA_ROWS // 2))):
        for rl in range(NA_QROWS):
            cols = []
            for kl in range(NA_WROWS):
                d = off + kl - rl
                ok = (-(NA_ROWS // 2) <= d < NA_ROWS // 2) if lo is None else (lo <= kl < lo + NA_ROWS)
                cols.append(tiles[d + NA_ROWS - 1] if ok else neg)
            pairs = [jnp.where(lo_half, cols[2 * j], cols[2 * j + 1]) for j in range(NA_WROWS // 2)]
            o_ref[v, rl * GRID_W:(rl + 1) * GRID_W, :] = jnp.concatenate(pairs, axis=1)


def _na_bias_tables(rpb):
    depth, nh, nr, nc = rpb.shape
    rp = jnp.pad(rpb.astype(f32),
                 ((0, 0), (0, 0), (0, 16 - nr), (NA_BIAS_LANE0, 2 * LANES - NA_BIAS_LANE0 - nc)))
    tq, tkw = NA_QROWS * GRID_W, NA_WROWS * GRID_W
    return pl.pallas_call(
        _na_bias_body,
        grid=(depth, nh),
        in_specs=[pl.BlockSpec((None, None, 16, 2 * LANES), lambda l, h: (l, h, 0, 0))],
        out_specs=pl.BlockSpec((None, 3, None, tq, tkw), lambda l, h: (l, 0, h, 0, 0)),
        out_shape=jax.ShapeDtypeStruct((depth, 3, nh, tq, tkw), f32),
        compiler_params=_cparams(("parallel", "parallel")),
        name="na_bias",
    )(rp)


def _mla_body(q_ref, k_ref, vt_ref, o_ref, s_scr, *, nk):
    tq = q_ref.shape[0]
    tk = MLA_TK
    qt = q_ref[...].astype(f32).T.astype(bf16)
    qts = (qt[:MLA_PAD], qt[MLA_PAD:])
    ones = jnp.ones((MLA_VROWS - MLA_V, tk), bf16)

    def scores(j, slot):
        off = pl.multiple_of(j * tk, tk)
        kb = k_ref[pl.ds(off, tk), :]
        cmax = []
        for h in range(2):
            s = jnp.dot(kb[:, h * MLA_PAD:(h + 1) * MLA_PAD], qts[h], preferred_element_type=f32)
            s_scr[slot, h] = s
            cmax.append(jnp.max(s, axis=0, keepdims=True))
        return tuple(cmax)

    def accumulate(j, slot, cmax, state):
        off = pl.multiple_of(j * tk, tk)
        vt = vt_ref[:, pl.ds(off, tk)]
        new = []
        for h in range(2):
            m, acc = state[2 * h], state[2 * h + 1]
            mn = jnp.maximum(m, cmax[h])
            al = jnp.exp2(m - mn)
            p = jnp.exp2(s_scr[slot, h] - mn).astype(bf16)
            vte = jnp.concatenate([vt[h * MLA_V:(h + 1) * MLA_V], ones], axis=0)
            acc = al * acc + jnp.dot(vte, p, preferred_element_type=f32)
            new += [mn, acc]
        return tuple(new)

    def double_step(i, carry):
        cmax, state = carry[:2], carry[2:]
        c1 = scores(2 * i + 1, 1)
        state = accumulate(2 * i, 0, cmax, state)
        c2 = scores(2 * i + 2, 0)
        state = accumulate(2 * i + 1, 1, c1, state)
        return c2 + state

    m0 = jnp.full((1, tq), -jnp.inf, f32)
    a0 = jnp.zeros((MLA_VROWS, tq), f32)
    carry = scores(0, 0) + (m0, a0, m0, a0)
    carry = lax.fori_loop(0, nk // 2 - 1, double_step, carry)
    cmax, state = carry[:2], carry[2:]
    c1 = scores(nk - 1, 1)
    state = accumulate(nk - 2, 0, cmax, state)
    state = accumulate(nk - 1, 1, c1, state)
    ot = jnp.concatenate([state[1][:MLA_V] / state[1][MLA_V:MLA_V + 1],
                          state[3][:MLA_V] / state[3][MLA_V:MLA_V + 1]], axis=0)
    o_ref[...] = ot.T.astype(bf16)


def _mla_attention(mq, mk, mvt):
    B, L, _ = mq.shape
    npair = MLA_HEADS // 2
    nk = L // MLA_TK
    assert nk % 2 == 0
    return pl.pallas_call(
        functools.partial(_mla_body, nk=nk),
        grid=(B, npair, L // MLA_TQ),
        in_specs=[pl.BlockSpec((None, MLA_TQ, 2 * MLA_PAD), lambda b, p, i: (b, i, p)),
                  pl.BlockSpec((None, L, 2 * MLA_PAD), lambda b, p, i: (b, 0, p)),
                  pl.BlockSpec((None, 2 * MLA_V, L), lambda b, p, i: (b, p, 0))],
        out_specs=pl.BlockSpec((None, MLA_TQ, 2 * MLA_V), lambda b, p, i: (b, i, p)),
        out_shape=jax.ShapeDtypeStruct((B, L, MLA_HEADS * MLA_V), bf16),
        scratch_shapes=[pltpu.VMEM((2, 2, MLA_TK, MLA_TQ), f32)],
        compiler_params=_cparams(("parallel", "parallel", "arbitrary")),
        name="mla_attn",
    )(mq, mk, mvt)


def _toeplitz_body(pf_ref, bb_ref, c_ref, o_ref):
    b = bb_ref[...]
    pfr, pfi, pbr, pbi = pf_ref[0], pf_ref[1], pf_ref[2], pf_ref[3]
    bfr, bfi, bbr, bbi = b[:, 0:1], b[:, 1:2], b[:, 2:3], b[:, 3:4]
    rhs = jnp.concatenate([pfr * bfr - pfi * bfi, pfr * bfi + pfi * bfr,
                           pbr * bbr - pbi * bbi, pbr * bbi + pbi * bbr], axis=0)
    kern = jnp.dot(c_ref[...], rhs, precision=lax.Precision.HIGHEST, preferred_element_type=f32)
    for c in range(SSM_GC):
        rowb = jnp.broadcast_to(kern[c:c + 1, :], (LANES, 2 * LANES))
        toe = pltpu.roll(rowb, LANES + 1, 1, stride=1, stride_axis=0)
        o_ref[:, c * LANES:(c + 1) * LANES] = toe[:, :LANES].astype(bf16)


def _toeplitz_gen(pf, bb, cm):
    dg = pf.shape[0]
    n = SSM_GC * LANES
    return pl.pallas_call(
        _toeplitz_body,
        grid=(dg, SSM_GC),
        in_specs=[pl.BlockSpec((None, 4, SSM_STATE, 2 * LANES), lambda g, c: (g, 0, 0, 0)),
                  pl.BlockSpec((None, None, SSM_STATE, 4), lambda g, c: (g, c, 0, 0)),
                  pl.BlockSpec((None, SSM_GC, 4 * SSM_STATE), lambda g, c: (g, 0, 0))],
        out_specs=pl.BlockSpec((None, LANES, n), lambda g, c: (g, c, 0)),
        out_shape=jax.ShapeDtypeStruct((dg, n, n), bf16),
        compiler_params=_cparams(("parallel", "arbitrary")),
        name="s5_toeplitz",
    )(pf, bb, cm)


def _ssm_body(u_ref, t_ref, wv_ref, wc_ref, a_ref, d_ref, o_ref,
              vre, vim, xfre, xfim, xbre, xbim, *, nb, nj):
    m = nb * nj
    u32 = jnp.concatenate([u_ref[:, c, :] for c in range(SSM_GC)], axis=1)
    ub = u32.astype(bf16)
    y = jnp.dot(ub, t_ref[...], preferred_element_type=f32)
    v = jnp.dot(ub, wv_ref[...], preferred_element_type=f32)
    vre[...] = v[:, :LANES]
    vim[...] = v[:, LANES:]
    are = a_ref[0:1, :]
    aim = a_ref[1:2, :]
    is_fwd = lax.broadcasted_iota(jnp.int32, (nb, LANES), 1) < SSM_STATE

    def rows(j):
        return pl.ds(j, nb, stride=nj) if nb > 1 else pl.ds(j, 1)

    def step(k, carry):
        xr, xi = carry
        rf, rb = rows(k), rows(nj - 1 - k)
        xfre[rf, :] = xr
        xfim[rf, :] = xi
        xbre[rb, :] = xr
        xbim[rb, :] = xi
        vr = jnp.where(is_fwd, vre[rf, :], vre[rb, :])
        vi = jnp.where(is_fwd, vim[rf, :], vim[rb, :])
        return are * xr - aim * xi + vr, are * xi + aim * xr + vi

    z = jnp.zeros((nb, LANES), f32)
    lax.fori_loop(0, nj, step, (z, z))
    fwd_m = lax.broadcasted_iota(jnp.int32, (m, LANES), 1) < SSM_STATE
    xin = jnp.concatenate([jnp.where(fwd_m, xfre[...], xbre[...]),
                           jnp.where(fwd_m, xfim[...], xbim[...])], axis=1).astype(bf16)
    y = y + jnp.dot(xin, wc_ref[...], preferred_element_type=f32) + u32 * d_ref[...]
    for c in range(SSM_GC):
        o_ref[:, c, :] = y[:, c * LANES:(c + 1) * LANES]


def _ssm(ut, layer, S, nb, nj):
    m = nb * nj
    n = SSM_GC * LANES
    u4 = ut.reshape(m, SSM_GROUPS, SSM_GC, LANES)
    gspec = lambda tail: pl.BlockSpec((None,) + tail, lambda g: (layer * SSM_GROUPS + g,) + (0,) * len(tail))
    io = pl.BlockSpec((m, None, SSM_GC, LANES), lambda g: (0, g, 0, 0))
    out = pl.pallas_call(
        functools.partial(_ssm_body, nb=nb, nj=nj),
        grid=(SSM_GROUPS,),
        in_specs=[io, gspec((n, n)), gspec((n, 4 * SSM_STATE)), gspec((4 * SSM_STATE, n)),
                  gspec((2, LANES)), gspec((1, n))],
        out_specs=io,
        out_shape=jax.ShapeDtypeStruct((m, SSM_GROUPS, SSM_GC, LANES), f32),
        scratch_shapes=[pltpu.VMEM((m, LANES), f32)] * 6,
        compiler_params=_cparams(("parallel",)),
        name="s5_conv",
    )(u4, S["toe"], S["wv"], S["wc"], S["a"], S["d"])
    return out.reshape(m, SSM_WIDTH, LANES)


def _ssm_tables(a_re, a_im, b_re, b_im, c_re, c_im, log_dt, ssm_d):
    depth = a_re.shape[0]
    dg = depth * SSM_GROUPS
    lam = lax.complex(a_re.astype(f32), a_im.astype(f32))
    dt = jnp.exp(log_dt.astype(f32))[..., None]
    lam_dt = lam * dt
    lam_bar = jnp.exp(lam_dt)
    b_bar = ((lam_bar - 1.0) / lam)[..., None] * lax.complex(b_re.astype(f32), b_im.astype(f32))
    cmat = lax.complex(c_re.astype(f32), c_im.astype(f32))
    steps = jnp.arange(LANES + 1, dtype=f32)
    pw = jnp.exp(lam_dt[..., None] * steps)
    pwf, pwb = pw[:, 0], pw[:, 1]

    e = np.arange(2 * LANES)
    f_ok = (e >= LANES - 1) & (e <= 2 * LANES - 2)
    b_ok = e <= LANES - 1
    pf_f = jnp.where(f_ok, pwf[..., np.clip(e - (LANES - 1), 0, LANES - 1)], 0.0)
    pf_b = jnp.where(b_ok, pwb[..., np.clip(LANES - 1 - e, 0, LANES - 1)], 0.0)
    pf = jnp.stack([pf_f.real, pf_f.imag, pf_b.real, pf_b.imag], axis=2)
    pf = pf.reshape(dg, 4, SSM_STATE, 2 * LANES)

    bbf, bbb = b_bar[:, 0], b_bar[:, 1]
    bb = jnp.stack([bbf.real, bbf.imag, bbb.real, bbb.imag], axis=-1)
    bb = jnp.transpose(bb, (0, 1, 3, 2, 4)).reshape(dg, SSM_GC, SSM_STATE, 4)

    cf, cb = cmat[:, 0], cmat[:, 1]
    cm = jnp.concatenate([cf.real, -cf.imag, cb.real, -cb.imag], axis=-1).reshape(dg, SSM_GC, 4 * SSM_STATE)

    s_idx = np.arange(LANES)
    cof = pwf[..., LANES - 1 - s_idx][..., None, :, :] * jnp.swapaxes(bbf, -1, -2)[..., None]
    cob = pwb[..., s_idx][..., None, :, :] * jnp.swapaxes(bbb, -1, -2)[..., None]
    cof = jnp.swapaxes(cof, -1, -2)
    cob = jnp.swapaxes(cob, -1, -2)
    wv = jnp.concatenate([cof.real, cob.real, cof.imag, cob.imag], axis=-1)
    wv = wv.reshape(dg, SSM_GC * LANES, 4 * SSM_STATE).astype(bf16)

    t_idx = np.arange(LANES)
    clf = cf[..., None] * pwf[..., None, :, :][..., t_idx + 1]
    clb = cb[..., None] * pwb[..., None, :, :][..., LANES - t_idx]
    to_rows = lambda a: jnp.transpose(a, (0, 1, 3, 2, 4)).reshape(dg, SSM_STATE, SSM_GC * LANES)
    wc = jnp.concatenate([to_rows(clf.real), to_rows(clb.real), to_rows(-clf.imag), to_rows(-clb.imag)], axis=1)
    wc = wc.astype(bf16)

    a128 = jnp.concatenate([pwf[..., LANES], pwb[..., LANES]], axis=-1)
    a = jnp.stack([a128.real, a128.imag], axis=2).reshape(dg, 2, LANES)
    d = jnp.repeat(ssm_d.astype(f32).reshape(dg, SSM_GC), LANES, axis=-1).reshape(dg, 1, SSM_GC * LANES)
    return dict(pf=pf, bb=bb, cm=cm, wv=wv, wc=wc, a=a, d=d)


def _outproj_body(x_ref, mod_ref, yna_ref, ymla_ref, yst_ref, gluw_ref, glub_ref, wo_ref, g2_ref,
                  x1_ref, h2_ref):
    ys = jnp.concatenate([yst_ref[c].T for c in range(yst_ref.shape[0])], axis=0)
    ys = 0.5 * ys * (1.0 + jnp.tanh(math.sqrt(2.0 / math.pi) * (ys + 0.044715 * (ys * ys * ys))))
    gate = jax.nn.sigmoid(jnp.dot(ys.astype(bf16), gluw_ref[...], preferred_element_type=f32) + glub_ref[...])
    yssm = (ys * gate).astype(bf16)
    o1 = NA_WIDTH
    o2 = NA_WIDTH + MLA_HEADS * MLA_V
    mix = (jnp.dot(yna_ref[...], wo_ref[0:o1, :], preferred_element_type=f32)
           + jnp.dot(ymla_ref[...], wo_ref[o1:o2, :], preferred_element_type=f32)
           + jnp.dot(yssm, wo_ref[o2:, :], preferred_element_type=f32))
    x1 = x_ref[...] + mod_ref[2:3, :] * mix
    x1_ref[...] = x1
    ms = jnp.mean(x1 * x1, axis=-1, keepdims=True)
    h2 = (x1 * lax.rsqrt(ms + RMS_EPS) * g2_ref[...]) * (1.0 + mod_ref[4:5, :]) + mod_ref[3:4, :]
    h2_ref[...] = h2.astype(bf16)


def _outproj(x, mod, yna, ymla, yst, layer, P):
    B, L, D = x.shape
    tm = TOKEN_TILE
    nt = L // tm
    cpt = tm // LANES
    tok = lambda w: pl.BlockSpec((None, tm, w), lambda b, i: (b, i, 0))
    lw = lambda name: _layer_spec(P[name].shape[1:], layer)
    return pl.pallas_call(
        _outproj_body,
        grid=(B, nt),
        in_specs=[tok(D), pl.BlockSpec((None, 6, D), lambda b, i: (b, 0, 0)), tok(NA_WIDTH), tok(MLA_HEADS * MLA_V),
                  pl.BlockSpec((cpt, SSM_WIDTH, LANES), lambda b, i: (b * nt + i, 0, 0)),
                  lw("glu_w"), lw("glu_b"), lw("w_out"), lw("g2")],
        out_specs=(tok(D), tok(D)),
        out_shape=(jax.ShapeDtypeStruct((B, L, D), f32), jax.ShapeDtypeStruct((B, L, D), bf16)),
        compiler_params=_cparams(("parallel", "parallel")),
        name="out_proj",
    )(x, mod, yna, ymla, yst, P["glu_w"], P["glu_b"], P["w_out"], P["g2"])


def _ffn_body(x1_ref, h2_ref, mod_ref, wg_ref, wu_ref, wd_ref, o_ref, act_ref):
    h2 = h2_ref[...]
    th = 256
    for c in range(FFN_HIDDEN // th):
        cs = slice(c * th, (c + 1) * th)
        g = jnp.dot(h2, wg_ref[:, cs], preferred_element_type=f32)
        u = jnp.dot(h2, wu_ref[:, cs], preferred_element_type=f32)
        act_ref[:, cs] = (g * jax.nn.sigmoid(g) * u).astype(bf16)
    ffn = jnp.dot(act_ref[...], wd_ref[...], preferred_element_type=f32)
    o_ref[...] = x1_ref[...] + mod_ref[5:6, :] * ffn


def _ffn(x1, h2, mod, layer, P):
    B, L, D = x1.shape
    tm = TOKEN_TILE
    tok = pl.BlockSpec((None, tm, D), lambda b, i: (b, i, 0))

    def wspec(name):
        tail = P[name].shape[1:]
        return pl.BlockSpec((None,) + tail, lambda b, i: (layer, 0, 0), pipeline_mode=pl.Buffered(1))

    return pl.pallas_call(
        _ffn_body,
        grid=(B, L // tm),
        in_specs=[tok, tok, pl.BlockSpec((None, 6, D), lambda b, i: (b, 0, 0)),
                  wspec("w_gate"), wspec("w_up"), wspec("w_down")],
        out_specs=tok,
        out_shape=jax.ShapeDtypeStruct((B, L, D), f32),
        scratch_shapes=[pltpu.VMEM((tm, FFN_HIDDEN), bf16)],
        compiler_params=_cparams(("parallel", "parallel")),
        name="ffn",
    )(x1, h2, mod, P["w_gate"], P["w_up"], P["w_down"])


def _prep_params(norm1_g, w_in, na_q_g, na_k_g, mla_cq_g, mla_ckv_g, mla_w_uq, mla_w_ukv, mla_qn_g, mla_kn_g,
                 mla_qr_g, mla_kr_g, glu_w, glu_b, w_out, norm2_g, ffn_w_gate, ffn_w_up, ffn_w_down):
    depth = w_in.shape[0]
    o1 = 3 * NA_WIDTH
    o2 = o1 + MLA_Q_LORA
    o3 = o2 + MLA_KV_LORA
    o4 = o3 + MLA_ROPE
    row = lambda a: a.astype(f32)[:, None, :]
    zeros = lambda *s: jnp.zeros(s, f32)
    P = {}
    P["g1"] = row(norm1_g)
    P["g2"] = row(norm2_g)
    P["w_na"] = w_in[:, :, :o1].astype(bf16)
    P["w_c"] = w_in[:, :, o1:o3].astype(bf16)
    P["w_kr"] = jnp.concatenate([zeros(depth, D_MODEL, MLA_NOPE), w_in[:, :, o3:o4],
                                 zeros(depth, D_MODEL, MLA_PAD - MLA_NOPE - MLA_ROPE)], axis=-1).astype(bf16)
    P["w_ut"] = jnp.swapaxes(w_in[:, :, o4:], 1, 2).astype(bf16)
    wq = mla_w_uq.reshape(depth, MLA_Q_LORA, MLA_HEADS, MLA_NOPE + MLA_ROPE)
    wq = jnp.concatenate([wq, zeros(depth, MLA_Q_LORA, MLA_HEADS, MLA_PAD - MLA_NOPE - MLA_ROPE)], axis=-1)
    P["w_uq"] = wq.reshape(depth, MLA_Q_LORA, MLA_HEADS * MLA_PAD).astype(bf16)
    wkv = mla_w_ukv.reshape(depth, MLA_KV_LORA, MLA_HEADS, MLA_NOPE + MLA_V)
    wk = jnp.concatenate([wkv[..., :MLA_NOPE], zeros(depth, MLA_KV_LORA, MLA_HEADS, MLA_PAD - MLA_NOPE)], axis=-1)
    P["w_ukv_k"] = wk.reshape(depth, MLA_KV_LORA, MLA_HEADS * MLA_PAD).astype(bf16)
    P["w_ukv_vt"] = jnp.swapaxes(wkv[..., MLA_NOPE:].reshape(depth, MLA_KV_LORA, MLA_HEADS * MLA_V), 1, 2).astype(bf16)
    P["gq_na"] = row(jnp.tile(na_q_g, (1, NA_HEADS))) * (NA_HEAD_DIM ** -0.5)
    P["gk_na"] = row(jnp.tile(na_k_g, (1, NA_HEADS)))
    P["g_cq"] = row(mla_cq_g)
    P["g_ckv"] = row(mla_ckv_g)
    scale = (MLA_NOPE + MLA_ROPE) ** -0.5 * math.log2(math.e)
    pad = MLA_PAD - MLA_NOPE - MLA_ROPE
    gq = jnp.concatenate([mla_qn_g, mla_qr_g, zeros(depth, pad)], axis=-1) * scale
    P["gq"] = row(jnp.tile(gq, (1, MLA_HEADS)))
    gk = jnp.concatenate([mla_kn_g, zeros(depth, MLA_PAD - MLA_NOPE)], axis=-1)
    P["gk"] = row(jnp.tile(gk, (1, MLA_HEADS)))
    P["gkr"] = row(jnp.concatenate([zeros(depth, MLA_NOPE), mla_kr_g, zeros(depth, pad)], axis=-1))
    invq = np.concatenate([np.full(MLA_NOPE, 1.0 / MLA_NOPE), np.full(MLA_ROPE, 1.0 / MLA_ROPE), np.zeros(pad)])
    invk = np.concatenate([np.full(MLA_NOPE, 1.0 / MLA_NOPE), np.zeros(MLA_PAD - MLA_NOPE)])
    P["invq"] = jnp.asarray(np.tile(invq, MLA_HEADS)[None, :], f32)
    P["invk"] = jnp.asarray(np.tile(invk, MLA_HEADS)[None, :], f32)
    lane = np.arange(NA_WIDTH)
    P["G_na"] = jnp.asarray((lane[:, None] // NA_HEAD_DIM) == (lane[None, :] // NA_HEAD_DIM), bf16)
    lane = np.arange(2 * MLA_PAD)
    grp = np.where(lane % MLA_PAD < MLA_NOPE, 0, np.where(lane % MLA_PAD < MLA_NOPE + MLA_ROPE, 1, 2))
    same = (lane[:, None] // MLA_PAD == lane[None, :] // MLA_PAD) & (grp[:, None] == grp[None, :]) & (grp[:, None] < 2)
    P["G_qm"] = jnp.asarray(same, bf16)
    P["glu_w"] = glu_w.astype(bf16)
    P["glu_b"] = row(glu_b)
    P["w_out"] = w_out.astype(bf16)
    P["w_gate"] = ffn_w_gate.astype(bf16)
    P["w_up"] = ffn_w_up.astype(bf16)
    P["w_down"] = ffn_w_down.astype(bf16)
    return P


def _rope_tables(length):
    inv = 1.0 / (ROPE_BASE ** (jnp.arange(0, MLA_ROPE, 2, dtype=f32) / MLA_ROPE))
    ang = jnp.arange(length, dtype=f32)[:, None] * inv[None, :]
    cos, sin = jnp.cos(ang), jnp.sin(ang)
    half = MLA_ROPE // 2
    z = lambda w: jnp.zeros((length, w), f32)
    pad = MLA_PAD - MLA_NOPE - MLA_ROPE
    cos_t = jnp.concatenate([jnp.ones((length, MLA_NOPE), f32), cos, cos, z(pad)], axis=-1)
    sin_lo = jnp.concatenate([z(MLA_NOPE), -sin, z(half), z(pad)], axis=-1)
    sin_hi = jnp.concatenate([z(MLA_NOPE), z(half), sin, z(pad)], axis=-1)
    return cos_t, sin_lo, sin_hi


def _trunk(x, mods, P, S, bias):
    B, L, _ = x.shape
    rope_tabs = _rope_tables(L)
    nj = L // LANES
    for layer in range(len(mods)):
        mod = mods[layer]
        naq, nak, nav, mq, mk, mvt, ut = _inproj(x, mod, layer, P, rope_tabs)
        yna = _na_attention(naq, nak, nav, bias, layer)
        ymla = _mla_attention(mq, mk, mvt)
        yst = _ssm(ut, layer, S, B, nj)
        x1, h2 = _outproj(x, mod, yna, ymla, yst, layer, P)
        x = _ffn(x1, h2, mod, layer, P)
    return x


def kernel(x_prompt, x_sample, c_prompt, c_sample, ada_w, ada_b, norm1_g, w_in, na_q_g, na_k_g, na_rpb, mla_cq_g,
           mla_ckv_g, mla_w_uq, mla_w_ukv, mla_qn_g, mla_kn_g, mla_qr_g, mla_kr_g, ssm_a_re, ssm_a_im, ssm_b_re,
           ssm_b_im, ssm_c_re, ssm_c_im, ssm_log_dt, ssm_d, glu_w, glu_b, w_out, norm2_g, ffn_w_gate, ffn_w_up,
           ffn_w_down):
    depth = w_in.shape[0]
    nbp, nbs = c_prompt.shape[0], c_sample.shape[0]
    rows = -(-(nbp + nbs) // 8) * 8
    c_all = jnp.concatenate([c_prompt, c_sample, jnp.zeros((rows - nbp - nbs, D_MODEL), f32)], axis=0)
    mod = _modulation(c_all, ada_w, ada_b).reshape(depth, rows, 6, D_MODEL)
    mods_p = [mod[l, :nbp] for l in range(depth)]
    mods_s = [mod[l, nbp:nbp + nbs] for l in range(depth)]

    P = _prep_params(norm1_g, w_in, na_q_g, na_k_g, mla_cq_g, mla_ckv_g, mla_w_uq, mla_w_ukv, mla_qn_g, mla_kn_g,
                     mla_qr_g, mla_kr_g, glu_w, glu_b, w_out, norm2_g, ffn_w_gate, ffn_w_up, ffn_w_down)
    S = _ssm_tables(ssm_a_re, ssm_a_im, ssm_b_re, ssm_b_im, ssm_c_re, ssm_c_im, ssm_log_dt, ssm_d)
    S["toe"] = _toeplitz_gen(S["pf"], S["bb"], S["cm"])
    bias = _na_bias_tables(na_rpb)

    y_prompt = _trunk(x_prompt, mods_p, P, S, bias)
    y_sample = _trunk(x_sample, mods_s, P, S, bias)
    return (y_prompt, y_sample)
```

```python
import functools
import math

import numpy as np
import jax
import jax.numpy as jnp
from jax import lax
from jax.experimental import pallas as pl
from jax.experimental.pallas import tpu as pltpu

f32 = jnp.float32
bf16 = jnp.bfloat16

D_MODEL = 1024
GRID_W = 64
NA_HEADS = 4
NA_HEAD_DIM = 64
NA_WIDTH = NA_HEADS * NA_HEAD_DIM
NA_ROWS = 8
NA_COLS = 16
MLA_HEADS = 8
MLA_NOPE = 64
MLA_ROPE = 32
MLA_V = 64
MLA_Q_LORA = 384
MLA_KV_LORA = 256
ROPE_BASE = 10000.0
SSM_GROUPS = 16
SSM_GC = 16
SSM_STATE = 64
SSM_WIDTH = SSM_GROUPS * SSM_GC
FFN_HIDDEN = 2816
RMS_EPS = 1e-6
NEG_INF = -1e30

LANES = 128
MLA_PAD = 128
NA_QROWS = 4
NA_WROWS = NA_QROWS + NA_ROWS
NA_BIAS_LANE0 = 200
TOKEN_TILE = 512
MLA_TQ = 256
MLA_TK = 1024
MLA_AHEAD = 2
MLA_SLOTS = 2 * MLA_AHEAD
MLA_VROWS = MLA_V + 16
VMEM_LIMIT = 48 * 1024 * 1024

_NT = (((1,), (1,)), ((), ()))
_TN = (((0,), (0,)), ((), ()))


def _cparams(sem):
    return pltpu.CompilerParams(dimension_semantics=sem, vmem_limit_bytes=VMEM_LIMIT)


def _layer_spec(tail, layer):
    n = len(tail)
    return pl.BlockSpec((None,) + tuple(tail), lambda *_: (layer,) + (0,) * n)


def _mod_body(c_ref, w_ref, b_ref, o_ref):
    c = c_ref[...]
    s = c * jax.nn.sigmoid(c)
    o_ref[...] = jnp.dot(s, w_ref[...], precision=lax.Precision.HIGHEST, preferred_element_type=f32) + b_ref[...]


def _modulation(c_all, ada_w, ada_b):
    depth, d, n = ada_w.shape
    tn = 1536
    rows = c_all.shape[0]
    return pl.pallas_call(
        _mod_body,
        grid=(depth, n // tn),
        in_specs=[
            pl.BlockSpec((rows, d), lambda l, j: (0, 0)),
            pl.BlockSpec((None, d, tn), lambda l, j: (l, 0, j)),
            pl.BlockSpec((None, 1, tn), lambda l, j: (l, 0, j)),
        ],
        out_specs=pl.BlockSpec((None, rows, tn), lambda l, j: (l, 0, j)),
        out_shape=jax.ShapeDtypeStruct((depth, rows, n), f32),
        compiler_params=_cparams(("arbitrary", "arbitrary")),
        name="adaln_mod",
    )(c_all, ada_w, ada_b.reshape(depth, 1, n))


def _rope(x, cos, sin_lo, sin_hi):
    return x * cos + pltpu.roll(x, LANES - 16, 1) * sin_lo + pltpu.roll(x, 16, 1) * sin_hi


def _inproj_body(x_ref, mod_ref, g1_ref, wna_ref, wc_ref, wkr_ref, wut_ref, wuq_ref, wukvk_ref, wukvvt_ref,
                 gqna_ref, gkna_ref, gcq_ref, gckv_ref, gq_ref, gk_ref, gkr_ref, invq_ref, invk_ref,
                 gna_ref, gqm_ref, cos_ref, sinlo_ref, sinhi_ref,
                 naq_ref, nak_ref, nav_ref, mqt_ref, mk_ref, mvt_ref, ut_ref):
    x = x_ref[...]
    shift1 = mod_ref[0:1, :]
    scale1 = mod_ref[1:2, :]
    ms = jnp.mean(x * x, axis=-1, keepdims=True)
    h = (x * lax.rsqrt(ms + RMS_EPS) * g1_ref[...]) * (1.0 + scale1) + shift1
    hb = h.astype(bf16)

    z = jnp.dot(hb, wna_ref[...], preferred_element_type=f32)
    q = z[:, :NA_WIDTH]
    k = z[:, NA_WIDTH:2 * NA_WIDTH]
    gna = gna_ref[...]
    ssq = jnp.dot((q * q).astype(bf16), gna, preferred_element_type=f32) * (1.0 / NA_HEAD_DIM)
    ssk = jnp.dot((k * k).astype(bf16), gna, preferred_element_type=f32) * (1.0 / NA_HEAD_DIM)
    naq_ref[...] = (q * lax.rsqrt(ssq + RMS_EPS) * gqna_ref[...]).astype(bf16)
    nak_ref[...] = (k * lax.rsqrt(ssk + RMS_EPS) * gkna_ref[...]).astype(bf16)
    nav_ref[...] = z[:, 2 * NA_WIDTH:].astype(bf16)

    zc = jnp.dot(hb, wc_ref[...], preferred_element_type=f32)
    cq = zc[:, :MLA_Q_LORA]
    ckv = zc[:, MLA_Q_LORA:]
    cqn = (cq * lax.rsqrt(jnp.mean(cq * cq, axis=-1, keepdims=True) + RMS_EPS) * gcq_ref[...]).astype(bf16)
    ckvn = (ckv * lax.rsqrt(jnp.mean(ckv * ckv, axis=-1, keepdims=True) + RMS_EPS) * gckv_ref[...]).astype(bf16)

    cos = cos_ref[...]
    sin_lo = sinlo_ref[...]
    sin_hi = sinhi_ref[...]
    gqm = gqm_ref[...]

    kr = jnp.dot(hb, wkr_ref[...], preferred_element_type=f32)
    kr_ms = jnp.sum(kr * kr, axis=-1, keepdims=True) * (1.0 / MLA_ROPE)
    kr = _rope(kr * lax.rsqrt(kr_ms + RMS_EPS) * gkr_ref[...], cos, sin_lo, sin_hi)

    qraw = jnp.dot(cqn, wuq_ref[...], preferred_element_type=f32)
    kraw = jnp.dot(ckvn, wukvk_ref[...], preferred_element_type=f32)
    for p in range(MLA_HEADS // 2):
        sl = slice(2 * p * MLA_PAD, 2 * (p + 1) * MLA_PAD)
        qs = qraw[:, sl]
        ss = jnp.dot((qs * qs).astype(bf16), gqm, preferred_element_type=f32) * invq_ref[:, sl]
        qn = qs * lax.rsqrt(ss + RMS_EPS) * gq_ref[:, sl]
        ks = kraw[:, sl]
        ss = jnp.dot((ks * ks).astype(bf16), gqm, preferred_element_type=f32) * invk_ref[:, sl]
        kn = ks * lax.rsqrt(ss + RMS_EPS) * gk_ref[:, sl]
        for hh in range(2):
            lo = (2 * p + hh) * MLA_PAD
            piece = qn[:, hh * MLA_PAD:(hh + 1) * MLA_PAD]
            mqt_ref[lo:lo + MLA_PAD, :] = _rope(piece, cos, sin_lo, sin_hi).T.astype(bf16)
            mk_ref[:, lo:lo + MLA_PAD] = (kn[:, hh * MLA_PAD:(hh + 1) * MLA_PAD] + kr).astype(bf16)
    mvt_ref[...] = lax.dot_general(wukvvt_ref[...], ckvn, _NT, preferred_element_type=f32).astype(bf16)

    ut = lax.dot_general(wut_ref[...], hb, _NT, preferred_element_type=f32)
    for c in range(ut_ref.shape[0]):
        ut_ref[c] = ut[:, c * LANES:(c + 1) * LANES]


def _inproj(x, mod, layer, P, rope_tabs):
    B, L, D = x.shape
    tm = TOKEN_TILE
    nt = L // tm
    cpt = tm // LANES
    tok = lambda w: pl.BlockSpec((None, tm, w), lambda b, i: (b, i, 0))
    pos = pl.BlockSpec((tm, LANES), lambda b, i: (i, 0))
    const2 = lambda a: pl.BlockSpec(a.shape, lambda b, i: (0, 0))
    lw = lambda name: _layer_spec(P[name].shape[1:], layer)
    names = ["g1", "w_na", "w_c", "w_kr", "w_ut", "w_uq", "w_ukv_k", "w_ukv_vt",
             "gq_na", "gk_na", "g_cq", "g_ckv", "gq", "gk", "gkr"]
    consts = ["invq", "invk", "G_na", "G_qm"]
    in_specs = ([tok(D), pl.BlockSpec((None, 6, D), lambda b, i: (b, 0, 0))]
                + [lw(n) for n in names] + [const2(P[n]) for n in consts] + [pos, pos, pos])
    out_shapes = (
        jax.ShapeDtypeStruct((B, L, NA_WIDTH), bf16),
        jax.ShapeDtypeStruct((B, L, NA_WIDTH), bf16),
        jax.ShapeDtypeStruct((B, L, NA_WIDTH), bf16),
        jax.ShapeDtypeStruct((B, MLA_HEADS * MLA_PAD, L), bf16),
        jax.ShapeDtypeStruct((B, L, MLA_HEADS * MLA_PAD), bf16),
        jax.ShapeDtypeStruct((B, MLA_HEADS * MLA_V, L), bf16),
        jax.ShapeDtypeStruct((B * L // LANES, SSM_WIDTH, LANES), f32),
    )
    chan = lambda w: pl.BlockSpec((None, w, tm), lambda b, i: (b, 0, i))
    out_specs = (tok(NA_WIDTH), tok(NA_WIDTH), tok(NA_WIDTH), chan(MLA_HEADS * MLA_PAD), tok(MLA_HEADS * MLA_PAD),
                 chan(MLA_HEADS * MLA_V),
                 pl.BlockSpec((cpt, SSM_WIDTH, LANES), lambda b, i: (b * nt + i, 0, 0)))
    return pl.pallas_call(
        _inproj_body,
        grid=(B, nt),
        in_specs=in_specs,
        out_specs=out_specs,
        out_shape=out_shapes,
        compiler_params=_cparams(("parallel", "parallel")),
        name="in_proj",
    )(x, mod, *[P[n] for n in names], *[P[n] for n in consts], *rope_tabs)


def _na_body(q_ref, k_ref, v_ref, b_ref, o_ref, *, rows):
    i = pl.program_id(1)
    ws = jnp.clip(NA_QROWS * i - NA_ROWS // 2, 0, rows - NA_WROWS)
    start = pl.multiple_of(ws * GRID_W, GRID_W)
    kw = k_ref[pl.ds(start, NA_WROWS * GRID_W), :]
    vw = v_ref[pl.ds(start, NA_WROWS * GRID_W), :]
    q = q_ref[...]
    lane = lax.broadcasted_iota(jnp.int32, (1, NA_WIDTH), 1)
    out = jnp.zeros((NA_QROWS * GRID_W, NA_WIDTH), f32)
    for h in range(NA_HEADS):
        hm = (lane >= h * NA_HEAD_DIM) & (lane < (h + 1) * NA_HEAD_DIM)
        qh = jnp.where(hm, q, jnp.zeros_like(q))
        s = lax.dot_general(qh, kw, _NT, preferred_element_type=f32) + b_ref[h]
        m = jnp.max(s, axis=-1, keepdims=True)
        p = jnp.exp(s - m)
        l = jnp.sum(p, axis=-1, keepdims=True)
        o = jnp.dot(p.astype(bf16), vw, preferred_element_type=f32)
        out = jnp.where(hm, o / l, out)
    o_ref[...] = out.astype(bf16)


def _na_attention(q, k, v, bias, layer):
    B, L, W = q.shape
    rows = L // GRID_W
    nblk = rows // NA_QROWS
    tq = NA_QROWS * GRID_W

    def bias_map(b, i):
        variant = jnp.where(i == 0, 0, jnp.where(i == nblk - 1, 2, 1))
        return (layer, variant, 0, 0, 0)

    full = pl.BlockSpec((None, L, W), lambda b, i: (b, 0, 0))
    return pl.pallas_call(
        functools.partial(_na_body, rows=rows),
        grid=(B, nblk),
        in_specs=[pl.BlockSpec((None, tq, W), lambda b, i: (b, i, 0)), full, full,
                  pl.BlockSpec((None, None, NA_HEADS, tq, NA_WROWS * GRID_W), bias_map)],
        out_specs=pl.BlockSpec((None, tq, W), lambda b, i: (b, i, 0)),
        out_shape=jax.ShapeDtypeStruct((B, L, W), bf16),
        compiler_params=_cparams(("parallel", "arbitrary")),
        name="na_attn",
    )(q, k, v, bias)


def _na_bias_body(r_ref, o_ref):
    r = r_ref[...]
    qc = lax.broadcasted_iota(jnp.int32, (GRID_W, LANES), 0)
    lane = lax.broadcasted_iota(jnp.int32, (GRID_W, LANES), 1)
    lo_half = lane < GRID_W
    kc = jnp.where(lo_half, lane, lane - GRID_W)
    cs = jnp.clip(qc - NA_COLS // 2, 0, GRID_W - NA_COLS)
    cvalid = (kc >= cs) & (kc < cs + NA_COLS)
    neg = jnp.full((GRID_W, LANES), NEG_INF, f32)
    tiles = []
    for dr in range(2 * NA_ROWS - 1):
        rowb = jnp.broadcast_to(r[dr:dr + 1, :], (GRID_W, 2 * LANES))
        base = 2 * LANES - (NA_COLS - 1) - NA_BIAS_LANE0
        t_lo = pltpu.roll(rowb, base, 1, stride=1, stride_axis=0)[:, :LANES]
        t_hi = pltpu.roll(rowb, base + GRID_W, 1, stride=1, stride_axis=0)[:, :LANES]
        tiles.append(jnp.where(cvalid, jnp.where(lo_half, t_lo, t_hi), neg))
    for v, (off, lo) in enumerate(((0, 0), (-(NA_ROWS // 2), None), (-NA_ROWS, NA_ROWS // 2))):
        for rl in range(NA_QROWS):
            cols = []
            for kl in range(NA_WROWS):
                d = off + kl - rl
                ok = (-(NA_ROWS // 2) <= d < NA_ROWS // 2) if lo is None else (lo <= kl < lo + NA_ROWS)
                cols.append(tiles[d + NA_ROWS - 1] if ok else neg)
            pairs = [jnp.where(lo_half, cols[2 * j], cols[2 * j + 1]) for j in range(NA_WROWS // 2)]
            o_ref[v, rl * GRID_W:(rl + 1) * GRID_W, :] = jnp.concatenate(pairs, axis=1)


def _na_bias_tables(rpb):
    depth, nh, nr, nc = rpb.shape
    rp = jnp.pad(rpb.astype(f32),
                 ((0, 0), (0, 0), (0, 16 - nr), (NA_BIAS_LANE0, 2 * LANES - NA_BIAS_LANE0 - nc)))
    tq, tkw = NA_QROWS * GRID_W, NA_WROWS * GRID_W
    return pl.pallas_call(
        _na_bias_body,
        grid=(depth, nh),
        in_specs=[pl.BlockSpec((None, None, 16, 2 * LANES), lambda l, h: (l, h, 0, 0))],
        out_specs=pl.BlockSpec((None, 3, None, tq, tkw), lambda l, h: (l, 0, h, 0, 0)),
        out_shape=jax.ShapeDtypeStruct((depth, 3, nh, tq, tkw), f32),
        compiler_params=_cparams(("parallel", "parallel")),
        name="na_bias",
    )(rp)


def _mla_body(qt_ref, k_ref, vt_ref, o_ref, s_scr, *, nq, nk):
    tq, tk = MLA_TQ, MLA_TK
    ones = jnp.ones((MLA_VROWS - MLA_V, tk), bf16)

    def scores(t, slot):
        qoff = pl.multiple_of((t // nk) * tq, tq)
        koff = pl.multiple_of((t % nk) * tk, tk)
        qt = qt_ref[:, pl.ds(qoff, tq)]
        kb = k_ref[pl.ds(koff, tk), :]
        cmax = []
        for h in range(2):
            s = jnp.dot(kb[:, h * MLA_PAD:(h + 1) * MLA_PAD], qt[h * MLA_PAD:(h + 1) * MLA_PAD],
                        preferred_element_type=f32)
            s_scr[slot, h] = s
            cmax.append(jnp.max(s, axis=0, keepdims=True))
        return tuple(cmax)

    def accumulate(t, slot, cmax, state):
        j = t % nk
        qoff = pl.multiple_of((t // nk) * tq, tq)
        koff = pl.multiple_of(j * tk, tk)
        vt = vt_ref[:, pl.ds(koff, tk)]
        tile_start = j == 0
        new = []
        for h in range(2):
            m, acc = state[2 * h], state[2 * h + 1]
            m = jnp.where(tile_start, -jnp.inf, m)
            mn = jnp.maximum(m, cmax[h])
            al = jnp.exp2(m - mn)
            p = jnp.exp2(s_scr[slot, h] - mn).astype(bf16)
            vte = jnp.concatenate([vt[h * MLA_V:(h + 1) * MLA_V], ones], axis=0)
            acc = al * acc + jnp.dot(vte, p, preferred_element_type=f32)
            o_ref[h * MLA_V:(h + 1) * MLA_V, pl.ds(qoff, tq)] = (acc[:MLA_V] / acc[MLA_V:MLA_V + 1]).astype(bf16)
            new += [mn, acc]
        return tuple(new)

    nt = nq * nk
    ahead, ns = MLA_AHEAD, MLA_SLOTS

    def trip(t0, carry, last):
        pend, state = list(carry[:2 * ahead]), carry[2 * ahead:]
        for u in range(ns):
            if not (last and u + ahead >= ns):
                pend += scores(t0 + u + ahead, (u + ahead) % ns)
            state = accumulate(t0 + u, u, tuple(pend[:2]), state)
            pend = pend[2:]
        return tuple(pend) + state

    m0 = jnp.full((1, tq), -jnp.inf, f32)
    a0 = jnp.zeros((MLA_VROWS, tq), f32)
    carry = ()
    for t in range(ahead):
        carry += scores(t, t)
    carry = lax.fori_loop(0, nt // ns - 1, lambda i, c: trip(i * ns, c, False), carry + (m0, a0, m0, a0))
    trip(nt - ns, carry, True)


def _mla_attention(mqt, mk, mvt):
    B, L, _ = mk.shape
    npair = MLA_HEADS // 2
    nq, nk = L // MLA_TQ, L // MLA_TK
    assert (nq * nk) % MLA_SLOTS == 0
    mode = dict(pipeline_mode=pl.Buffered(1)) if L * 2 * MLA_PAD * 2 > 4 * 1024 * 1024 else {}
    return pl.pallas_call(
        functools.partial(_mla_body, nq=nq, nk=nk),
        grid=(B, npair),
        in_specs=[pl.BlockSpec((None, 2 * MLA_PAD, L), lambda b, p: (b, p, 0), **mode),
                  pl.BlockSpec((None, L, 2 * MLA_PAD), lambda b, p: (b, 0, p), **mode),
                  pl.BlockSpec((None, 2 * MLA_V, L), lambda b, p: (b, p, 0), **mode)],
        out_specs=pl.BlockSpec((None, 2 * MLA_V, L), lambda b, p: (b, p, 0)),
        out_shape=jax.ShapeDtypeStruct((B, MLA_HEADS * MLA_V, L), bf16),
        scratch_shapes=[pltpu.VMEM((MLA_SLOTS, 2, MLA_TK, MLA_TQ), f32)],
        compiler_params=_cparams(("parallel", "parallel")),
        name="mla_attn",
    )(mqt, mk, mvt)


def _toeplitz_body(pf_ref, bb_ref, c_ref, o_ref):
    b = bb_ref[...]
    pfr, pfi, pbr, pbi = pf_ref[0], pf_ref[1], pf_ref[2], pf_ref[3]
    bfr, bfi, bbr, bbi = b[:, 0:1], b[:, 1:2], b[:, 2:3], b[:, 3:4]
    rhs = jnp.concatenate([pfr * bfr - pfi * bfi, pfr * bfi + pfi * bfr,
                           pbr * bbr - pbi * bbi, pbr * bbi + pbi * bbr], axis=0)
    kern = jnp.dot(c_ref[...], rhs, precision=lax.Precision.HIGHEST, preferred_element_type=f32)
    for c in range(SSM_GC):
        rowb = jnp.broadcast_to(kern[c:c + 1, :], (LANES, 2 * LANES))
        toe = pltpu.roll(rowb, LANES + 1, 1, stride=1, stride_axis=0)
        o_ref[:, c * LANES:(c + 1) * LANES] = toe[:, :LANES].astype(bf16)


def _toeplitz_gen(pf, bb, cm):
    dg = pf.shape[0]
    n = SSM_GC * LANES
    return pl.pallas_call(
        _toeplitz_body,
        grid=(dg, SSM_GC),
        in_specs=[pl.BlockSpec((None, 4, SSM_STATE, 2 * LANES), lambda g, c: (g, 0, 0, 0)),
                  pl.BlockSpec((None, None, SSM_STATE, 4), lambda g, c: (g, c, 0, 0)),
                  pl.BlockSpec((None, SSM_GC, 4 * SSM_STATE), lambda g, c: (g, 0, 0))],
        out_specs=pl.BlockSpec((None, LANES, n), lambda g, c: (g, c, 0)),
        out_shape=jax.ShapeDtypeStruct((dg, n, n), bf16),
        compiler_params=_cparams(("parallel", "arbitrary")),
        name="s5_toeplitz",
    )(pf, bb, cm)


def _ssm_body(u_ref, t_ref, wv_ref, wc_ref, a_ref, d_ref, o_ref,
              vre, vim, xfre, xfim, xbre, xbim, *, nb, nj):
    m = nb * nj
    u32 = jnp.concatenate([u_ref[:, c, :] for c in range(SSM_GC)], axis=1)
    ub = u32.astype(bf16)
    y = jnp.dot(ub, t_ref[...], preferred_element_type=f32)
    v = jnp.dot(ub, wv_ref[...], preferred_element_type=f32)
    vre[...] = v[:, :LANES]
    vim[...] = v[:, LANES:]
    are = a_ref[0:1, :]
    aim = a_ref[1:2, :]
    is_fwd = lax.broadcasted_iota(jnp.int32, (nb, LANES), 1) < SSM_STATE

    def rows(j):
        return pl.ds(j, nb, stride=nj) if nb > 1 else pl.ds(j, 1)

    def step(k, carry):
        xr, xi = carry
        rf, rb = rows(k), rows(nj - 1 - k)
        xfre[rf, :] = xr
        xfim[rf, :] = xi
        xbre[rb, :] = xr
        xbim[rb, :] = xi
        vr = jnp.where(is_fwd, vre[rf, :], vre[rb, :])
        vi = jnp.where(is_fwd, vim[rf, :], vim[rb, :])
        return are * xr - aim * xi + vr, are * xi + aim * xr + vi

    z = jnp.zeros((nb, LANES), f32)
    lax.fori_loop(0, nj, step, (z, z))
    fwd_m = lax.broadcasted_iota(jnp.int32, (m, LANES), 1) < SSM_STATE
    xin = jnp.concatenate([jnp.where(fwd_m, xfre[...], xbre[...]),
                           jnp.where(fwd_m, xfim[...], xbim[...])], axis=1).astype(bf16)
    y = y + jnp.dot(xin, wc_ref[...], preferred_element_type=f32) + u32 * d_ref[...]
    for c in range(SSM_GC):
        o_ref[:, c, :] = y[:, c * LANES:(c + 1) * LANES]


def _ssm(ut, layer, S, nb, nj):
    m = nb * nj
    n = SSM_GC * LANES
    u4 = ut.reshape(m, SSM_GROUPS, SSM_GC, LANES)
    gspec = lambda tail: pl.BlockSpec((None,) + tail, lambda g: (layer * SSM_GROUPS + g,) + (0,) * len(tail))
    io = pl.BlockSpec((m, None, SSM_GC, LANES), lambda g: (0, g, 0, 0))
    out = pl.pallas_call(
        functools.partial(_ssm_body, nb=nb, nj=nj),
        grid=(SSM_GROUPS,),
        in_specs=[io, gspec((n, n)), gspec((n, 4 * SSM_STATE)), gspec((4 * SSM_STATE, n)),
                  gspec((2, LANES)), gspec((1, n))],
        out_specs=io,
        out_shape=jax.ShapeDtypeStruct((m, SSM_GROUPS, SSM_GC, LANES), f32),
        scratch_shapes=[pltpu.VMEM((m, LANES), f32)] * 6,
        compiler_params=_cparams(("parallel",)),
        name="s5_conv",
    )(u4, S["toe"], S["wv"], S["wc"], S["a"], S["d"])
    return out.reshape(m, SSM_WIDTH, LANES)


def _ssm_tables(a_re, a_im, b_re, b_im, c_re, c_im, log_dt, ssm_d):
    depth = a_re.shape[0]
    dg = depth * SSM_GROUPS
    lam = lax.complex(a_re.astype(f32), a_im.astype(f32))
    dt = jnp.exp(log_dt.astype(f32))[..., None]
    lam_dt = lam * dt
    lam_bar = jnp.exp(lam_dt)
    b_bar = ((lam_bar - 1.0) / lam)[..., None] * lax.complex(b_re.astype(f32), b_im.astype(f32))
    cmat = lax.complex(c_re.astype(f32), c_im.astype(f32))
    steps = jnp.arange(LANES + 1, dtype=f32)
    pw = jnp.exp(lam_dt[..., None] * steps)
    pwf, pwb = pw[:, 0], pw[:, 1]

    e = np.arange(2 * LANES)
    f_ok = (e >= LANES - 1) & (e <= 2 * LANES - 2)
    b_ok = e <= LANES - 1
    pf_f = jnp.where(f_ok, pwf[..., np.clip(e - (LANES - 1), 0, LANES - 1)], 0.0)
    pf_b = jnp.where(b_ok, pwb[..., np.clip(LANES - 1 - e, 0, LANES - 1)], 0.0)
    pf = jnp.stack([pf_f.real, pf_f.imag, pf_b.real, pf_b.imag], axis=2)
    pf = pf.reshape(dg, 4, SSM_STATE, 2 * LANES)

    bbf, bbb = b_bar[:, 0], b_bar[:, 1]
    bb = jnp.stack([bbf.real, bbf.imag, bbb.real, bbb.imag], axis=-1)
    bb = jnp.transpose(bb, (0, 1, 3, 2, 4)).reshape(dg, SSM_GC, SSM_STATE, 4)

    cf, cb = cmat[:, 0], cmat[:, 1]
    cm = jnp.concatenate([cf.real, -cf.imag, cb.real, -cb.imag], axis=-1).reshape(dg, SSM_GC, 4 * SSM_STATE)

    s_idx = np.arange(LANES)
    cof = pwf[..., LANES - 1 - s_idx][..., None, :, :] * jnp.swapaxes(bbf, -1, -2)[..., None]
    cob = pwb[..., s_idx][..., None, :, :] * jnp.swapaxes(bbb, -1, -2)[..., None]
    cof = jnp.swapaxes(cof, -1, -2)
    cob = jnp.swapaxes(cob, -1, -2)
    wv = jnp.concatenate([cof.real, cob.real, cof.imag, cob.imag], axis=-1)
    wv = wv.reshape(dg, SSM_GC * LANES, 4 * SSM_STATE).astype(bf16)

    t_idx = np.arange(LANES)
    clf = cf[..., None] * pwf[..., None, :, :][..., t_idx + 1]
    clb = cb[..., None] * pwb[..., None, :, :][..., LANES - t_idx]
    to_rows = lambda a: jnp.transpose(a, (0, 1, 3, 2, 4)).reshape(dg, SSM_STATE, SSM_GC * LANES)
    wc = jnp.concatenate([to_rows(clf.real), to_rows(clb.real), to_rows(-clf.imag), to_rows(-clb.imag)], axis=1)
    wc = wc.astype(bf16)

    a128 = jnp.concatenate([pwf[..., LANES], pwb[..., LANES]], axis=-1)
    a = jnp.stack([a128.real, a128.imag], axis=2).reshape(dg, 2, LANES)
    d = jnp.repeat(ssm_d.astype(f32).reshape(dg, SSM_GC), LANES, axis=-1).reshape(dg, 1, SSM_GC * LANES)
    return dict(pf=pf, bb=bb, cm=cm, wv=wv, wc=wc, a=a, d=d)


def _outproj_body(x_ref, mod_ref, yna_ref, ymlat_ref, yst_ref, gluw_ref, glub_ref, wo_ref, g2_ref,
                  x1_ref, h2_ref):
    ys = jnp.concatenate([yst_ref[c].T for c in range(yst_ref.shape[0])], axis=0)
    ys = 0.5 * ys * (1.0 + jnp.tanh(math.sqrt(2.0 / math.pi) * (ys + 0.044715 * (ys * ys * ys))))
    gate = jax.nn.sigmoid(jnp.dot(ys.astype(bf16), gluw_ref[...], preferred_element_type=f32) + glub_ref[...])
    yssm = (ys * gate).astype(bf16)
    o1 = NA_WIDTH
    o2 = NA_WIDTH + MLA_HEADS * MLA_V
    mix = (jnp.dot(yna_ref[...], wo_ref[0:o1, :], preferred_element_type=f32)
           + lax.dot_general(ymlat_ref[...], wo_ref[o1:o2, :], _TN, preferred_element_type=f32)
           + jnp.dot(yssm, wo_ref[o2:, :], preferred_element_type=f32))
    x1 = x_ref[...] + mod_ref[2:3, :] * mix
    x1_ref[...] = x1
    ms = jnp.mean(x1 * x1, axis=-1, keepdims=True)
    h2 = (x1 * lax.rsqrt(ms + RMS_EPS) * g2_ref[...]) * (1.0 + mod_ref[4:5, :]) + mod_ref[3:4, :]
    h2_ref[...] = h2.astype(bf16)


def _outproj(x, mod, yna, ymla, yst, layer, P):
    B, L, D = x.shape
    tm = TOKEN_TILE
    nt = L // tm
    cpt = tm // LANES
    tok = lambda w: pl.BlockSpec((None, tm, w), lambda b, i: (b, i, 0))
    lw = lambda name: _layer_spec(P[name].shape[1:], layer)
    return pl.pallas_call(
        _outproj_body,
        grid=(B, nt),
        in_specs=[tok(D), pl.BlockSpec((None, 6, D), lambda b, i: (b, 0, 0)), tok(NA_WIDTH),
                  pl.BlockSpec((None, MLA_HEADS * MLA_V, tm), lambda b, i: (b, 0, i)),
                  pl.BlockSpec((cpt, SSM_WIDTH, LANES), lambda b, i: (b * nt + i, 0, 0)),
                  lw("glu_w"), lw("glu_b"), lw("w_out"), lw("g2")],
        out_specs=(tok(D), tok(D)),
        out_shape=(jax.ShapeDtypeStruct((B, L, D), f32), jax.ShapeDtypeStruct((B, L, D), bf16)),
        compiler_params=_cparams(("parallel", "parallel")),
        name="out_proj",
    )(x, mod, yna, ymla, yst, P["glu_w"], P["glu_b"], P["w_out"], P["g2"])


def _ffn_body(x1_ref, h2_ref, mod_ref, wg_ref, wu_ref, wd_ref, o_ref, act_ref):
    h2 = h2_ref[...]
    th = 256
    for c in range(FFN_HIDDEN // th):
        cs = slice(c * th, (c + 1) * th)
        g = jnp.dot(h2, wg_ref[:, cs], preferred_element_type=f32)
        u = jnp.dot(h2, wu_ref[:, cs], preferred_element_type=f32)
        act_ref[:, cs] = (g * jax.nn.sigmoid(g) * u).astype(bf16)
    ffn = jnp.dot(act_ref[...], wd_ref[...], preferred_element_type=f32)
    o_ref[...] = x1_ref[...] + mod_ref[5:6, :] * ffn


def _ffn(x1, h2, mod, layer, P):
    B, L, D = x1.shape
    tm = TOKEN_TILE
    tok = pl.BlockSpec((None, tm, D), lambda b, i: (b, i, 0))

    def wspec(name):
        tail = P[name].shape[1:]
        return pl.BlockSpec((None,) + tail, lambda b, i: (layer, 0, 0), pipeline_mode=pl.Buffered(1))

    return pl.pallas_call(
        _ffn_body,
        grid=(B, L // tm),
        in_specs=[tok, tok, pl.BlockSpec((None, 6, D), lambda b, i: (b, 0, 0)),
                  wspec("w_gate"), wspec("w_up"), wspec("w_down")],
        out_specs=tok,
        out_shape=jax.ShapeDtypeStruct((B, L, D), f32),
        scratch_shapes=[pltpu.VMEM((tm, FFN_HIDDEN), bf16)],
        compiler_params=_cparams(("parallel", "parallel")),
        name="ffn",
    )(x1, h2, mod, P["w_gate"], P["w_up"], P["w_down"])


def _prep_params(norm1_g, w_in, na_q_g, na_k_g, mla_cq_g, mla_ckv_g, mla_w_uq, mla_w_ukv, mla_qn_g, mla_kn_g,
                 mla_qr_g, mla_kr_g, glu_w, glu_b, w_out, norm2_g, ffn_w_gate, ffn_w_up, ffn_w_down):
    depth = w_in.shape[0]
    o1 = 3 * NA_WIDTH
    o2 = o1 + MLA_Q_LORA
    o3 = o2 + MLA_KV_LORA
    o4 = o3 + MLA_ROPE
    row = lambda a: a.astype(f32)[:, None, :]
    zeros = lambda *s: jnp.zeros(s, f32)
    P = {}
    P["g1"] = row(norm1_g)
    P["g2"] = row(norm2_g)
    P["w_na"] = w_in[:, :, :o1].astype(bf16)
    P["w_c"] = w_in[:, :, o1:o3].astype(bf16)
    P["w_kr"] = jnp.concatenate([zeros(depth, D_MODEL, MLA_NOPE), w_in[:, :, o3:o4],
                                 zeros(depth, D_MODEL, MLA_PAD - MLA_NOPE - MLA_ROPE)], axis=-1).astype(bf16)
    P["w_ut"] = jnp.swapaxes(w_in[:, :, o4:], 1, 2).astype(bf16)
    wq = mla_w_uq.reshape(depth, MLA_Q_LORA, MLA_HEADS, MLA_NOPE + MLA_ROPE)
    wq = jnp.concatenate([wq, zeros(depth, MLA_Q_LORA, MLA_HEADS, MLA_PAD - MLA_NOPE - MLA_ROPE)], axis=-1)
    P["w_uq"] = wq.reshape(depth, MLA_Q_LORA, MLA_HEADS * MLA_PAD).astype(bf16)
    wkv = mla_w_ukv.reshape(depth, MLA_KV_LORA, MLA_HEADS, MLA_NOPE + MLA_V)
    wk = jnp.concatenate([wkv[..., :MLA_NOPE], zeros(depth, MLA_KV_LORA, MLA_HEADS, MLA_PAD - MLA_NOPE)], axis=-1)
    P["w_ukv_k"] = wk.reshape(depth, MLA_KV_LORA, MLA_HEADS * MLA_PAD).astype(bf16)
    P["w_ukv_vt"] = jnp.swapaxes(wkv[..., MLA_NOPE:].reshape(depth, MLA_KV_LORA, MLA_HEADS * MLA_V), 1, 2).astype(bf16)
    P["gq_na"] = row(jnp.tile(na_q_g, (1, NA_HEADS))) * (NA_HEAD_DIM ** -0.5)
    P["gk_na"] = row(jnp.tile(na_k_g, (1, NA_HEADS)))
    P["g_cq"] = row(mla_cq_g)
    P["g_ckv"] = row(mla_ckv_g)
    scale = (MLA_NOPE + MLA_ROPE) ** -0.5 * math.log2(math.e)
    pad = MLA_PAD - MLA_NOPE - MLA_ROPE
    gq = jnp.concatenate([mla_qn_g, mla_qr_g, zeros(depth, pad)], axis=-1) * scale
    P["gq"] = row(jnp.tile(gq, (1, MLA_HEADS)))
    gk = jnp.concatenate([mla_kn_g, zeros(depth, MLA_PAD - MLA_NOPE)], axis=-1)
    P["gk"] = row(jnp.tile(gk, (1, MLA_HEADS)))
    P["gkr"] = row(jnp.concatenate([zeros(depth, MLA_NOPE), mla_kr_g, zeros(depth, pad)], axis=-1))
    invq = np.concatenate([np.full(MLA_NOPE, 1.0 / MLA_NOPE), np.full(MLA_ROPE, 1.0 / MLA_ROPE), np.zeros(pad)])
    invk = np.concatenate([np.full(MLA_NOPE, 1.0 / MLA_NOPE), np.zeros(MLA_PAD - MLA_NOPE)])
    P["invq"] = jnp.asarray(np.tile(invq, MLA_HEADS)[None, :], f32)
    P["invk"] = jnp.asarray(np.tile(invk, MLA_HEADS)[None, :], f32)
    lane = np.arange(NA_WIDTH)
    P["G_na"] = jnp.asarray((lane[:, None] // NA_HEAD_DIM) == (lane[None, :] // NA_HEAD_DIM), bf16)
    lane = np.arange(2 * MLA_PAD)
    grp = np.where(lane % MLA_PAD < MLA_NOPE, 0, np.where(lane % MLA_PAD < MLA_NOPE + MLA_ROPE, 1, 2))
    same = (lane[:, None] // MLA_PAD == lane[None, :] // MLA_PAD) & (grp[:, None] == grp[None, :]) & (grp[:, None] < 2)
    P["G_qm"] = jnp.asarray(same, bf16)
    P["glu_w"] = glu_w.astype(bf16)
    P["glu_b"] = row(glu_b)
    P["w_out"] = w_out.astype(bf16)
    P["w_gate"] = ffn_w_gate.astype(bf16)
    P["w_up"] = ffn_w_up.astype(bf16)
    P["w_down"] = ffn_w_down.astype(bf16)
    return P


def _rope_tables(length):
    inv = 1.0 / (ROPE_BASE ** (jnp.arange(0, MLA_ROPE, 2, dtype=f32) / MLA_ROPE))
    ang = jnp.arange(length, dtype=f32)[:, None] * inv[None, :]
    cos, sin = jnp.cos(ang), jnp.sin(ang)
    half = MLA_ROPE // 2
    z = lambda w: jnp.zeros((length, w), f32)
    pad = MLA_PAD - MLA_NOPE - MLA_ROPE
    cos_t = jnp.concatenate([jnp.ones((length, MLA_NOPE), f32), cos, cos, z(pad)], axis=-1)
    sin_lo = jnp.concatenate([z(MLA_NOPE), -sin, z(half), z(pad)], axis=-1)
    sin_hi = jnp.concatenate([z(MLA_NOPE), z(half), sin, z(pad)], axis=-1)
    return cos_t, sin_lo, sin_hi


def _trunk(x, mods, P, S, bias):
    B, L, _ = x.shape
    rope_tabs = _rope_tables(L)
    nj = L // LANES
    for layer in range(len(mods)):
        mod = mods[layer]
        naq, nak, nav, mq, mk, mvt, ut = _inproj(x, mod, layer, P, rope_tabs)
        yna = _na_attention(naq, nak, nav, bias, layer)
        ymla = _mla_attention(mq, mk, mvt)
        yst = _ssm(ut, layer, S, B, nj)
        x1, h2 = _outproj(x, mod, yna, ymla, yst, layer, P)
        x = _ffn(x1, h2, mod, layer, P)
    return x


def kernel(x_prompt, x_sample, c_prompt, c_sample, ada_w, ada_b, norm1_g, w_in, na_q_g, na_k_g, na_rpb, mla_cq_g,
           mla_ckv_g, mla_w_uq, mla_w_ukv, mla_qn_g, mla_kn_g, mla_qr_g, mla_kr_g, ssm_a_re, ssm_a_im, ssm_b_re,
           ssm_b_im, ssm_c_re, ssm_c_im, ssm_log_dt, ssm_d, glu_w, glu_b, w_out, norm2_g, ffn_w_gate, ffn_w_up,
           ffn_w_down):
    depth = w_in.shape[0]
    nbp, nbs = c_prompt.shape[0], c_sample.shape[0]
    rows = -(-(nbp + nbs) // 8) * 8
    c_all = jnp.concatenate([c_prompt, c_sample, jnp.zeros((rows - nbp - nbs, D_MODEL), f32)], axis=0)
    mod = _modulation(c_all, ada_w, ada_b).reshape(depth, rows, 6, D_MODEL)
    mods_p = [mod[l, :nbp] for l in range(depth)]
    mods_s = [mod[l, nbp:nbp + nbs] for l in range(depth)]

    P = _prep_params(norm1_g, w_in, na_q_g, na_k_g, mla_cq_g, mla_ckv_g, mla_w_uq, mla_w_ukv, mla_qn_g, mla_kn_g,
                     mla_qr_g, mla_kr_g, glu_w, glu_b, w_out, norm2_g, ffn_w_gate, ffn_w_up, ffn_w_down)
    S = _ssm_tables(ssm_a_re, ssm_a_im, ssm_b_re, ssm_b_im, ssm_c_re, ssm_c_im, ssm_log_dt, ssm_d)
    S["toe"] = _toeplitz_gen(S["pf"], S["bb"], S["cm"])
    bias = _na_bias_tables(na_rpb)

    y_prompt = _trunk(x_prompt, mods_p, P, S, bias)
    y_sample = _trunk(x_sample, mods_s, P, S, bias)
    return (y_prompt, y_sample)
```

```python
import functools
import math

import numpy as np
import jax
import jax.numpy as jnp
from jax import lax
from jax.experimental import pallas as pl
from jax.experimental.pallas import tpu as pltpu

f32 = jnp.float32
bf16 = jnp.bfloat16

D_MODEL = 1024
GRID_W = 64
NA_HEADS = 4
NA_HEAD_DIM = 64
NA_WIDTH = NA_HEADS * NA_HEAD_DIM
NA_ROWS = 8
NA_COLS = 16
MLA_HEADS = 8
MLA_NOPE = 64
MLA_ROPE = 32
MLA_V = 64
MLA_Q_LORA = 384
MLA_KV_LORA = 256
ROPE_BASE = 10000.0
SSM_GROUPS = 16
SSM_GC = 16
SSM_STATE = 64
SSM_WIDTH = SSM_GROUPS * SSM_GC
FFN_HIDDEN = 2816
RMS_EPS = 1e-6
NEG_INF = -1e30

LANES = 128
MLA_PAD = 128
NA_QROWS = 4
NA_WROWS = NA_QROWS + NA_ROWS
NA_BIAS_LANE0 = 200
TOKEN_TILE = 512
MLA_CW = 256
MLA_TQ = MLA_CW
MLA_TK = 1024
MLA_AHEAD = 2
MLA_SLOTS = 2 * MLA_AHEAD
MLA_VROWS = MLA_V + 16
VMEM_LIMIT = 48 * 1024 * 1024

_NT = (((1,), (1,)), ((), ()))
_TN = (((0,), (0,)), ((), ()))


def _cparams(sem):
    return pltpu.CompilerParams(dimension_semantics=sem, vmem_limit_bytes=VMEM_LIMIT)


def _layer_spec(tail, layer):
    n = len(tail)
    return pl.BlockSpec((None,) + tuple(tail), lambda *_: (layer,) + (0,) * n)


def _mod_body(c_ref, w_ref, b_ref, o_ref):
    c = c_ref[...]
    s = c * jax.nn.sigmoid(c)
    o_ref[...] = jnp.dot(s, w_ref[...], precision=lax.Precision.HIGHEST, preferred_element_type=f32) + b_ref[...]


def _modulation(c_all, ada_w, ada_b):
    depth, d, n = ada_w.shape
    tn = 1536
    rows = c_all.shape[0]
    return pl.pallas_call(
        _mod_body,
        grid=(depth, n // tn),
        in_specs=[
            pl.BlockSpec((rows, d), lambda l, j: (0, 0)),
            pl.BlockSpec((None, d, tn), lambda l, j: (l, 0, j)),
            pl.BlockSpec((None, 1, tn), lambda l, j: (l, 0, j)),
        ],
        out_specs=pl.BlockSpec((None, rows, tn), lambda l, j: (l, 0, j)),
        out_shape=jax.ShapeDtypeStruct((depth, rows, n), f32),
        compiler_params=_cparams(("arbitrary", "arbitrary")),
        name="adaln_mod",
    )(c_all, ada_w, ada_b.reshape(depth, 1, n))


def _rope(x, cos, sin_lo, sin_hi):
    return x * cos + pltpu.roll(x, LANES - 16, 1) * sin_lo + pltpu.roll(x, 16, 1) * sin_hi


def _inproj_body(x_ref, mod_ref, g1_ref, wna_ref, wc_ref, wkr_ref, wut_ref, wuq_ref, wukvk_ref, wukvvt_ref,
                 gqna_ref, gkna_ref, gcq_ref, gckv_ref, gq_ref, gk_ref, gkr_ref, invq_ref, invk_ref,
                 gna_ref, gqm_ref, cos_ref, sinlo_ref, sinhi_ref,
                 naq_ref, nak_ref, nav_ref, mqt_ref, mk_ref, mvt_ref, ut_ref):
    x = x_ref[...]
    shift1 = mod_ref[0:1, :]
    scale1 = mod_ref[1:2, :]
    ms = jnp.mean(x * x, axis=-1, keepdims=True)
    h = (x * lax.rsqrt(ms + RMS_EPS) * g1_ref[...]) * (1.0 + scale1) + shift1
    hb = h.astype(bf16)

    z = jnp.dot(hb, wna_ref[...], preferred_element_type=f32)
    q = z[:, :NA_WIDTH]
    k = z[:, NA_WIDTH:2 * NA_WIDTH]
    gna = gna_ref[...]
    ssq = jnp.dot((q * q).astype(bf16), gna, preferred_element_type=f32) * (1.0 / NA_HEAD_DIM)
    ssk = jnp.dot((k * k).astype(bf16), gna, preferred_element_type=f32) * (1.0 / NA_HEAD_DIM)
    naq_ref[...] = (q * lax.rsqrt(ssq + RMS_EPS) * gqna_ref[...]).astype(bf16)
    nak_ref[...] = (k * lax.rsqrt(ssk + RMS_EPS) * gkna_ref[...]).astype(bf16)
    nav_ref[...] = z[:, 2 * NA_WIDTH:].astype(bf16)

    zc = jnp.dot(hb, wc_ref[...], preferred_element_type=f32)
    cq = zc[:, :MLA_Q_LORA]
    ckv = zc[:, MLA_Q_LORA:]
    cqn = (cq * lax.rsqrt(jnp.mean(cq * cq, axis=-1, keepdims=True) + RMS_EPS) * gcq_ref[...]).astype(bf16)
    ckvn = (ckv * lax.rsqrt(jnp.mean(ckv * ckv, axis=-1, keepdims=True) + RMS_EPS) * gckv_ref[...]).astype(bf16)

    cos = cos_ref[...]
    sin_lo = sinlo_ref[...]
    sin_hi = sinhi_ref[...]
    gqm = gqm_ref[...]

    kr = jnp.dot(hb, wkr_ref[...], preferred_element_type=f32)
    kr_ms = jnp.sum(kr * kr, axis=-1, keepdims=True) * (1.0 / MLA_ROPE)
    kr = _rope(kr * lax.rsqrt(kr_ms + RMS_EPS) * gkr_ref[...], cos, sin_lo, sin_hi)

    qraw = jnp.dot(cqn, wuq_ref[...], preferred_element_type=f32)
    kraw = jnp.dot(ckvn, wukvk_ref[...], preferred_element_type=f32)
    for p in range(MLA_HEADS // 2):
        sl = slice(2 * p * MLA_PAD, 2 * (p + 1) * MLA_PAD)
        qs = qraw[:, sl]
        ss = jnp.dot((qs * qs).astype(bf16), gqm, preferred_element_type=f32) * invq_ref[:, sl]
        qn = qs * lax.rsqrt(ss + RMS_EPS) * gq_ref[:, sl]
        ks = kraw[:, sl]
        ss = jnp.dot((ks * ks).astype(bf16), gqm, preferred_element_type=f32) * invk_ref[:, sl]
        kn = ks * lax.rsqrt(ss + RMS_EPS) * gk_ref[:, sl]
        for hh in range(2):
            lo = (2 * p + hh) * MLA_PAD
            piece = qn[:, hh * MLA_PAD:(hh + 1) * MLA_PAD]
            mqt_ref[lo:lo + MLA_PAD, :] = _rope(piece, cos, sin_lo, sin_hi).T.astype(bf16)
            mk_ref[:, lo:lo + MLA_PAD] = (kn[:, hh * MLA_PAD:(hh + 1) * MLA_PAD] + kr).astype(bf16)
    mvt_ref[...] = lax.dot_general(wukvvt_ref[...], ckvn, _NT, preferred_element_type=f32).astype(bf16)

    ut = lax.dot_general(wut_ref[...], hb, _NT, preferred_element_type=f32)
    for c in range(ut_ref.shape[0]):
        ut_ref[c] = ut[:, c * LANES:(c + 1) * LANES].reshape(SSM_GROUPS, SSM_GC, LANES)


def _inproj(x, mod, layer, P, rope_tabs):
    B, L, D = x.shape
    tm = TOKEN_TILE
    nt = L // tm
    cpt = tm // LANES
    tok = lambda w: pl.BlockSpec((None, tm, w), lambda b, i: (b, i, 0))
    pos = pl.BlockSpec((tm, LANES), lambda b, i: (i, 0))
    const2 = lambda a: pl.BlockSpec(a.shape, lambda b, i: (0, 0))
    lw = lambda name: _layer_spec(P[name].shape[1:], layer)
    names = ["g1", "w_na", "w_c", "w_kr", "w_ut", "w_uq", "w_ukv_k", "w_ukv_vt",
             "gq_na", "gk_na", "g_cq", "g_ckv", "gq", "gk", "gkr"]
    consts = ["invq", "invk", "G_na", "G_qm"]
    in_specs = ([tok(D), pl.BlockSpec((None, 6, D), lambda b, i: (b, 0, 0))]
                + [lw(n) for n in names] + [const2(P[n]) for n in consts] + [pos, pos, pos])
    out_shapes = (
        jax.ShapeDtypeStruct((B, L, NA_WIDTH), bf16),
        jax.ShapeDtypeStruct((B, L, NA_WIDTH), bf16),
        jax.ShapeDtypeStruct((B, L, NA_WIDTH), bf16),
        jax.ShapeDtypeStruct((B, MLA_HEADS * MLA_PAD, L), bf16),
        jax.ShapeDtypeStruct((B, L, MLA_HEADS * MLA_PAD), bf16),
        jax.ShapeDtypeStruct((B, MLA_HEADS * MLA_V, L), bf16),
        jax.ShapeDtypeStruct((B * L // LANES, SSM_GROUPS, SSM_GC, LANES), f32),
    )
    chan = lambda w: pl.BlockSpec((None, w, tm), lambda b, i: (b, 0, i))
    out_specs = (tok(NA_WIDTH), tok(NA_WIDTH), tok(NA_WIDTH), chan(MLA_HEADS * MLA_PAD), tok(MLA_HEADS * MLA_PAD),
                 chan(MLA_HEADS * MLA_V),
                 pl.BlockSpec((cpt, SSM_GROUPS, SSM_GC, LANES), lambda b, i: (b * nt + i, 0, 0, 0)))
    return pl.pallas_call(
        _inproj_body,
        grid=(B, nt),
        in_specs=in_specs,
        out_specs=out_specs,
        out_shape=out_shapes,
        compiler_params=_cparams(("parallel", "parallel")),
        name="in_proj",
    )(x, mod, *[P[n] for n in names], *[P[n] for n in consts], *rope_tabs)


def _na_body(q_ref, k_ref, v_ref, b_ref, o_ref, *, rows):
    i = pl.program_id(1)
    ws = jnp.clip(NA_QROWS * i - NA_ROWS // 2, 0, rows - NA_WROWS)
    start = pl.multiple_of(ws * GRID_W, GRID_W)
    kw = k_ref[pl.ds(start, NA_WROWS * GRID_W), :]
    vw = v_ref[pl.ds(start, NA_WROWS * GRID_W), :]
    q = q_ref[...]
    lane = lax.broadcasted_iota(jnp.int32, (1, NA_WIDTH), 1)
    out = jnp.zeros((NA_QROWS * GRID_W, NA_WIDTH), f32)
    for h in range(NA_HEADS):
        hm = (lane >= h * NA_HEAD_DIM) & (lane < (h + 1) * NA_HEAD_DIM)
        qh = jnp.where(hm, q, jnp.zeros_like(q))
        s = lax.dot_general(qh, kw, _NT, preferred_element_type=f32) + b_ref[h]
        m = jnp.max(s, axis=-1, keepdims=True)
        p = jnp.exp(s - m)
        l = jnp.sum(p, axis=-1, keepdims=True)
        o = jnp.dot(p.astype(bf16), vw, preferred_element_type=f32)
        out = jnp.where(hm, o / l, out)
    o_ref[...] = out.astype(bf16)


def _na_attention(q, k, v, bias, layer):
    B, L, W = q.shape
    rows = L // GRID_W
    nblk = rows // NA_QROWS
    tq = NA_QROWS * GRID_W

    def bias_map(b, i):
        variant = jnp.where(i == 0, 0, jnp.where(i == nblk - 1, 2, 1))
        return (layer, variant, 0, 0, 0)

    full = pl.BlockSpec((None, L, W), lambda b, i: (b, 0, 0))
    return pl.pallas_call(
        functools.partial(_na_body, rows=rows),
        grid=(B, nblk),
        in_specs=[pl.BlockSpec((None, tq, W), lambda b, i: (b, i, 0)), full, full,
                  pl.BlockSpec((None, None, NA_HEADS, tq, NA_WROWS * GRID_W), bias_map)],
        out_specs=pl.BlockSpec((None, tq, W), lambda b, i: (b, i, 0)),
        out_shape=jax.ShapeDtypeStruct((B, L, W), bf16),
        compiler_params=_cparams(("parallel", "arbitrary")),
        name="na_attn",
    )(q, k, v, bias)


def _na_bias_body(r_ref, o_ref):
    r = r_ref[...]
    qc = lax.broadcasted_iota(jnp.int32, (GRID_W, LANES), 0)
    lane = lax.broadcasted_iota(jnp.int32, (GRID_W, LANES), 1)
    lo_half = lane < GRID_W
    kc = jnp.where(lo_half, lane, lane - GRID_W)
    cs = jnp.clip(qc - NA_COLS // 2, 0, GRID_W - NA_COLS)
    cvalid = (kc >= cs) & (kc < cs + NA_COLS)
    neg = jnp.full((GRID_W, LANES), NEG_INF, f32)
    tiles = []
    for dr in range(2 * NA_ROWS - 1):
        rowb = jnp.broadcast_to(r[dr:dr + 1, :], (GRID_W, 2 * LANES))
        base = 2 * LANES - (NA_COLS - 1) - NA_BIAS_LANE0
        t_lo = pltpu.roll(rowb, base, 1, stride=1, stride_axis=0)[:, :LANES]
        t_hi = pltpu.roll(rowb, base + GRID_W, 1, stride=1, stride_axis=0)[:, :LANES]
        tiles.append(jnp.where(cvalid, jnp.where(lo_half, t_lo, t_hi), neg))
    for v, (off, lo) in enumerate(((0, 0), (-(NA_ROWS // 2), None), (-NA_ROWS, NA_ROWS // 2))):
        for rl in range(NA_QROWS):
            cols = []
            for kl in range(NA_WROWS):
                d = off + kl - rl
                ok = (-(NA_ROWS // 2) <= d < NA_ROWS // 2) if lo is None else (lo <= kl < lo + NA_ROWS)
                cols.append(tiles[d + NA_ROWS - 1] if ok else neg)
            pairs = [jnp.where(lo_half, cols[2 * j], cols[2 * j + 1]) for j in range(NA_WROWS // 2)]
            o_ref[v, rl * GRID_W:(rl + 1) * GRID_W, :] = jnp.concatenate(pairs, axis=1)


def _na_bias_tables(rpb):
    depth, nh, nr, nc = rpb.shape
    rp = jnp.pad(rpb.astype(f32),
                 ((0, 0), (0, 0), (0, 16 - nr), (NA_BIAS_LANE0, 2 * LANES - NA_BIAS_LANE0 - nc)))
    tq, tkw = NA_QROWS * GRID_W, NA_WROWS * GRID_W
    return pl.pallas_call(
        _na_bias_body,
        grid=(depth, nh),
        in_specs=[pl.BlockSpec((None, None, 16, 2 * LANES), lambda l, h: (l, h, 0, 0))],
        out_specs=pl.BlockSpec((None, 3, None, tq, tkw), lambda l, h: (l, 0, h, 0, 0)),
        out_shape=jax.ShapeDtypeStruct((depth, 3, nh, tq, tkw), f32),
        compiler_params=_cparams(("parallel", "parallel")),
        name="na_bias",
    )(rp)


def _mla_body(qt_ref, k_ref, vt_ref, o_ref, s_scr, *, nq, nk):
    tq, tk = MLA_TQ, MLA_TK
    ones = jnp.ones((MLA_VROWS - MLA_V, tk), bf16)
    chains = [(h, c) for h in range(2) for c in range(tq // MLA_CW)]
    nch = len(chains)

    def scores(t, slot):
        qoff = pl.multiple_of((t // nk) * tq, tq)
        koff = pl.multiple_of((t % nk) * tk, tk)
        qt = qt_ref[:, pl.ds(qoff, tq)]
        kb = k_ref[pl.ds(koff, tk), :]
        cmax = []
        for n, (h, c) in enumerate(chains):
            s = jnp.dot(kb[:, h * MLA_PAD:(h + 1) * MLA_PAD],
                        qt[h * MLA_PAD:(h + 1) * MLA_PAD, c * MLA_CW:(c + 1) * MLA_CW],
                        preferred_element_type=f32)
            s_scr[slot, n] = s
            cmax.append(jnp.max(s, axis=0, keepdims=True))
        return tuple(cmax)

    def accumulate(t, slot, cmax, state):
        j = t % nk
        qoff = pl.multiple_of((t // nk) * tq, tq)
        koff = pl.multiple_of(j * tk, tk)
        vt = vt_ref[:, pl.ds(koff, tk)]
        tile_start = j == 0
        new = []
        for n, (h, c) in enumerate(chains):
            m, acc = state[2 * n], state[2 * n + 1]
            m = jnp.where(tile_start, -jnp.inf, m)
            mn = jnp.maximum(m, cmax[n])
            al = jnp.exp2(m - mn)
            p = jnp.exp2(s_scr[slot, n] - mn).astype(bf16)
            vte = jnp.concatenate([vt[h * MLA_V:(h + 1) * MLA_V], ones], axis=0)
            acc = al * acc + jnp.dot(vte, p, preferred_element_type=f32)
            o_ref[h * MLA_V:(h + 1) * MLA_V, pl.ds(qoff + c * MLA_CW, MLA_CW)] = (
                acc[:MLA_V] / acc[MLA_V:MLA_V + 1]).astype(bf16)
            new += [mn, acc]
        return tuple(new)

    nt = nq * nk
    ahead, ns = MLA_AHEAD, MLA_SLOTS

    def trip(t0, carry, last):
        pend, state = list(carry[:nch * ahead]), carry[nch * ahead:]
        for u in range(ns):
            if not (last and u + ahead >= ns):
                pend += scores(t0 + u + ahead, (u + ahead) % ns)
            state = accumulate(t0 + u, u, tuple(pend[:nch]), state)
            pend = pend[nch:]
        return tuple(pend) + state

    m0 = jnp.full((1, MLA_CW), -jnp.inf, f32)
    a0 = jnp.zeros((MLA_VROWS, MLA_CW), f32)
    carry = ()
    for t in range(ahead):
        carry += scores(t, t)
    carry = lax.fori_loop(0, nt // ns - 1, lambda i, c: trip(i * ns, c, False), carry + (m0, a0) * nch)
    trip(nt - ns, carry, True)


def _mla_attention(mqt, mk, mvt):
    B, L, _ = mk.shape
    npair = MLA_HEADS // 2
    nq, nk = L // MLA_TQ, L // MLA_TK
    assert (nq * nk) % MLA_SLOTS == 0
    mode = dict(pipeline_mode=pl.Buffered(1)) if L * 2 * MLA_PAD * 2 > 4 * 1024 * 1024 else {}
    return pl.pallas_call(
        functools.partial(_mla_body, nq=nq, nk=nk),
        grid=(B, npair),
        in_specs=[pl.BlockSpec((None, 2 * MLA_PAD, L), lambda b, p: (b, p, 0), **mode),
                  pl.BlockSpec((None, L, 2 * MLA_PAD), lambda b, p: (b, 0, p), **mode),
                  pl.BlockSpec((None, 2 * MLA_V, L), lambda b, p: (b, p, 0), **mode)],
        out_specs=pl.BlockSpec((None, 2 * MLA_V, L), lambda b, p: (b, p, 0)),
        out_shape=jax.ShapeDtypeStruct((B, MLA_HEADS * MLA_V, L), bf16),
        scratch_shapes=[pltpu.VMEM((MLA_SLOTS, 2 * MLA_TQ // MLA_CW, MLA_TK, MLA_CW), f32)],
        compiler_params=_cparams(("parallel", "parallel")),
        name="mla_attn",
    )(mqt, mk, mvt)


def _toeplitz_body(pf_ref, bb_ref, c_ref, o_ref):
    b = bb_ref[...]
    pfr, pfi, pbr, pbi = pf_ref[0], pf_ref[1], pf_ref[2], pf_ref[3]
    bfr, bfi, bbr, bbi = b[:, 0:1], b[:, 1:2], b[:, 2:3], b[:, 3:4]
    rhs = jnp.concatenate([pfr * bfr - pfi * bfi, pfr * bfi + pfi * bfr,
                           pbr * bbr - pbi * bbi, pbr * bbi + pbi * bbr], axis=0)
    kern = jnp.dot(c_ref[...], rhs, precision=lax.Precision.HIGHEST, preferred_element_type=f32)
    for c in range(SSM_GC):
        rowb = jnp.broadcast_to(kern[c:c + 1, :], (LANES, 2 * LANES))
        toe = pltpu.roll(rowb, LANES + 1, 1, stride=1, stride_axis=0)
        o_ref[:, c * LANES:(c + 1) * LANES] = toe[:, :LANES].astype(bf16)


def _toeplitz_gen(pf, bb, cm):
    dg = pf.shape[0]
    n = SSM_GC * LANES
    return pl.pallas_call(
        _toeplitz_body,
        grid=(dg, SSM_GC),
        in_specs=[pl.BlockSpec((None, 4, SSM_STATE, 2 * LANES), lambda g, c: (g, 0, 0, 0)),
                  pl.BlockSpec((None, None, SSM_STATE, 4), lambda g, c: (g, c, 0, 0)),
                  pl.BlockSpec((None, SSM_GC, 4 * SSM_STATE), lambda g, c: (g, 0, 0))],
        out_specs=pl.BlockSpec((None, LANES, n), lambda g, c: (g, c, 0)),
        out_shape=jax.ShapeDtypeStruct((dg, n, n), bf16),
        compiler_params=_cparams(("parallel", "arbitrary")),
        name="s5_toeplitz",
    )(pf, bb, cm)


def _ssm_body(u_ref, t_ref, wv_ref, wc_ref, a_ref, d_ref, o_ref,
              vre, vim, xfre, xfim, xbre, xbim, *, nb, nj):
    m = nb * nj
    u32 = jnp.concatenate([u_ref[:, c, :] for c in range(SSM_GC)], axis=1)
    ub = u32.astype(bf16)
    y = jnp.dot(ub, t_ref[...], preferred_element_type=f32)
    v = jnp.dot(ub, wv_ref[...], preferred_element_type=f32)
    vre[...] = v[:, :LANES]
    vim[...] = v[:, LANES:]
    are = a_ref[0:1, :]
    aim = a_ref[1:2, :]
    is_fwd = lax.broadcasted_iota(jnp.int32, (nb, LANES), 1) < SSM_STATE

    def rows(j):
        return pl.ds(j, nb, stride=nj) if nb > 1 else pl.ds(j, 1)

    def step(k, carry):
        xr, xi = carry
        rf, rb = rows(k), rows(nj - 1 - k)
        xfre[rf, :] = xr
        xfim[rf, :] = xi
        xbre[rb, :] = xr
        xbim[rb, :] = xi
        vr = jnp.where(is_fwd, vre[rf, :], vre[rb, :])
        vi = jnp.where(is_fwd, vim[rf, :], vim[rb, :])
        return are * xr - aim * xi + vr, are * xi + aim * xr + vi

    z = jnp.zeros((nb, LANES), f32)
    lax.fori_loop(0, nj, step, (z, z))
    fwd_m = lax.broadcasted_iota(jnp.int32, (m, LANES), 1) < SSM_STATE
    xin = jnp.concatenate([jnp.where(fwd_m, xfre[...], xbre[...]),
                           jnp.where(fwd_m, xfim[...], xbim[...])], axis=1).astype(bf16)
    y = y + jnp.dot(xin, wc_ref[...], preferred_element_type=f32) + u32 * d_ref[...]
    for c in range(SSM_GC):
        o_ref[:, c, :] = y[:, c * LANES:(c + 1) * LANES]


def _ssm(ut, layer, S, nb, nj):
    m = nb * nj
    n = SSM_GC * LANES
    gspec = lambda tail: pl.BlockSpec((None,) + tail, lambda g: (layer * SSM_GROUPS + g,) + (0,) * len(tail))
    io = pl.BlockSpec((m, None, SSM_GC, LANES), lambda g: (0, g, 0, 0))
    out = pl.pallas_call(
        functools.partial(_ssm_body, nb=nb, nj=nj),
        grid=(SSM_GROUPS,),
        in_specs=[io, gspec((n, n)), gspec((n, 4 * SSM_STATE)), gspec((4 * SSM_STATE, n)),
                  gspec((2, LANES)), gspec((1, n))],
        out_specs=io,
        out_shape=jax.ShapeDtypeStruct((m, SSM_GROUPS, SSM_GC, LANES), f32),
        scratch_shapes=[pltpu.VMEM((m, LANES), f32)] * 6,
        compiler_params=_cparams(("parallel",)),
        name="s5_conv",
    )(ut, S["toe"], S["wv"], S["wc"], S["a"], S["d"])
    return out


def _ssm_tables(a_re, a_im, b_re, b_im, c_re, c_im, log_dt, ssm_d):
    depth = a_re.shape[0]
    dg = depth * SSM_GROUPS
    lam = lax.complex(a_re.astype(f32), a_im.astype(f32))
    dt = jnp.exp(log_dt.astype(f32))[..., None]
    lam_dt = lam * dt
    lam_bar = jnp.exp(lam_dt)
    b_bar = ((lam_bar - 1.0) / lam)[..., None] * lax.complex(b_re.astype(f32), b_im.astype(f32))
    cmat = lax.complex(c_re.astype(f32), c_im.astype(f32))
    steps = jnp.arange(LANES + 1, dtype=f32)
    pw = jnp.exp(lam_dt[..., None] * steps)
    pwf, pwb = pw[:, 0], pw[:, 1]

    e = np.arange(2 * LANES)
    f_ok = (e >= LANES - 1) & (e <= 2 * LANES - 2)
    b_ok = e <= LANES - 1
    pf_f = jnp.where(f_ok, pwf[..., np.clip(e - (LANES - 1), 0, LANES - 1)], 0.0)
    pf_b = jnp.where(b_ok, pwb[..., np.clip(LANES - 1 - e, 0, LANES - 1)], 0.0)
    pf = jnp.stack([pf_f.real, pf_f.imag, pf_b.real, pf_b.imag], axis=2)
    pf = pf.reshape(dg, 4, SSM_STATE, 2 * LANES)

    bbf, bbb = b_bar[:, 0], b_bar[:, 1]
    bb = jnp.stack([bbf.real, bbf.imag, bbb.real, bbb.imag], axis=-1)
    bb = jnp.transpose(bb, (0, 1, 3, 2, 4)).reshape(dg, SSM_GC, SSM_STATE, 4)

    cf, cb = cmat[:, 0], cmat[:, 1]
    cm = jnp.concatenate([cf.real, -cf.imag, cb.real, -cb.imag], axis=-1).reshape(dg, SSM_GC, 4 * SSM_STATE)

    s_idx = np.arange(LANES)
    cof = pwf[..., LANES - 1 - s_idx][..., None, :, :] * jnp.swapaxes(bbf, -1, -2)[..., None]
    cob = pwb[..., s_idx][..., None, :, :] * jnp.swapaxes(bbb, -1, -2)[..., None]
    cof = jnp.swapaxes(cof, -1, -2)
    cob = jnp.swapaxes(cob, -1, -2)
    wv = jnp.concatenate([cof.real, cob.real, cof.imag, cob.imag], axis=-1)
    wv = wv.reshape(dg, SSM_GC * LANES, 4 * SSM_STATE).astype(bf16)

    t_idx = np.arange(LANES)
    clf = cf[..., None] * pwf[..., None, :, :][..., t_idx + 1]
    clb = cb[..., None] * pwb[..., None, :, :][..., LANES - t_idx]
    to_rows = lambda a: jnp.transpose(a, (0, 1, 3, 2, 4)).reshape(dg, SSM_STATE, SSM_GC * LANES)
    wc = jnp.concatenate([to_rows(clf.real), to_rows(clb.real), to_rows(-clf.imag), to_rows(-clb.imag)], axis=1)
    wc = wc.astype(bf16)

    a128 = jnp.concatenate([pwf[..., LANES], pwb[..., LANES]], axis=-1)
    a = jnp.stack([a128.real, a128.imag], axis=2).reshape(dg, 2, LANES)
    d = jnp.repeat(ssm_d.astype(f32).reshape(dg, SSM_GC), LANES, axis=-1).reshape(dg, 1, SSM_GC * LANES)
    return dict(pf=pf, bb=bb, cm=cm, wv=wv, wc=wc, a=a, d=d)


def _mixffn_body(x_ref, mod_ref, yna_ref, ymlat_ref, yst_ref, gluw_ref, glub_ref, wo_ref, g2_ref,
                 wg_ref, wu_ref, wd_ref, o_ref, act_ref):
    ys = jnp.concatenate([yst_ref[c].reshape(SSM_WIDTH, LANES).T for c in range(yst_ref.shape[0])],
                         axis=0)
    ys = 0.5 * ys * (1.0 + jnp.tanh(math.sqrt(2.0 / math.pi) * (ys + 0.044715 * (ys * ys * ys))))
    gate = jax.nn.sigmoid(jnp.dot(ys.astype(bf16), gluw_ref[...], preferred_element_type=f32) + glub_ref[...])
    yssm = (ys * gate).astype(bf16)
    o1 = NA_WIDTH
    o2 = NA_WIDTH + MLA_HEADS * MLA_V
    mix = (jnp.dot(yna_ref[...], wo_ref[0:o1, :], preferred_element_type=f32)
           + lax.dot_general(ymlat_ref[...], wo_ref[o1:o2, :], _TN, preferred_element_type=f32)
           + jnp.dot(yssm, wo_ref[o2:, :], preferred_element_type=f32))
    x1 = x_ref[...] + mod_ref[2:3, :] * mix
    o_ref[...] = x1
    ms = jnp.mean(x1 * x1, axis=-1, keepdims=True)
    h2 = ((x1 * lax.rsqrt(ms + RMS_EPS) * g2_ref[...]) * (1.0 + mod_ref[4:5, :]) + mod_ref[3:4, :]).astype(bf16)

    th = 256
    for c in range(FFN_HIDDEN // th):
        cs = slice(c * th, (c + 1) * th)
        g = jnp.dot(h2, wg_ref[:, cs], preferred_element_type=f32)
        u = jnp.dot(h2, wu_ref[:, cs], preferred_element_type=f32)
        act_ref[:, cs] = (g * jax.nn.sigmoid(g) * u).astype(bf16)
    ffn = jnp.dot(act_ref[...], wd_ref[...], preferred_element_type=f32)
    o_ref[...] = o_ref[...] + mod_ref[5:6, :] * ffn


def _mixffn(x, mod, yna, ymlat, yst, layer, P):
    B, L, D = x.shape
    tm = TOKEN_TILE
    nt = L // tm
    cpt = tm // LANES
    tok = lambda w: pl.BlockSpec((None, tm, w), lambda b, i: (b, i, 0))

    def wspec(name):
        tail = P[name].shape[1:]
        return pl.BlockSpec((None,) + tail, lambda b, i: (layer,) + (0,) * len(tail), pipeline_mode=pl.Buffered(1))

    return pl.pallas_call(
        _mixffn_body,
        grid=(B, nt),
        in_specs=[tok(D), pl.BlockSpec((None, 6, D), lambda b, i: (b, 0, 0)), tok(NA_WIDTH),
                  pl.BlockSpec((None, MLA_HEADS * MLA_V, tm), lambda b, i: (b, 0, i)),
                  pl.BlockSpec((cpt, SSM_GROUPS, SSM_GC, LANES), lambda b, i: (b * nt + i, 0, 0, 0)),
                  wspec("glu_w"), wspec("glu_b"), wspec("w_out"), wspec("g2"),
                  wspec("w_gate"), wspec("w_up"), wspec("w_down")],
        out_specs=tok(D),
        out_shape=jax.ShapeDtypeStruct((B, L, D), f32),
        scratch_shapes=[pltpu.VMEM((tm, FFN_HIDDEN), bf16)],
        compiler_params=_cparams(("parallel", "parallel")),
        name="mix_ffn",
    )(x, mod, yna, ymlat, yst, P["glu_w"], P["glu_b"], P["w_out"], P["g2"], P["w_gate"], P["w_up"], P["w_down"])


def _prep_params(norm1_g, w_in, na_q_g, na_k_g, mla_cq_g, mla_ckv_g, mla_w_uq, mla_w_ukv, mla_qn_g, mla_kn_g,
                 mla_qr_g, mla_kr_g, glu_w, glu_b, w_out, norm2_g, ffn_w_gate, ffn_w_up, ffn_w_down):
    depth = w_in.shape[0]
    o1 = 3 * NA_WIDTH
    o2 = o1 + MLA_Q_LORA
    o3 = o2 + MLA_KV_LORA
    o4 = o3 + MLA_ROPE
    row = lambda a: a.astype(f32)[:, None, :]
    zeros = lambda *s: jnp.zeros(s, f32)
    P = {}
    P["g1"] = row(norm1_g)
    P["g2"] = row(norm2_g)
    P["w_na"] = w_in[:, :, :o1].astype(bf16)
    P["w_c"] = w_in[:, :, o1:o3].astype(bf16)
    P["w_kr"] = jnp.concatenate([zeros(depth, D_MODEL, MLA_NOPE), w_in[:, :, o3:o4],
                                 zeros(depth, D_MODEL, MLA_PAD - MLA_NOPE - MLA_ROPE)], axis=-1).astype(bf16)
    P["w_ut"] = jnp.swapaxes(w_in[:, :, o4:], 1, 2).astype(bf16)
    wq = mla_w_uq.reshape(depth, MLA_Q_LORA, MLA_HEADS, MLA_NOPE + MLA_ROPE)
    wq = jnp.concatenate([wq, zeros(depth, MLA_Q_LORA, MLA_HEADS, MLA_PAD - MLA_NOPE - MLA_ROPE)], axis=-1)
    P["w_uq"] = wq.reshape(depth, MLA_Q_LORA, MLA_HEADS * MLA_PAD).astype(bf16)
    wkv = mla_w_ukv.reshape(depth, MLA_KV_LORA, MLA_HEADS, MLA_NOPE + MLA_V)
    wk = jnp.concatenate([wkv[..., :MLA_NOPE], zeros(depth, MLA_KV_LORA, MLA_HEADS, MLA_PAD - MLA_NOPE)], axis=-1)
    P["w_ukv_k"] = wk.reshape(depth, MLA_KV_LORA, MLA_HEADS * MLA_PAD).astype(bf16)
    P["w_ukv_vt"] = jnp.swapaxes(wkv[..., MLA_NOPE:].reshape(depth, MLA_KV_LORA, MLA_HEADS * MLA_V), 1, 2).astype(bf16)
    P["gq_na"] = row(jnp.tile(na_q_g, (1, NA_HEADS))) * (NA_HEAD_DIM ** -0.5)
    P["gk_na"] = row(jnp.tile(na_k_g, (1, NA_HEADS)))
    P["g_cq"] = row(mla_cq_g)
    P["g_ckv"] = row(mla_ckv_g)
    scale = (MLA_NOPE + MLA_ROPE) ** -0.5 * math.log2(math.e)
    pad = MLA_PAD - MLA_NOPE - MLA_ROPE
    gq = jnp.concatenate([mla_qn_g, mla_qr_g, zeros(depth, pad)], axis=-1) * scale
    P["gq"] = row(jnp.tile(gq, (1, MLA_HEADS)))
    gk = jnp.concatenate([mla_kn_g, zeros(depth, MLA_PAD - MLA_NOPE)], axis=-1)
    P["gk"] = row(jnp.tile(gk, (1, MLA_HEADS)))
    P["gkr"] = row(jnp.concatenate([zeros(depth, MLA_NOPE), mla_kr_g, zeros(depth, pad)], axis=-1))
    invq = np.concatenate([np.full(MLA_NOPE, 1.0 / MLA_NOPE), np.full(MLA_ROPE, 1.0 / MLA_ROPE), np.zeros(pad)])
    invk = np.concatenate([np.full(MLA_NOPE, 1.0 / MLA_NOPE), np.zeros(MLA_PAD - MLA_NOPE)])
    P["invq"] = jnp.asarray(np.tile(invq, MLA_HEADS)[None, :], f32)
    P["invk"] = jnp.asarray(np.tile(invk, MLA_HEADS)[None, :], f32)
    lane = np.arange(NA_WIDTH)
    P["G_na"] = jnp.asarray((lane[:, None] // NA_HEAD_DIM) == (lane[None, :] // NA_HEAD_DIM), bf16)
    lane = np.arange(2 * MLA_PAD)
    grp = np.where(lane % MLA_PAD < MLA_NOPE, 0, np.where(lane % MLA_PAD < MLA_NOPE + MLA_ROPE, 1, 2))
    same = (lane[:, None] // MLA_PAD == lane[None, :] // MLA_PAD) & (grp[:, None] == grp[None, :]) & (grp[:, None] < 2)
    P["G_qm"] = jnp.asarray(same, bf16)
    P["glu_w"] = glu_w.astype(bf16)
    P["glu_b"] = row(glu_b)
    P["w_out"] = w_out.astype(bf16)
    P["w_gate"] = ffn_w_gate.astype(bf16)
    P["w_up"] = ffn_w_up.astype(bf16)
    P["w_down"] = ffn_w_down.astype(bf16)
    return P


def _rope_tables(length):
    inv = 1.0 / (ROPE_BASE ** (jnp.arange(0, MLA_ROPE, 2, dtype=f32) / MLA_ROPE))
    ang = jnp.arange(length, dtype=f32)[:, None] * inv[None, :]
    cos, sin = jnp.cos(ang), jnp.sin(ang)
    half = MLA_ROPE // 2
    z = lambda w: jnp.zeros((length, w), f32)
    pad = MLA_PAD - MLA_NOPE - MLA_ROPE
    cos_t = jnp.concatenate([jnp.ones((length, MLA_NOPE), f32), cos, cos, z(pad)], axis=-1)
    sin_lo = jnp.concatenate([z(MLA_NOPE), -sin, z(half), z(pad)], axis=-1)
    sin_hi = jnp.concatenate([z(MLA_NOPE), z(half), sin, z(pad)], axis=-1)
    return cos_t, sin_lo, sin_hi


def _trunk(x, mods, P, S, bias):
    B, L, _ = x.shape
    rope_tabs = _rope_tables(L)
    nj = L // LANES
    for layer in range(len(mods)):
        mod = mods[layer]
        naq, nak, nav, mq, mk, mvt, ut = _inproj(x, mod, layer, P, rope_tabs)
        yna = _na_attention(naq, nak, nav, bias, layer)
        ymla = _mla_attention(mq, mk, mvt)
        yst = _ssm(ut, layer, S, B, nj)
        x = _mixffn(x, mod, yna, ymla, yst, layer, P)
    return x


def kernel(x_prompt, x_sample, c_prompt, c_sample, ada_w, ada_b, norm1_g, w_in, na_q_g, na_k_g, na_rpb, mla_cq_g,
           mla_ckv_g, mla_w_uq, mla_w_ukv, mla_qn_g, mla_kn_g, mla_qr_g, mla_kr_g, ssm_a_re, ssm_a_im, ssm_b_re,
           ssm_b_im, ssm_c_re, ssm_c_im, ssm_log_dt, ssm_d, glu_w, glu_b, w_out, norm2_g, ffn_w_gate, ffn_w_up,
           ffn_w_down):
    depth = w_in.shape[0]
    nbp, nbs = c_prompt.shape[0], c_sample.shape[0]
    rows = -(-(nbp + nbs) // 8) * 8
    c_all = jnp.concatenate([c_prompt, c_sample, jnp.zeros((rows - nbp - nbs, D_MODEL), f32)], axis=0)
    mod = _modulation(c_all, ada_w, ada_b).reshape(depth, rows, 6, D_MODEL)
    mods_p = [mod[l, :nbp] for l in range(depth)]
    mods_s = [mod[l, nbp:nbp + nbs] for l in range(depth)]

    P = _prep_params(norm1_g, w_in, na_q_g, na_k_g, mla_cq_g, mla_ckv_g, mla_w_uq, mla_w_ukv, mla_qn_g, mla_kn_g,
                     mla_qr_g, mla_kr_g, glu_w, glu_b, w_out, norm2_g, ffn_w_gate, ffn_w_up, ffn_w_down)
    S = _ssm_tables(ssm_a_re, ssm_a_im, ssm_b_re, ssm_b_im, ssm_c_re, ssm_c_im, ssm_log_dt, ssm_d)
    S["toe"] = _toeplitz_gen(S["pf"], S["bb"], S["cm"])
    bias = _na_bias_tables(na_rpb)

    y_prompt = _trunk(x_prompt, mods_p, P, S, bias)
    y_sample = _trunk(x_sample, mods_s, P, S, bias)
    return (y_prompt, y_sample)
```

```python
import functools
import math

import numpy as np
import jax
import jax.numpy as jnp
from jax import lax
from jax.experimental import pallas as pl
from jax.experimental.pallas import tpu as pltpu

f32 = jnp.float32
bf16 = jnp.bfloat16

D_MODEL = 1024
GRID_W = 64
NA_HEADS = 4
NA_HEAD_DIM = 64
NA_WIDTH = NA_HEADS * NA_HEAD_DIM
NA_ROWS = 8
NA_COLS = 16
MLA_HEADS = 8
MLA_NOPE = 64
MLA_ROPE = 32
MLA_V = 64
MLA_Q_LORA = 384
MLA_KV_LORA = 256
ROPE_BASE = 10000.0
SSM_GROUPS = 16
SSM_GC = 16
SSM_STATE = 64
SSM_WIDTH = SSM_GROUPS * SSM_GC
FFN_HIDDEN = 2816
RMS_EPS = 1e-6
NEG_INF = -1e30

LANES = 128
MLA_PAD = 128
NA_QROWS = 4
NA_WROWS = NA_QROWS + NA_ROWS
NA_BIAS_LANE0 = 200
TOKEN_TILE = 512
MLA_CW = 256
MLA_TQ = MLA_CW
MLA_TK = 1024
MLA_KC = 256
MLA_AHEAD = 2
MLA_SLOTS = 2 * MLA_AHEAD
MLA_VROWS = MLA_V + 16
VMEM_LIMIT = 48 * 1024 * 1024

_NT = (((1,), (1,)), ((), ()))
_TN = (((0,), (0,)), ((), ()))


def _cparams(sem):
    return pltpu.CompilerParams(dimension_semantics=sem, vmem_limit_bytes=VMEM_LIMIT)


def _layer_spec(tail, layer):
    n = len(tail)
    return pl.BlockSpec((None,) + tuple(tail), lambda *_: (layer,) + (0,) * n)


def _mod_body(c_ref, w_ref, b_ref, o_ref):
    c = c_ref[...]
    s = c * jax.nn.sigmoid(c)
    o_ref[...] = jnp.dot(s, w_ref[...], precision=lax.Precision.HIGHEST, preferred_element_type=f32) + b_ref[...]


def _modulation(c_all, ada_w, ada_b):
    depth, d, n = ada_w.shape
    tn = 1536
    rows = c_all.shape[0]
    return pl.pallas_call(
        _mod_body,
        grid=(depth, n // tn),
        in_specs=[
            pl.BlockSpec((rows, d), lambda l, j: (0, 0)),
            pl.BlockSpec((None, d, tn), lambda l, j: (l, 0, j)),
            pl.BlockSpec((None, 1, tn), lambda l, j: (l, 0, j)),
        ],
        out_specs=pl.BlockSpec((None, rows, tn), lambda l, j: (l, 0, j)),
        out_shape=jax.ShapeDtypeStruct((depth, rows, n), f32),
        compiler_params=_cparams(("arbitrary", "arbitrary")),
        name="adaln_mod",
    )(c_all, ada_w, ada_b.reshape(depth, 1, n))


def _rope(x, cos, sin_lo, sin_hi):
    return x * cos + pltpu.roll(x, LANES - 16, 1) * sin_lo + pltpu.roll(x, 16, 1) * sin_hi


def _inproj_body(x_ref, mod_ref, g1_ref, wna_ref, wc_ref, wkr_ref, wut_ref, wuq_ref, wukvk_ref, wukvvt_ref,
                 gqna_ref, gkna_ref, gcq_ref, gckv_ref, gq_ref, gk_ref, gkr_ref, invq_ref, invk_ref,
                 gna_ref, gqm_ref, cos_ref, sinlo_ref, sinhi_ref,
                 naq_ref, nak_ref, nav_ref, mqt_ref, mk_ref, mvt_ref, ut_ref):
    x = x_ref[...]
    shift1 = mod_ref[0:1, :]
    scale1 = mod_ref[1:2, :]
    ms = jnp.mean(x * x, axis=-1, keepdims=True)
    h = (x * lax.rsqrt(ms + RMS_EPS) * g1_ref[...]) * (1.0 + scale1) + shift1
    hb = h.astype(bf16)

    z = jnp.dot(hb, wna_ref[...], preferred_element_type=f32)
    q = z[:, :NA_WIDTH]
    k = z[:, NA_WIDTH:2 * NA_WIDTH]
    gna = gna_ref[...]
    ssq = jnp.dot((q * q).astype(bf16), gna, preferred_element_type=f32) * (1.0 / NA_HEAD_DIM)
    ssk = jnp.dot((k * k).astype(bf16), gna, preferred_element_type=f32) * (1.0 / NA_HEAD_DIM)
    naq_ref[...] = (q * lax.rsqrt(ssq + RMS_EPS) * gqna_ref[...]).astype(bf16)
    nak_ref[...] = (k * lax.rsqrt(ssk + RMS_EPS) * gkna_ref[...]).astype(bf16)
    nav_ref[...] = z[:, 2 * NA_WIDTH:].astype(bf16)

    zc = jnp.dot(hb, wc_ref[...], preferred_element_type=f32)
    cq = zc[:, :MLA_Q_LORA]
    ckv = zc[:, MLA_Q_LORA:]
    cqn = (cq * lax.rsqrt(jnp.mean(cq * cq, axis=-1, keepdims=True) + RMS_EPS) * gcq_ref[...]).astype(bf16)
    ckvn = (ckv * lax.rsqrt(jnp.mean(ckv * ckv, axis=-1, keepdims=True) + RMS_EPS) * gckv_ref[...]).astype(bf16)

    cos = cos_ref[...]
    sin_lo = sinlo_ref[...]
    sin_hi = sinhi_ref[...]
    gqm = gqm_ref[...]

    kr = jnp.dot(hb, wkr_ref[...], preferred_element_type=f32)
    kr_ms = jnp.sum(kr * kr, axis=-1, keepdims=True) * (1.0 / MLA_ROPE)
    kr = _rope(kr * lax.rsqrt(kr_ms + RMS_EPS) * gkr_ref[...], cos, sin_lo, sin_hi)

    qraw = jnp.dot(cqn, wuq_ref[...], preferred_element_type=f32)
    kraw = jnp.dot(ckvn, wukvk_ref[...], preferred_element_type=f32)
    for p in range(MLA_HEADS // 2):
        sl = slice(2 * p * MLA_PAD, 2 * (p + 1) * MLA_PAD)
        qs = qraw[:, sl]
        ss = jnp.dot((qs * qs).astype(bf16), gqm, preferred_element_type=f32) * invq_ref[:, sl]
        qn = qs * lax.rsqrt(ss + RMS_EPS) * gq_ref[:, sl]
        ks = kraw[:, sl]
        ss = jnp.dot((ks * ks).astype(bf16), gqm, preferred_element_type=f32) * invk_ref[:, sl]
        kn = ks * lax.rsqrt(ss + RMS_EPS) * gk_ref[:, sl]
        for hh in range(2):
            lo = (2 * p + hh) * MLA_PAD
            piece = qn[:, hh * MLA_PAD:(hh + 1) * MLA_PAD]
            mqt_ref[lo:lo + MLA_PAD, :] = _rope(piece, cos, sin_lo, sin_hi).T.astype(bf16)
            mk_ref[:, lo:lo + MLA_PAD] = (kn[:, hh * MLA_PAD:(hh + 1) * MLA_PAD] + kr).astype(bf16)
    mvt_ref[...] = lax.dot_general(wukvvt_ref[...], ckvn, _NT, preferred_element_type=f32).astype(bf16)

    ut = lax.dot_general(wut_ref[...], hb, _NT, preferred_element_type=f32)
    for c in range(ut_ref.shape[0]):
        ut_ref[c] = ut[:, c * LANES:(c + 1) * LANES].reshape(SSM_GROUPS, SSM_GC, LANES)


def _inproj(x, mod, layer, P, rope_tabs):
    B, L, D = x.shape
    tm = TOKEN_TILE
    nt = L // tm
    cpt = tm // LANES
    tok = lambda w: pl.BlockSpec((None, tm, w), lambda b, i: (b, i, 0))
    pos = pl.BlockSpec((tm, LANES), lambda b, i: (i, 0))
    const2 = lambda a: pl.BlockSpec(a.shape, lambda b, i: (0, 0))
    lw = lambda name: _layer_spec(P[name].shape[1:], layer)
    names = ["g1", "w_na", "w_c", "w_kr", "w_ut", "w_uq", "w_ukv_k", "w_ukv_vt",
             "gq_na", "gk_na", "g_cq", "g_ckv", "gq", "gk", "gkr"]
    consts = ["invq", "invk", "G_na", "G_qm"]
    in_specs = ([tok(D), pl.BlockSpec((None, 6, D), lambda b, i: (b, 0, 0))]
                + [lw(n) for n in names] + [const2(P[n]) for n in consts] + [pos, pos, pos])
    out_shapes = (
        jax.ShapeDtypeStruct((B, L, NA_WIDTH), bf16),
        jax.ShapeDtypeStruct((B, L, NA_WIDTH), bf16),
        jax.ShapeDtypeStruct((B, L, NA_WIDTH), bf16),
        jax.ShapeDtypeStruct((B, MLA_HEADS * MLA_PAD, L), bf16),
        jax.ShapeDtypeStruct((B, L, MLA_HEADS * MLA_PAD), bf16),
        jax.ShapeDtypeStruct((B, MLA_HEADS * MLA_V, L), bf16),
        jax.ShapeDtypeStruct((B * L // LANES, SSM_GROUPS, SSM_GC, LANES), f32),
    )
    chan = lambda w: pl.BlockSpec((None, w, tm), lambda b, i: (b, 0, i))
    out_specs = (tok(NA_WIDTH), tok(NA_WIDTH), tok(NA_WIDTH), chan(MLA_HEADS * MLA_PAD), tok(MLA_HEADS * MLA_PAD),
                 chan(MLA_HEADS * MLA_V),
                 pl.BlockSpec((cpt, SSM_GROUPS, SSM_GC, LANES), lambda b, i: (b * nt + i, 0, 0, 0)))
    return pl.pallas_call(
        _inproj_body,
        grid=(B, nt),
        in_specs=in_specs,
        out_specs=out_specs,
        out_shape=out_shapes,
        compiler_params=_cparams(("parallel", "parallel")),
        name="in_proj",
    )(x, mod, *[P[n] for n in names], *[P[n] for n in consts], *rope_tabs)


def _na_body(q_ref, k_ref, v_ref, b_ref, o_ref, *, rows):
    i = pl.program_id(1)
    ws = jnp.clip(NA_QROWS * i - NA_ROWS // 2, 0, rows - NA_WROWS)
    start = pl.multiple_of(ws * GRID_W, GRID_W)
    kw = k_ref[pl.ds(start, NA_WROWS * GRID_W), :]
    vw = v_ref[pl.ds(start, NA_WROWS * GRID_W), :]
    q = q_ref[...]
    lane = lax.broadcasted_iota(jnp.int32, (1, NA_WIDTH), 1)
    out = jnp.zeros((NA_QROWS * GRID_W, NA_WIDTH), f32)
    for h in range(NA_HEADS):
        hm = (lane >= h * NA_HEAD_DIM) & (lane < (h + 1) * NA_HEAD_DIM)
        qh = jnp.where(hm, q, jnp.zeros_like(q))
        s = lax.dot_general(qh, kw, _NT, preferred_element_type=f32) + b_ref[h]
        m = jnp.max(s, axis=-1, keepdims=True)
        p = jnp.exp(s - m)
        l = jnp.sum(p, axis=-1, keepdims=True)
        o = jnp.dot(p.astype(bf16), vw, preferred_element_type=f32)
        out = jnp.where(hm, o / l, out)
    o_ref[...] = out.astype(bf16)


def _na_attention(q, k, v, bias, layer):
    B, L, W = q.shape
    rows = L // GRID_W
    nblk = rows // NA_QROWS
    tq = NA_QROWS * GRID_W

    def bias_map(b, i):
        variant = jnp.where(i == 0, 0, jnp.where(i == nblk - 1, 2, 1))
        return (layer, variant, 0, 0, 0)

    full = pl.BlockSpec((None, L, W), lambda b, i: (b, 0, 0))
    return pl.pallas_call(
        functools.partial(_na_body, rows=rows),
        grid=(B, nblk),
        in_specs=[pl.BlockSpec((None, tq, W), lambda b, i: (b, i, 0)), full, full,
                  pl.BlockSpec((None, None, NA_HEADS, tq, NA_WROWS * GRID_W), bias_map)],
        out_specs=pl.BlockSpec((None, tq, W), lambda b, i: (b, i, 0)),
        out_shape=jax.ShapeDtypeStruct((B, L, W), bf16),
        compiler_params=_cparams(("parallel", "arbitrary")),
        name="na_attn",
    )(q, k, v, bias)


def _na_bias_body(r_ref, o_ref):
    r = r_ref[...]
    qc = lax.broadcasted_iota(jnp.int32, (GRID_W, LANES), 0)
    lane = lax.broadcasted_iota(jnp.int32, (GRID_W, LANES), 1)
    lo_half = lane < GRID_W
    kc = jnp.where(lo_half, lane, lane - GRID_W)
    cs = jnp.clip(qc - NA_COLS // 2, 0, GRID_W - NA_COLS)
    cvalid = (kc >= cs) & (kc < cs + NA_COLS)
    neg = jnp.full((GRID_W, LANES), NEG_INF, f32)
    tiles = []
    for dr in range(2 * NA_ROWS - 1):
        rowb = jnp.broadcast_to(r[dr:dr + 1, :], (GRID_W, 2 * LANES))
        base = 2 * LANES - (NA_COLS - 1) - NA_BIAS_LANE0
        t_lo = pltpu.roll(rowb, base, 1, stride=1, stride_axis=0)[:, :LANES]
        t_hi = pltpu.roll(rowb, base + GRID_W, 1, stride=1, stride_axis=0)[:, :LANES]
        tiles.append(jnp.where(cvalid, jnp.where(lo_half, t_lo, t_hi), neg))
    for v, (off, lo) in enumerate(((0, 0), (-(NA_ROWS // 2), None), (-NA_ROWS, NA_ROWS // 2))):
        for rl in range(NA_QROWS):
            cols = []
            for kl in range(NA_WROWS):
                d = off + kl - rl
                ok = (-(NA_ROWS // 2) <= d < NA_ROWS // 2) if lo is None else (lo <= kl < lo + NA_ROWS)
                cols.append(tiles[d + NA_ROWS - 1] if ok else neg)
            pairs = [jnp.where(lo_half, cols[2 * j], cols[2 * j + 1]) for j in range(NA_WROWS // 2)]
            o_ref[v, rl * GRID_W:(rl + 1) * GRID_W, :] = jnp.concatenate(pairs, axis=1)


def _na_bias_tables(rpb):
    depth, nh, nr, nc = rpb.shape
    rp = jnp.pad(rpb.astype(f32),
                 ((0, 0), (0, 0), (0, 16 - nr), (NA_BIAS_LANE0, 2 * LANES - NA_BIAS_LANE0 - nc)))
    tq, tkw = NA_QROWS * GRID_W, NA_WROWS * GRID_W
    return pl.pallas_call(
        _na_bias_body,
        grid=(depth, nh),
        in_specs=[pl.BlockSpec((None, None, 16, 2 * LANES), lambda l, h: (l, h, 0, 0))],
        out_specs=pl.BlockSpec((None, 3, None, tq, tkw), lambda l, h: (l, 0, h, 0, 0)),
        out_shape=jax.ShapeDtypeStruct((depth, 3, nh, tq, tkw), f32),
        compiler_params=_cparams(("parallel", "parallel")),
        name="na_bias",
    )(rp)


def _mla_body(qt_ref, k_ref, vt_ref, o_ref, s_scr, *, nq, nk):
    tq, tk = MLA_TQ, MLA_TK
    ones = jnp.ones((MLA_VROWS - MLA_V, MLA_KC), bf16)
    chains = [(h, c) for h in range(2) for c in range(tq // MLA_CW)]
    nch = len(chains)

    def scores(t, slot):
        qoff = pl.multiple_of((t // nk) * tq, tq)
        koff = pl.multiple_of((t % nk) * tk, tk)
        qt = qt_ref[:, pl.ds(qoff, tq)]
        kb = k_ref[pl.ds(koff, tk), :]
        cmax = []
        for n, (h, c) in enumerate(chains):
            s = jnp.dot(kb[:, h * MLA_PAD:(h + 1) * MLA_PAD],
                        qt[h * MLA_PAD:(h + 1) * MLA_PAD, c * MLA_CW:(c + 1) * MLA_CW],
                        preferred_element_type=f32)
            s_scr[slot, n] = s
            cmax.append(jnp.max(s, axis=0, keepdims=True))
        return tuple(cmax)

    nkc = tk // MLA_KC

    def step(t_s, slot_s, t_a, slot_a, cmax, state):
        if t_s is not None:
            qoff_s = pl.multiple_of((t_s // nk) * tq, tq)
            koff_s = pl.multiple_of((t_s % nk) * tk, tk)
            qt = qt_ref[:, pl.ds(qoff_s, tq)]
        j = t_a % nk
        qoff = pl.multiple_of((t_a // nk) * tq, tq)
        koff = pl.multiple_of(j * tk, tk)
        tile_start = j == 0
        mns, als = [], []
        for n in range(nch):
            m = jnp.where(tile_start, -jnp.inf, state[2 * n])
            mns.append(jnp.maximum(m, cmax[n]))
            als.append(jnp.exp2(m - mns[n]))
        parts, cnew = [None] * nch, [None] * nch
        for kc in range(nkc):
            rows = slice(kc * MLA_KC, (kc + 1) * MLA_KC)
            if t_s is not None:
                kb = k_ref[pl.ds(koff_s + kc * MLA_KC, MLA_KC), :]
                for n, (h, c) in enumerate(chains):
                    s = jnp.dot(kb[:, h * MLA_PAD:(h + 1) * MLA_PAD],
                                qt[h * MLA_PAD:(h + 1) * MLA_PAD, c * MLA_CW:(c + 1) * MLA_CW],
                                preferred_element_type=f32)
                    s_scr[slot_s, n, rows, :] = s
                    cm = jnp.max(s, axis=0, keepdims=True)
                    cnew[n] = cm if cnew[n] is None else jnp.maximum(cnew[n], cm)
            vt = vt_ref[:, pl.ds(koff + kc * MLA_KC, MLA_KC)]
            for n, (h, c) in enumerate(chains):
                p = jnp.exp2(s_scr[slot_a, n, rows, :] - mns[n]).astype(bf16)
                vte = jnp.concatenate([vt[h * MLA_V:(h + 1) * MLA_V], ones], axis=0)
                d = jnp.dot(vte, p, preferred_element_type=f32)
                parts[n] = d if parts[n] is None else parts[n] + d
        new = []
        for n, (h, c) in enumerate(chains):
            acc = als[n] * state[2 * n + 1] + parts[n]
            o_ref[h * MLA_V:(h + 1) * MLA_V, pl.ds(qoff + c * MLA_CW, MLA_CW)] = (
                acc[:MLA_V] / acc[MLA_V:MLA_V + 1]).astype(bf16)
            new += [mns[n], acc]
        return (tuple(cnew) if t_s is not None else ()), tuple(new)

    nt = nq * nk
    ahead, ns = MLA_AHEAD, MLA_SLOTS

    def trip(t0, carry, last):
        pend, state = list(carry[:nch * ahead]), carry[nch * ahead:]
        for u in range(ns):
            t_s = None if (last and u + ahead >= ns) else t0 + u + ahead
            cn, state = step(t_s, (u + ahead) % ns, t0 + u, u, tuple(pend[:nch]), state)
            pend = pend[nch:] + list(cn)
        return tuple(pend) + state

    m0 = jnp.full((1, MLA_CW), -jnp.inf, f32)
    a0 = jnp.zeros((MLA_VROWS, MLA_CW), f32)
    carry = ()
    for t in range(ahead):
        carry += scores(t, t)
    carry = lax.fori_loop(0, nt // ns - 1, lambda i, c: trip(i * ns, c, False), carry + (m0, a0) * nch)
    trip(nt - ns, carry, True)


def _mla_attention(mqt, mk, mvt):
    B, L, _ = mk.shape
    npair = MLA_HEADS // 2
    nq, nk = L // MLA_TQ, L // MLA_TK
    assert (nq * nk) % MLA_SLOTS == 0
    mode = dict(pipeline_mode=pl.Buffered(1)) if L * 2 * MLA_PAD * 2 > 4 * 1024 * 1024 else {}
    return pl.pallas_call(
        functools.partial(_mla_body, nq=nq, nk=nk),
        grid=(B, npair),
        in_specs=[pl.BlockSpec((None, 2 * MLA_PAD, L), lambda b, p: (b, p, 0), **mode),
                  pl.BlockSpec((None, L, 2 * MLA_PAD), lambda b, p: (b, 0, p), **mode),
                  pl.BlockSpec((None, 2 * MLA_V, L), lambda b, p: (b, p, 0), **mode)],
        out_specs=pl.BlockSpec((None, 2 * MLA_V, L), lambda b, p: (b, p, 0)),
        out_shape=jax.ShapeDtypeStruct((B, MLA_HEADS * MLA_V, L), bf16),
        scratch_shapes=[pltpu.VMEM((MLA_SLOTS, 2 * MLA_TQ // MLA_CW, MLA_TK, MLA_CW), f32)],
        compiler_params=_cparams(("parallel", "parallel")),
        name="mla_attn",
    )(mqt, mk, mvt)


def _toeplitz_body(g_ref, pf_ref, o_ref):
    kern = jnp.dot(g_ref[...], pf_ref[...], precision=lax.Precision.HIGHEST, preferred_element_type=f32)
    bits = lax.bitcast_convert_type(kern.astype(bf16).astype(f32), jnp.int32)
    for cp in range(SSM_GC):
        for c in range(0, SSM_GC, 2):
            r = cp * SSM_GC + c
            packed = lax.shift_right_logical(bits[r:r + 1, :], 16) | bits[r + 1:r + 2, :]
            rowb = jnp.broadcast_to(packed, (LANES, 2 * LANES))
            toe = pltpu.roll(rowb, LANES + 1, 1, stride=1, stride_axis=0)[:, :LANES]
            even = lax.bitcast_convert_type(lax.shift_left(toe, 16), f32)
            odd = lax.bitcast_convert_type(toe & jnp.int32(-65536), f32)
            o_ref[cp * LANES:(cp + 1) * LANES, c * LANES:(c + 1) * LANES] = even.astype(bf16)
            o_ref[cp * LANES:(cp + 1) * LANES, (c + 1) * LANES:(c + 2) * LANES] = odd.astype(bf16)


def _toeplitz_gen(gm, pf):
    dg = pf.shape[0]
    n = SSM_GC * LANES
    k = 4 * SSM_STATE
    return pl.pallas_call(
        _toeplitz_body,
        grid=(dg,),
        in_specs=[pl.BlockSpec((None, SSM_GC * SSM_GC, k), lambda g: (g, 0, 0)),
                  pl.BlockSpec((None, k, 2 * LANES), lambda g: (g, 0, 0))],
        out_specs=pl.BlockSpec((None, n, n), lambda g: (g, 0, 0)),
        out_shape=jax.ShapeDtypeStruct((dg, n, n), bf16),
        compiler_params=_cparams(("parallel",)),
        name="s5_toeplitz",
    )(gm, pf)


def _ssm_body(u_ref, t_ref, wv_ref, wc_ref, a_ref, d_ref, o_ref,
              vre, vim, xfre, xfim, xbre, xbim, *, nb, nj):
    m = nb * nj
    u32 = jnp.concatenate([u_ref[:, c, :] for c in range(SSM_GC)], axis=1)
    ub = u32.astype(bf16)
    y = jnp.dot(ub, t_ref[...], preferred_element_type=f32)
    v = jnp.dot(ub, wv_ref[...], preferred_element_type=f32)
    vre[...] = v[:, :LANES]
    vim[...] = v[:, LANES:]
    are = a_ref[0:1, :]
    aim = a_ref[1:2, :]
    is_fwd = lax.broadcasted_iota(jnp.int32, (nb, LANES), 1) < SSM_STATE

    def rows(j):
        return pl.ds(j, nb, stride=nj) if nb > 1 else pl.ds(j, 1)

    def step(k, carry):
        xr, xi = carry
        rf, rb = rows(k), rows(nj - 1 - k)
        xfre[rf, :] = xr
        xfim[rf, :] = xi
        xbre[rb, :] = xr
        xbim[rb, :] = xi
        vr = jnp.where(is_fwd, vre[rf, :], vre[rb, :])
        vi = jnp.where(is_fwd, vim[rf, :], vim[rb, :])
        return are * xr - aim * xi + vr, are * xi + aim * xr + vi

    z = jnp.zeros((nb, LANES), f32)
    lax.fori_loop(0, nj, step, (z, z))
    fwd_m = lax.broadcasted_iota(jnp.int32, (m, LANES), 1) < SSM_STATE
    xin = jnp.concatenate([jnp.where(fwd_m, xfre[...], xbre[...]),
                           jnp.where(fwd_m, xfim[...], xbim[...])], axis=1).astype(bf16)
    y = y + jnp.dot(xin, wc_ref[...], preferred_element_type=f32) + u32 * d_ref[...]
    for c in range(SSM_GC):
        o_ref[:, c, :] = y[:, c * LANES:(c + 1) * LANES]


def _ssm(ut, layer, S, nb, nj):
    m = nb * nj
    n = SSM_GC * LANES
    gspec = lambda tail: pl.BlockSpec((None,) + tail, lambda g: (layer * SSM_GROUPS + g,) + (0,) * len(tail))
    io = pl.BlockSpec((m, None, SSM_GC, LANES), lambda g: (0, g, 0, 0))
    out = pl.pallas_call(
        functools.partial(_ssm_body, nb=nb, nj=nj),
        grid=(SSM_GROUPS,),
        in_specs=[io, gspec((n, n)), gspec((n, 4 * SSM_STATE)), gspec((4 * SSM_STATE, n)),
                  gspec((2, LANES)), gspec((1, n))],
        out_specs=io,
        out_shape=jax.ShapeDtypeStruct((m, SSM_GROUPS, SSM_GC, LANES), f32),
        scratch_shapes=[pltpu.VMEM((m, LANES), f32)] * 6,
        compiler_params=_cparams(("parallel",)),
        name="s5_conv",
    )(ut, S["toe"], S["wv"], S["wc"], S["a"], S["d"])
    return out


def _ssm_tables(a_re, a_im, b_re, b_im, c_re, c_im, log_dt, ssm_d):
    depth = a_re.shape[0]
    dg = depth * SSM_GROUPS
    lam = lax.complex(a_re.astype(f32), a_im.astype(f32))
    dt = jnp.exp(log_dt.astype(f32))[..., None]
    lam_dt = lam * dt
    lam_bar = jnp.exp(lam_dt)
    b_bar = ((lam_bar - 1.0) / lam)[..., None] * lax.complex(b_re.astype(f32), b_im.astype(f32))
    cmat = lax.complex(c_re.astype(f32), c_im.astype(f32))
    steps = jnp.arange(LANES + 1, dtype=f32)
    pw = jnp.exp(lam_dt[..., None] * steps)
    pwf, pwb = pw[:, 0], pw[:, 1]

    e = np.arange(2 * LANES)
    f_ok = (e >= LANES - 1) & (e <= 2 * LANES - 2)
    b_ok = e <= LANES - 1
    pf_f = jnp.where(f_ok, pwf[..., np.clip(e - (LANES - 1), 0, LANES - 1)], 0.0)
    pf_b = jnp.where(b_ok, pwb[..., np.clip(LANES - 1 - e, 0, LANES - 1)], 0.0)
    pf = jnp.stack([pf_f.real, pf_f.imag, pf_b.real, pf_b.imag], axis=2)
    pf = pf.reshape(dg, 4 * SSM_STATE, 2 * LANES)

    bbf, bbb = b_bar[:, 0], b_bar[:, 1]
    cf, cb = cmat[:, 0], cmat[:, 1]
    gf = cf[:, :, None, :, :] * jnp.swapaxes(bbf, -1, -2)[:, :, :, None, :]
    gb = cb[:, :, None, :, :] * jnp.swapaxes(bbb, -1, -2)[:, :, :, None, :]
    gm = jnp.concatenate([gf.real, -gf.imag, gb.real, -gb.imag], axis=-1)
    gm = gm.reshape(dg, SSM_GC * SSM_GC, 4 * SSM_STATE)

    s_idx = np.arange(LANES)
    cof = pwf[..., LANES - 1 - s_idx][..., None, :, :] * jnp.swapaxes(bbf, -1, -2)[..., None]
    cob = pwb[..., s_idx][..., None, :, :] * jnp.swapaxes(bbb, -1, -2)[..., None]
    cof = jnp.swapaxes(cof, -1, -2)
    cob = jnp.swapaxes(cob, -1, -2)
    wv = jnp.concatenate([cof.real, cob.real, cof.imag, cob.imag], axis=-1)
    wv = wv.reshape(dg, SSM_GC * LANES, 4 * SSM_STATE).astype(bf16)

    t_idx = np.arange(LANES)
    clf = cf[..., None] * pwf[..., None, :, :][..., t_idx + 1]
    clb = cb[..., None] * pwb[..., None, :, :][..., LANES - t_idx]
    to_rows = lambda a: jnp.transpose(a, (0, 1, 3, 2, 4)).reshape(dg, SSM_STATE, SSM_GC * LANES)
    wc = jnp.concatenate([to_rows(clf.real), to_rows(clb.real), to_rows(-clf.imag), to_rows(-clb.imag)], axis=1)
    wc = wc.astype(bf16)

    a128 = jnp.concatenate([pwf[..., LANES], pwb[..., LANES]], axis=-1)
    a = jnp.stack([a128.real, a128.imag], axis=2).reshape(dg, 2, LANES)
    d = jnp.repeat(ssm_d.astype(f32).reshape(dg, SSM_GC), LANES, axis=-1).reshape(dg, 1, SSM_GC * LANES)
    return dict(pf=pf, gm=gm, wv=wv, wc=wc, a=a, d=d)


def _mixffn_body(x_ref, mod_ref, yna_ref, ymlat_ref, yst_ref, gluw_ref, glub_ref, wo_ref, g2_ref,
                 wg_ref, wu_ref, wd_ref, o_ref, act_ref):
    ys = jnp.concatenate([yst_ref[c].reshape(SSM_WIDTH, LANES).T for c in range(yst_ref.shape[0])],
                         axis=0)
    ys = 0.5 * ys * (1.0 + jnp.tanh(math.sqrt(2.0 / math.pi) * (ys + 0.044715 * (ys * ys * ys))))
    gate = jax.nn.sigmoid(jnp.dot(ys.astype(bf16), gluw_ref[...], preferred_element_type=f32) + glub_ref[...])
    yssm = (ys * gate).astype(bf16)
    o1 = NA_WIDTH
    o2 = NA_WIDTH + MLA_HEADS * MLA_V
    mix = (jnp.dot(yna_ref[...], wo_ref[0:o1, :], preferred_element_type=f32)
           + lax.dot_general(ymlat_ref[...], wo_ref[o1:o2, :], _TN, preferred_element_type=f32)
           + jnp.dot(yssm, wo_ref[o2:, :], preferred_element_type=f32))
    x1 = x_ref[...] + mod_ref[2:3, :] * mix
    o_ref[...] = x1
    ms = jnp.mean(x1 * x1, axis=-1, keepdims=True)
    h2 = ((x1 * lax.rsqrt(ms + RMS_EPS) * g2_ref[...]) * (1.0 + mod_ref[4:5, :]) + mod_ref[3:4, :]).astype(bf16)

    th = 256
    for c in range(FFN_HIDDEN // th):
        cs = slice(c * th, (c + 1) * th)
        g = jnp.dot(h2, wg_ref[:, cs], preferred_element_type=f32)
        u = jnp.dot(h2, wu_ref[:, cs], preferred_element_type=f32)
        act_ref[:, cs] = (g * jax.nn.sigmoid(g) * u).astype(bf16)
    ffn = jnp.dot(act_ref[...], wd_ref[...], preferred_element_type=f32)
    o_ref[...] = o_ref[...] + mod_ref[5:6, :] * ffn


def _mixffn(x, mod, yna, ymlat, yst, layer, P):
    B, L, D = x.shape
    tm = TOKEN_TILE
    nt = L // tm
    cpt = tm // LANES
    tok = lambda w: pl.BlockSpec((None, tm, w), lambda b, i: (b, i, 0))

    def wspec(name):
        tail = P[name].shape[1:]
        return pl.BlockSpec((None,) + tail, lambda b, i: (layer,) + (0,) * len(tail), pipeline_mode=pl.Buffered(1))

    return pl.pallas_call(
        _mixffn_body,
        grid=(B, nt),
        in_specs=[tok(D), pl.BlockSpec((None, 6, D), lambda b, i: (b, 0, 0)), tok(NA_WIDTH),
                  pl.BlockSpec((None, MLA_HEADS * MLA_V, tm), lambda b, i: (b, 0, i)),
                  pl.BlockSpec((cpt, SSM_GROUPS, SSM_GC, LANES), lambda b, i: (b * nt + i, 0, 0, 0)),
                  wspec("glu_w"), wspec("glu_b"), wspec("w_out"), wspec("g2"),
                  wspec("w_gate"), wspec("w_up"), wspec("w_down")],
        out_specs=tok(D),
        out_shape=jax.ShapeDtypeStruct((B, L, D), f32),
        scratch_shapes=[pltpu.VMEM((tm, FFN_HIDDEN), bf16)],
        compiler_params=_cparams(("parallel", "parallel")),
        name="mix_ffn",
    )(x, mod, yna, ymlat, yst, P["glu_w"], P["glu_b"], P["w_out"], P["g2"], P["w_gate"], P["w_up"], P["w_down"])


def _prep_params(norm1_g, w_in, na_q_g, na_k_g, mla_cq_g, mla_ckv_g, mla_w_uq, mla_w_ukv, mla_qn_g, mla_kn_g,
                 mla_qr_g, mla_kr_g, glu_w, glu_b, w_out, norm2_g, ffn_w_gate, ffn_w_up, ffn_w_down):
    depth = w_in.shape[0]
    o1 = 3 * NA_WIDTH
    o2 = o1 + MLA_Q_LORA
    o3 = o2 + MLA_KV_LORA
    o4 = o3 + MLA_ROPE
    row = lambda a: a.astype(f32)[:, None, :]
    zeros = lambda *s: jnp.zeros(s, f32)
    P = {}
    P["g1"] = row(norm1_g)
    P["g2"] = row(norm2_g)
    P["w_na"] = w_in[:, :, :o1].astype(bf16)
    P["w_c"] = w_in[:, :, o1:o3].astype(bf16)
    P["w_kr"] = jnp.concatenate([zeros(depth, D_MODEL, MLA_NOPE), w_in[:, :, o3:o4],
                                 zeros(depth, D_MODEL, MLA_PAD - MLA_NOPE - MLA_ROPE)], axis=-1).astype(bf16)
    P["w_ut"] = jnp.swapaxes(w_in[:, :, o4:], 1, 2).astype(bf16)
    wq = mla_w_uq.reshape(depth, MLA_Q_LORA, MLA_HEADS, MLA_NOPE + MLA_ROPE)
    wq = jnp.concatenate([wq, zeros(depth, MLA_Q_LORA, MLA_HEADS, MLA_PAD - MLA_NOPE - MLA_ROPE)], axis=-1)
    P["w_uq"] = wq.reshape(depth, MLA_Q_LORA, MLA_HEADS * MLA_PAD).astype(bf16)
    wkv = mla_w_ukv.reshape(depth, MLA_KV_LORA, MLA_HEADS, MLA_NOPE + MLA_V)
    wk = jnp.concatenate([wkv[..., :MLA_NOPE], zeros(depth, MLA_KV_LORA, MLA_HEADS, MLA_PAD - MLA_NOPE)], axis=-1)
    P["w_ukv_k"] = wk.reshape(depth, MLA_KV_LORA, MLA_HEADS * MLA_PAD).astype(bf16)
    P["w_ukv_vt"] = jnp.swapaxes(wkv[..., MLA_NOPE:].reshape(depth, MLA_KV_LORA, MLA_HEADS * MLA_V), 1, 2).astype(bf16)
    P["gq_na"] = row(jnp.tile(na_q_g, (1, NA_HEADS))) * (NA_HEAD_DIM ** -0.5)
    P["gk_na"] = row(jnp.tile(na_k_g, (1, NA_HEADS)))
    P["g_cq"] = row(mla_cq_g)
    P["g_ckv"] = row(mla_ckv_g)
    scale = (MLA_NOPE + MLA_ROPE) ** -0.5 * math.log2(math.e)
    pad = MLA_PAD - MLA_NOPE - MLA_ROPE
    gq = jnp.concatenate([mla_qn_g, mla_qr_g, zeros(depth, pad)], axis=-1) * scale
    P["gq"] = row(jnp.tile(gq, (1, MLA_HEADS)))
    gk = jnp.concatenate([mla_kn_g, zeros(depth, MLA_PAD - MLA_NOPE)], axis=-1)
    P["gk"] = row(jnp.tile(gk, (1, MLA_HEADS)))
    P["gkr"] = row(jnp.concatenate([zeros(depth, MLA_NOPE), mla_kr_g, zeros(depth, pad)], axis=-1))
    invq = np.concatenate([np.full(MLA_NOPE, 1.0 / MLA_NOPE), np.full(MLA_ROPE, 1.0 / MLA_ROPE), np.zeros(pad)])
    invk = np.concatenate([np.full(MLA_NOPE, 1.0 / MLA_NOPE), np.zeros(MLA_PAD - MLA_NOPE)])
    P["invq"] = jnp.asarray(np.tile(invq, MLA_HEADS)[None, :], f32)
    P["invk"] = jnp.asarray(np.tile(invk, MLA_HEADS)[None, :], f32)
    lane = np.arange(NA_WIDTH)
    P["G_na"] = jnp.asarray((lane[:, None] // NA_HEAD_DIM) == (lane[None, :] // NA_HEAD_DIM), bf16)
    lane = np.arange(2 * MLA_PAD)
    grp = np.where(lane % MLA_PAD < MLA_NOPE, 0, np.where(lane % MLA_PAD < MLA_NOPE + MLA_ROPE, 1, 2))
    same = (lane[:, None] // MLA_PAD == lane[None, :] // MLA_PAD) & (grp[:, None] == grp[None, :]) & (grp[:, None] < 2)
    P["G_qm"] = jnp.asarray(same, bf16)
    P["glu_w"] = glu_w.astype(bf16)
    P["glu_b"] = row(glu_b)
    P["w_out"] = w_out.astype(bf16)
    P["w_gate"] = ffn_w_gate.astype(bf16)
    P["w_up"] = ffn_w_up.astype(bf16)
    P["w_down"] = ffn_w_down.astype(bf16)
    return P


def _rope_tables(length):
    inv = 1.0 / (ROPE_BASE ** (jnp.arange(0, MLA_ROPE, 2, dtype=f32) / MLA_ROPE))
    ang = jnp.arange(length, dtype=f32)[:, None] * inv[None, :]
    cos, sin = jnp.cos(ang), jnp.sin(ang)
    half = MLA_ROPE // 2
    z = lambda w: jnp.zeros((length, w), f32)
    pad = MLA_PAD - MLA_NOPE - MLA_ROPE
    cos_t = jnp.concatenate([jnp.ones((length, MLA_NOPE), f32), cos, cos, z(pad)], axis=-1)
    sin_lo = jnp.concatenate([z(MLA_NOPE), -sin, z(half), z(pad)], axis=-1)
    sin_hi = jnp.concatenate([z(MLA_NOPE), z(half), sin, z(pad)], axis=-1)
    return cos_t, sin_lo, sin_hi


def _trunk(x, mods, P, S, bias):
    B, L, _ = x.shape
    rope_tabs = _rope_tables(L)
    nj = L // LANES
    for layer in range(len(mods)):
        mod = mods[layer]
        naq, nak, nav, mq, mk, mvt, ut = _inproj(x, mod, layer, P, rope_tabs)
        yna = _na_attention(naq, nak, nav, bias, layer)
        ymla = _mla_attention(mq, mk, mvt)
        yst = _ssm(ut, layer, S, B, nj)
        x = _mixffn(x, mod, yna, ymla, yst, layer, P)
    return x


def kernel(x_prompt, x_sample, c_prompt, c_sample, ada_w, ada_b, norm1_g, w_in, na_q_g, na_k_g, na_rpb, mla_cq_g,
           mla_ckv_g, mla_w_uq, mla_w_ukv, mla_qn_g, mla_kn_g, mla_qr_g, mla_kr_g, ssm_a_re, ssm_a_im, ssm_b_re,
           ssm_b_im, ssm_c_re, ssm_c_im, ssm_log_dt, ssm_d, glu_w, glu_b, w_out, norm2_g, ffn_w_gate, ffn_w_up,
           ffn_w_down):
    depth = w_in.shape[0]
    nbp, nbs = c_prompt.shape[0], c_sample.shape[0]
    rows = -(-(nbp + nbs) // 8) * 8
    c_all = jnp.concatenate([c_prompt, c_sample, jnp.zeros((rows - nbp - nbs, D_MODEL), f32)], axis=0)
    mod = _modulation(c_all, ada_w, ada_b).reshape(depth, rows, 6, D_MODEL)
    mods_p = [mod[l, :nbp] for l in range(depth)]
    mods_s = [mod[l, nbp:nbp + nbs] for l in range(depth)]

    P = _prep_params(norm1_g, w_in, na_q_g, na_k_g, mla_cq_g, mla_ckv_g, mla_w_uq, mla_w_ukv, mla_qn_g, mla_kn_g,
                     mla_qr_g, mla_kr_g, glu_w, glu_b, w_out, norm2_g, ffn_w_gate, ffn_w_up, ffn_w_down)
    S = _ssm_tables(ssm_a_re, ssm_a_im, ssm_b_re, ssm_b_im, ssm_c_re, ssm_c_im, ssm_log_dt, ssm_d)
    S["toe"] = _toeplitz_gen(S["gm"], S["pf"])
    bias = _na_bias_tables(na_rpb)

    y_prompt = _trunk(x_prompt, mods_p, P, S, bias)
    y_sample = _trunk(x_sample, mods_s, P, S, bias)
    return (y_prompt, y_sample)
```

```python
import functools
import math

import numpy as np
import jax
import jax.numpy as jnp
from jax import lax
from jax.experimental import pallas as pl
from jax.experimental.pallas import tpu as pltpu

f32 = jnp.float32
bf16 = jnp.bfloat16

D_MODEL = 1024
GRID_W = 64
NA_HEADS = 4
NA_HEAD_DIM = 64
NA_WIDTH = NA_HEADS * NA_HEAD_DIM
NA_ROWS = 8
NA_COLS = 16
MLA_HEADS = 8
MLA_NOPE = 64
MLA_ROPE = 32
MLA_V = 64
MLA_Q_LORA = 384
MLA_KV_LORA = 256
ROPE_BASE = 10000.0
SSM_GROUPS = 16
SSM_GC = 16
SSM_STATE = 64
SSM_WIDTH = SSM_GROUPS * SSM_GC
FFN_HIDDEN = 2816
RMS_EPS = 1e-6
NEG_INF = -1e30

LANES = 128
MLA_PAD = 128
NA_QROWS = 4
NA_WROWS = NA_QROWS + NA_ROWS
NA_BIAS_LANE0 = 200
TOKEN_TILE = 512
MLA_CW = 256
MLA_TQ = MLA_CW
MLA_TK = 1024
MLA_KC = 256
MLA_AHEAD = 2
MLA_SLOTS = 2 * MLA_AHEAD
MLA_TRIP = 8
MLA_VROWS = MLA_V + 16
VMEM_LIMIT = 48 * 1024 * 1024

_NT = (((1,), (1,)), ((), ()))
_TN = (((0,), (0,)), ((), ()))


def _cparams(sem):
    return pltpu.CompilerParams(dimension_semantics=sem, vmem_limit_bytes=VMEM_LIMIT)


def _layer_spec(tail, layer):
    n = len(tail)
    return pl.BlockSpec((None,) + tuple(tail), lambda *_: (layer,) + (0,) * n)


def _mod_body(c_ref, w_ref, b_ref, o_ref):
    c = c_ref[...]
    s = c * jax.nn.sigmoid(c)
    o_ref[...] = jnp.dot(s, w_ref[...], precision=lax.Precision.HIGHEST, preferred_element_type=f32) + b_ref[...]


def _modulation(c_all, ada_w, ada_b):
    depth, d, n = ada_w.shape
    tn = 1536
    rows = c_all.shape[0]
    return pl.pallas_call(
        _mod_body,
        grid=(depth, n // tn),
        in_specs=[
            pl.BlockSpec((rows, d), lambda l, j: (0, 0)),
            pl.BlockSpec((None, d, tn), lambda l, j: (l, 0, j)),
            pl.BlockSpec((None, 1, tn), lambda l, j: (l, 0, j)),
        ],
        out_specs=pl.BlockSpec((None, rows, tn), lambda l, j: (l, 0, j)),
        out_shape=jax.ShapeDtypeStruct((depth, rows, n), f32),
        compiler_params=_cparams(("arbitrary", "arbitrary")),
        name="adaln_mod",
    )(c_all, ada_w, ada_b.reshape(depth, 1, n))


def _rope(x, cos, sin_lo, sin_hi):
    return x * cos + pltpu.roll(x, LANES - 16, 1) * sin_lo + pltpu.roll(x, 16, 1) * sin_hi


def _inproj_body(x_ref, mod_ref, g1_ref, wna_ref, wc_ref, wkr_ref, wut_ref, wuq_ref, wukvk_ref, wukvvt_ref,
                 gqna_ref, gkna_ref, gcq_ref, gckv_ref, gq_ref, gk_ref, gkr_ref, invq_ref, invk_ref,
                 gna_ref, gqm_ref, cos_ref, sinlo_ref, sinhi_ref,
                 naq_ref, nak_ref, nav_ref, mqt_ref, mk_ref, mvt_ref, ut_ref):
    x = x_ref[...]
    shift1 = mod_ref[0:1, :]
    scale1 = mod_ref[1:2, :]
    ms = jnp.mean(x * x, axis=-1, keepdims=True)
    h = (x * lax.rsqrt(ms + RMS_EPS) * g1_ref[...]) * (1.0 + scale1) + shift1
    hb = h.astype(bf16)

    z = jnp.dot(hb, wna_ref[...], preferred_element_type=f32)
    q = z[:, :NA_WIDTH]
    k = z[:, NA_WIDTH:2 * NA_WIDTH]
    gna = gna_ref[...]
    ssq = jnp.dot((q * q).astype(bf16), gna, preferred_element_type=f32) * (1.0 / NA_HEAD_DIM)
    ssk = jnp.dot((k * k).astype(bf16), gna, preferred_element_type=f32) * (1.0 / NA_HEAD_DIM)
    naq_ref[...] = (q * lax.rsqrt(ssq + RMS_EPS) * gqna_ref[...]).astype(bf16)
    nak_ref[...] = (k * lax.rsqrt(ssk + RMS_EPS) * gkna_ref[...]).astype(bf16)
    nav_ref[...] = z[:, 2 * NA_WIDTH:].astype(bf16)

    zc = jnp.dot(hb, wc_ref[...], preferred_element_type=f32)
    cq = zc[:, :MLA_Q_LORA]
    ckv = zc[:, MLA_Q_LORA:]
    cqn = (cq * lax.rsqrt(jnp.mean(cq * cq, axis=-1, keepdims=True) + RMS_EPS) * gcq_ref[...]).astype(bf16)
    ckvn = (ckv * lax.rsqrt(jnp.mean(ckv * ckv, axis=-1, keepdims=True) + RMS_EPS) * gckv_ref[...]).astype(bf16)

    cos = cos_ref[...]
    sin_lo = sinlo_ref[...]
    sin_hi = sinhi_ref[...]
    gqm = gqm_ref[...]

    kr = jnp.dot(hb, wkr_ref[...], preferred_element_type=f32)
    kr_ms = jnp.sum(kr * kr, axis=-1, keepdims=True) * (1.0 / MLA_ROPE)
    kr = _rope(kr * lax.rsqrt(kr_ms + RMS_EPS) * gkr_ref[...], cos, sin_lo, sin_hi)

    qraw = jnp.dot(cqn, wuq_ref[...], preferred_element_type=f32)
    kraw = jnp.dot(ckvn, wukvk_ref[...], preferred_element_type=f32)
    for p in range(MLA_HEADS // 2):
        sl = slice(2 * p * MLA_PAD, 2 * (p + 1) * MLA_PAD)
        qs = qraw[:, sl]
        ss = jnp.dot((qs * qs).astype(bf16), gqm, preferred_element_type=f32) * invq_ref[:, sl]
        qn = qs * lax.rsqrt(ss + RMS_EPS) * gq_ref[:, sl]
        ks = kraw[:, sl]
        ss = jnp.dot((ks * ks).astype(bf16), gqm, preferred_element_type=f32) * invk_ref[:, sl]
        kn = ks * lax.rsqrt(ss + RMS_EPS) * gk_ref[:, sl]
        for hh in range(2):
            lo = (2 * p + hh) * MLA_PAD
            piece = qn[:, hh * MLA_PAD:(hh + 1) * MLA_PAD]
            mqt_ref[lo:lo + MLA_PAD, :] = _rope(piece, cos, sin_lo, sin_hi).T.astype(bf16)
            mk_ref[:, lo:lo + MLA_PAD] = (kn[:, hh * MLA_PAD:(hh + 1) * MLA_PAD] + kr).astype(bf16)
    mvt_ref[...] = lax.dot_general(wukvvt_ref[...], ckvn, _NT, preferred_element_type=f32).astype(bf16)

    ut = lax.dot_general(wut_ref[...], hb, _NT, preferred_element_type=f32)
    for c in range(ut_ref.shape[0]):
        ut_ref[c] = ut[:, c * LANES:(c + 1) * LANES].reshape(SSM_GROUPS, SSM_GC, LANES)


def _inproj(x, mod, layer, P, rope_tabs):
    B, L, D = x.shape
    tm = TOKEN_TILE
    nt = L // tm
    cpt = tm // LANES
    tok = lambda w: pl.BlockSpec((None, tm, w), lambda b, i: (b, i, 0))
    pos = pl.BlockSpec((tm, LANES), lambda b, i: (i, 0))
    const2 = lambda a: pl.BlockSpec(a.shape, lambda b, i: (0, 0))
    lw = lambda name: _layer_spec(P[name].shape[1:], layer)
    names = ["g1", "w_na", "w_c", "w_kr", "w_ut", "w_uq", "w_ukv_k", "w_ukv_vt",
             "gq_na", "gk_na", "g_cq", "g_ckv", "gq", "gk", "gkr"]
    consts = ["invq", "invk", "G_na", "G_qm"]
    in_specs = ([tok(D), pl.BlockSpec((None, 6, D), lambda b, i: (b, 0, 0))]
                + [lw(n) for n in names] + [const2(P[n]) for n in consts] + [pos, pos, pos])
    out_shapes = (
        jax.ShapeDtypeStruct((B, L, NA_WIDTH), bf16),
        jax.ShapeDtypeStruct((B, L, NA_WIDTH), bf16),
        jax.ShapeDtypeStruct((B, L, NA_WIDTH), bf16),
        jax.ShapeDtypeStruct((B, MLA_HEADS * MLA_PAD, L), bf16),
        jax.ShapeDtypeStruct((B, L, MLA_HEADS * MLA_PAD), bf16),
        jax.ShapeDtypeStruct((B, MLA_HEADS * MLA_V, L), bf16),
        jax.ShapeDtypeStruct((B * L // LANES, SSM_GROUPS, SSM_GC, LANES), f32),
    )
    chan = lambda w: pl.BlockSpec((None, w, tm), lambda b, i: (b, 0, i))
    out_specs = (tok(NA_WIDTH), tok(NA_WIDTH), tok(NA_WIDTH), chan(MLA_HEADS * MLA_PAD), tok(MLA_HEADS * MLA_PAD),
                 chan(MLA_HEADS * MLA_V),
                 pl.BlockSpec((cpt, SSM_GROUPS, SSM_GC, LANES), lambda b, i: (b * nt + i, 0, 0, 0)))
    return pl.pallas_call(
        _inproj_body,
        grid=(B, nt),
        in_specs=in_specs,
        out_specs=out_specs,
        out_shape=out_shapes,
        compiler_params=_cparams(("parallel", "parallel")),
        name="in_proj",
    )(x, mod, *[P[n] for n in names], *[P[n] for n in consts], *rope_tabs)


def _na_body(q_ref, k_ref, v_ref, b_ref, o_ref, *, rows):
    i = pl.program_id(1)
    ws = jnp.clip(NA_QROWS * i - NA_ROWS // 2, 0, rows - NA_WROWS)
    start = pl.multiple_of(ws * GRID_W, GRID_W)
    kw = k_ref[pl.ds(start, NA_WROWS * GRID_W), :]
    vw = v_ref[pl.ds(start, NA_WROWS * GRID_W), :]
    q = q_ref[...]
    lane = lax.broadcasted_iota(jnp.int32, (1, NA_WIDTH), 1)
    out = jnp.zeros((NA_QROWS * GRID_W, NA_WIDTH), f32)
    for h in range(NA_HEADS):
        hm = (lane >= h * NA_HEAD_DIM) & (lane < (h + 1) * NA_HEAD_DIM)
        qh = jnp.where(hm, q, jnp.zeros_like(q))
        s = lax.dot_general(qh, kw, _NT, preferred_element_type=f32) + b_ref[h]
        m = jnp.max(s, axis=-1, keepdims=True)
        p = jnp.exp(s - m)
        l = jnp.sum(p, axis=-1, keepdims=True)
        o = jnp.dot(p.astype(bf16), vw, preferred_element_type=f32)
        out = jnp.where(hm, o / l, out)
    o_ref[...] = out.astype(bf16)


def _na_attention(q, k, v, bias, layer):
    B, L, W = q.shape
    rows = L // GRID_W
    nblk = rows // NA_QROWS
    tq = NA_QROWS * GRID_W

    def bias_map(b, i):
        variant = jnp.where(i == 0, 0, jnp.where(i == nblk - 1, 2, 1))
        return (layer, variant, 0, 0, 0)

    full = pl.BlockSpec((None, L, W), lambda b, i: (b, 0, 0))
    return pl.pallas_call(
        functools.partial(_na_body, rows=rows),
        grid=(B, nblk),
        in_specs=[pl.BlockSpec((None, tq, W), lambda b, i: (b, i, 0)), full, full,
                  pl.BlockSpec((None, None, NA_HEADS, tq, NA_WROWS * GRID_W), bias_map)],
        out_specs=pl.BlockSpec((None, tq, W), lambda b, i: (b, i, 0)),
        out_shape=jax.ShapeDtypeStruct((B, L, W), bf16),
        compiler_params=_cparams(("parallel", "arbitrary")),
        name="na_attn",
    )(q, k, v, bias)


def _na_bias_body(r_ref, o_ref):
    r = r_ref[...]
    qc = lax.broadcasted_iota(jnp.int32, (GRID_W, LANES), 0)
    lane = lax.broadcasted_iota(jnp.int32, (GRID_W, LANES), 1)
    lo_half = lane < GRID_W
    kc = jnp.where(lo_half, lane, lane - GRID_W)
    cs = jnp.clip(qc - NA_COLS // 2, 0, GRID_W - NA_COLS)
    cvalid = (kc >= cs) & (kc < cs + NA_COLS)
    neg = jnp.full((GRID_W, LANES), NEG_INF, f32)
    tiles = []
    for dr in range(2 * NA_ROWS - 1):
        rowb = jnp.broadcast_to(r[dr:dr + 1, :], (GRID_W, 2 * LANES))
        base = 2 * LANES - (NA_COLS - 1) - NA_BIAS_LANE0
        t_lo = pltpu.roll(rowb, base, 1, stride=1, stride_axis=0)[:, :LANES]
        t_hi = pltpu.roll(rowb, base + GRID_W, 1, stride=1, stride_axis=0)[:, :LANES]
        tiles.append(jnp.where(cvalid, jnp.where(lo_half, t_lo, t_hi), neg))
    for v, (off, lo) in enumerate(((0, 0), (-(NA_ROWS // 2), None), (-NA_ROWS, NA_ROWS // 2))):
        for rl in range(NA_QROWS):
            cols = []
            for kl in range(NA_WROWS):
                d = off + kl - rl
                ok = (-(NA_ROWS // 2) <= d < NA_ROWS // 2) if lo is None else (lo <= kl < lo + NA_ROWS)
                cols.append(tiles[d + NA_ROWS - 1] if ok else neg)
            pairs = [jnp.where(lo_half, cols[2 * j], cols[2 * j + 1]) for j in range(NA_WROWS // 2)]
            o_ref[v, rl * GRID_W:(rl + 1) * GRID_W, :] = jnp.concatenate(pairs, axis=1)


def _na_bias_tables(rpb):
    depth, nh, nr, nc = rpb.shape
    rp = jnp.pad(rpb.astype(f32),
                 ((0, 0), (0, 0), (0, 16 - nr), (NA_BIAS_LANE0, 2 * LANES - NA_BIAS_LANE0 - nc)))
    tq, tkw = NA_QROWS * GRID_W, NA_WROWS * GRID_W
    return pl.pallas_call(
        _na_bias_body,
        grid=(depth, nh),
        in_specs=[pl.BlockSpec((None, None, 16, 2 * LANES), lambda l, h: (l, h, 0, 0))],
        out_specs=pl.BlockSpec((None, 3, None, tq, tkw), lambda l, h: (l, 0, h, 0, 0)),
        out_shape=jax.ShapeDtypeStruct((depth, 3, nh, tq, tkw), f32),
        compiler_params=_cparams(("parallel", "parallel")),
        name="na_bias",
    )(rp)


def _mla_body(qt_ref, k_ref, vt_ref, o_ref, s_scr, *, nq, nk):
    tq, tk = MLA_TQ, MLA_TK
    ones = jnp.ones((MLA_VROWS - MLA_V, MLA_KC), bf16)
    chains = [(h, c) for h in range(2) for c in range(tq // MLA_CW)]
    nch = len(chains)

    def scores(t, slot):
        qoff = pl.multiple_of((t // nk) * tq, tq)
        koff = pl.multiple_of((t % nk) * tk, tk)
        qt = qt_ref[:, pl.ds(qoff, tq)]
        kb = k_ref[pl.ds(koff, tk), :]
        cmax = []
        for n, (h, c) in enumerate(chains):
            s = jnp.dot(kb[:, h * MLA_PAD:(h + 1) * MLA_PAD],
                        qt[h * MLA_PAD:(h + 1) * MLA_PAD, c * MLA_CW:(c + 1) * MLA_CW],
                        preferred_element_type=f32)
            s_scr[slot, n] = s
            cmax.append(jnp.max(s, axis=0, keepdims=True))
        return tuple(cmax)

    nkc = tk // MLA_KC

    def step(t_s, slot_s, t_a, slot_a, cmax, state):
        if t_s is not None:
            qoff_s = pl.multiple_of((t_s // nk) * tq, tq)
            koff_s = pl.multiple_of((t_s % nk) * tk, tk)
            qt = qt_ref[:, pl.ds(qoff_s, tq)]
        j = t_a % nk
        qoff = pl.multiple_of((t_a // nk) * tq, tq)
        koff = pl.multiple_of(j * tk, tk)
        tile_start = j == 0
        mns, als = [], []
        for n in range(nch):
            m = jnp.where(tile_start, -jnp.inf, state[2 * n])
            mns.append(jnp.maximum(m, cmax[n]))
            als.append(jnp.exp2(m - mns[n]))
        parts, cnew = [None] * nch, [None] * nch
        for kc in range(nkc):
            rows = slice(kc * MLA_KC, (kc + 1) * MLA_KC)
            if t_s is not None:
                kb = k_ref[pl.ds(koff_s + kc * MLA_KC, MLA_KC), :]
                for n, (h, c) in enumerate(chains):
                    s = jnp.dot(kb[:, h * MLA_PAD:(h + 1) * MLA_PAD],
                                qt[h * MLA_PAD:(h + 1) * MLA_PAD, c * MLA_CW:(c + 1) * MLA_CW],
                                preferred_element_type=f32)
                    s_scr[slot_s, n, rows, :] = s
                    cm = jnp.max(s, axis=0, keepdims=True)
                    cnew[n] = cm if cnew[n] is None else jnp.maximum(cnew[n], cm)
            vt = vt_ref[:, pl.ds(koff + kc * MLA_KC, MLA_KC)]
            for n, (h, c) in enumerate(chains):
                p = jnp.exp2(s_scr[slot_a, n, rows, :] - mns[n]).astype(bf16)
                vte = jnp.concatenate([vt[h * MLA_V:(h + 1) * MLA_V], ones], axis=0)
                d = jnp.dot(vte, p, preferred_element_type=f32)
                parts[n] = d if parts[n] is None else parts[n] + d
        new = []
        for n, (h, c) in enumerate(chains):
            acc = als[n] * state[2 * n + 1] + parts[n]
            o_ref[h * MLA_V:(h + 1) * MLA_V, pl.ds(qoff + c * MLA_CW, MLA_CW)] = (
                acc[:MLA_V] / acc[MLA_V:MLA_V + 1]).astype(bf16)
            new += [mns[n], acc]
        return (tuple(cnew) if t_s is not None else ()), tuple(new)

    nt = nq * nk
    ahead, ns = MLA_AHEAD, MLA_SLOTS

    def trip(t0, carry, last):
        pend, state = list(carry[:nch * ahead]), carry[nch * ahead:]
        for u in range(MLA_TRIP):
            t_s = None if (last and u + ahead >= MLA_TRIP) else t0 + u + ahead
            cn, state = step(t_s, (u + ahead) % ns, t0 + u, u % ns, tuple(pend[:nch]), state)
            pend = pend[nch:] + list(cn)
        return tuple(pend) + state

    m0 = jnp.full((1, MLA_CW), -jnp.inf, f32)
    a0 = jnp.zeros((MLA_VROWS, MLA_CW), f32)
    carry = ()
    for t in range(ahead):
        carry += scores(t, t)
    carry = lax.fori_loop(0, nt // MLA_TRIP - 1, lambda i, c: trip(i * MLA_TRIP, c, False), carry + (m0, a0) * nch)
    trip(nt - MLA_TRIP, carry, True)


def _mla_attention(mqt, mk, mvt):
    B, L, _ = mk.shape
    npair = MLA_HEADS // 2
    nq, nk = L // MLA_TQ, L // MLA_TK
    assert (nq * nk) % MLA_TRIP == 0 and MLA_TRIP % MLA_SLOTS == 0
    mode = dict(pipeline_mode=pl.Buffered(1)) if L * 2 * MLA_PAD * 2 > 4 * 1024 * 1024 else {}
    return pl.pallas_call(
        functools.partial(_mla_body, nq=nq, nk=nk),
        grid=(B, npair),
        in_specs=[pl.BlockSpec((None, 2 * MLA_PAD, L), lambda b, p: (b, p, 0), **mode),
                  pl.BlockSpec((None, L, 2 * MLA_PAD), lambda b, p: (b, 0, p), **mode),
                  pl.BlockSpec((None, 2 * MLA_V, L), lambda b, p: (b, p, 0), **mode)],
        out_specs=pl.BlockSpec((None, 2 * MLA_V, L), lambda b, p: (b, p, 0)),
        out_shape=jax.ShapeDtypeStruct((B, MLA_HEADS * MLA_V, L), bf16),
        scratch_shapes=[pltpu.VMEM((MLA_SLOTS, 2 * MLA_TQ // MLA_CW, MLA_TK, MLA_CW), f32)],
        compiler_params=_cparams(("parallel", "parallel")),
        name="mla_attn",
    )(mqt, mk, mvt)


def _toeplitz_body(g_ref, pf_ref, o_ref):
    kern = jnp.dot(g_ref[...], pf_ref[...], precision=lax.Precision.HIGHEST, preferred_element_type=f32)
    bits = lax.bitcast_convert_type(kern.astype(bf16).astype(f32), jnp.int32)
    for cp in range(SSM_GC):
        for c in range(0, SSM_GC, 2):
            r = cp * SSM_GC + c
            packed = lax.shift_right_logical(bits[r:r + 1, :], 16) | bits[r + 1:r + 2, :]
            rowb = jnp.broadcast_to(packed, (LANES, 2 * LANES))
            toe = pltpu.roll(rowb, LANES + 1, 1, stride=1, stride_axis=0)[:, :LANES]
            even = lax.bitcast_convert_type(lax.shift_left(toe, 16), f32)
            odd = lax.bitcast_convert_type(toe & jnp.int32(-65536), f32)
            o_ref[cp * LANES:(cp + 1) * LANES, c * LANES:(c + 1) * LANES] = even.astype(bf16)
            o_ref[cp * LANES:(cp + 1) * LANES, (c + 1) * LANES:(c + 2) * LANES] = odd.astype(bf16)


def _toeplitz_gen(gm, pf):
    dg = pf.shape[0]
    n = SSM_GC * LANES
    k = 4 * SSM_STATE
    return pl.pallas_call(
        _toeplitz_body,
        grid=(dg,),
        in_specs=[pl.BlockSpec((None, SSM_GC * SSM_GC, k), lambda g: (g, 0, 0)),
                  pl.BlockSpec((None, k, 2 * LANES), lambda g: (g, 0, 0))],
        out_specs=pl.BlockSpec((None, n, n), lambda g: (g, 0, 0)),
        out_shape=jax.ShapeDtypeStruct((dg, n, n), bf16),
        compiler_params=_cparams(("parallel",)),
        name="s5_toeplitz",
    )(gm, pf)


def _ssm_body(u_ref, t_ref, wv_ref, wc_ref, a_ref, d_ref, o_ref,
              vre, vim, xfre, xfim, xbre, xbim, *, nb, nj):
    m = nb * nj
    u32 = jnp.concatenate([u_ref[:, c, :] for c in range(SSM_GC)], axis=1)
    ub = u32.astype(bf16)
    y = jnp.dot(ub, t_ref[...], preferred_element_type=f32)
    v = jnp.dot(ub, wv_ref[...], preferred_element_type=f32)
    vre[...] = v[:, :LANES]
    vim[...] = v[:, LANES:]
    are = a_ref[0:1, :]
    aim = a_ref[1:2, :]
    is_fwd = lax.broadcasted_iota(jnp.int32, (nb, LANES), 1) < SSM_STATE

    def rows(j):
        return pl.ds(j, nb, stride=nj) if nb > 1 else pl.ds(j, 1)

    def step(k, carry):
        xr, xi = carry
        rf, rb = rows(k), rows(nj - 1 - k)
        xfre[rf, :] = xr
        xfim[rf, :] = xi
        xbre[rb, :] = xr
        xbim[rb, :] = xi
        vr = jnp.where(is_fwd, vre[rf, :], vre[rb, :])
        vi = jnp.where(is_fwd, vim[rf, :], vim[rb, :])
        return are * xr - aim * xi + vr, are * xi + aim * xr + vi

    z = jnp.zeros((nb, LANES), f32)
    lax.fori_loop(0, nj, step, (z, z))
    fwd_m = lax.broadcasted_iota(jnp.int32, (m, LANES), 1) < SSM_STATE
    xin = jnp.concatenate([jnp.where(fwd_m, xfre[...], xbre[...]),
                           jnp.where(fwd_m, xfim[...], xbim[...])], axis=1).astype(bf16)
    y = y + jnp.dot(xin, wc_ref[...], preferred_element_type=f32) + u32 * d_ref[...]
    for c in range(SSM_GC):
        o_ref[:, c, :] = y[:, c * LANES:(c + 1) * LANES]


def _ssm(ut, layer, S, nb, nj):
    m = nb * nj
    n = SSM_GC * LANES
    gspec = lambda tail: pl.BlockSpec((None,) + tail, lambda g: (layer * SSM_GROUPS + g,) + (0,) * len(tail))
    io = pl.BlockSpec((m, None, SSM_GC, LANES), lambda g: (0, g, 0, 0))
    out = pl.pallas_call(
        functools.partial(_ssm_body, nb=nb, nj=nj),
        grid=(SSM_GROUPS,),
        in_specs=[io, gspec((n, n)), gspec((n, 4 * SSM_STATE)), gspec((4 * SSM_STATE, n)),
                  gspec((2, LANES)), gspec((1, n))],
        out_specs=io,
        out_shape=jax.ShapeDtypeStruct((m, SSM_GROUPS, SSM_GC, LANES), f32),
        scratch_shapes=[pltpu.VMEM((m, LANES), f32)] * 6,
        compiler_params=_cparams(("parallel",)),
        name="s5_conv",
    )(ut, S["toe"], S["wv"], S["wc"], S["a"], S["d"])
    return out


def _ssm_tables(a_re, a_im, b_re, b_im, c_re, c_im, log_dt, ssm_d):
    depth = a_re.shape[0]
    dg = depth * SSM_GROUPS
    lam = lax.complex(a_re.astype(f32), a_im.astype(f32))
    dt = jnp.exp(log_dt.astype(f32))[..., None]
    lam_dt = lam * dt
    lam_bar = jnp.exp(lam_dt)
    b_bar = ((lam_bar - 1.0) / lam)[..., None] * lax.complex(b_re.astype(f32), b_im.astype(f32))
    cmat = lax.complex(c_re.astype(f32), c_im.astype(f32))
    steps = jnp.arange(LANES + 1, dtype=f32)
    pw = jnp.exp(lam_dt[..., None] * steps)
    pwf, pwb = pw[:, 0], pw[:, 1]

    e = np.arange(2 * LANES)
    f_ok = (e >= LANES - 1) & (e <= 2 * LANES - 2)
    b_ok = e <= LANES - 1
    pf_f = jnp.where(f_ok, pwf[..., np.clip(e - (LANES - 1), 0, LANES - 1)], 0.0)
    pf_b = jnp.where(b_ok, pwb[..., np.clip(LANES - 1 - e, 0, LANES - 1)], 0.0)
    pf = jnp.stack([pf_f.real, pf_f.imag, pf_b.real, pf_b.imag], axis=2)
    pf = pf.reshape(dg, 4 * SSM_STATE, 2 * LANES)

    bbf, bbb = b_bar[:, 0], b_bar[:, 1]
    cf, cb = cmat[:, 0], cmat[:, 1]
    gf = cf[:, :, None, :, :] * jnp.swapaxes(bbf, -1, -2)[:, :, :, None, :]
    gb = cb[:, :, None, :, :] * jnp.swapaxes(bbb, -1, -2)[:, :, :, None, :]
    gm = jnp.concatenate([gf.real, -gf.imag, gb.real, -gb.imag], axis=-1)
    gm = gm.reshape(dg, SSM_GC * SSM_GC, 4 * SSM_STATE)

    s_idx = np.arange(LANES)
    pvf = jnp.swapaxes(pwf[..., LANES - 1 - s_idx], -1, -2)
    pvb = jnp.swapaxes(pwb[..., s_idx], -1, -2)
    pv1 = jnp.concatenate([pvf.real, pvb.real, pvf.imag, pvb.imag], axis=-1).reshape(dg, LANES, 4 * SSM_STATE)
    pv2 = jnp.concatenate([pvf.imag, pvb.imag, pvf.real, pvb.real], axis=-1).reshape(dg, LANES, 4 * SSM_STATE)
    bft, bbt = jnp.swapaxes(bbf, -1, -2), jnp.swapaxes(bbb, -1, -2)
    bv = jnp.stack([jnp.concatenate([bft.real, bbt.real, bft.real, bbt.real], axis=-1),
                    jnp.concatenate([-bft.imag, -bbt.imag, bft.imag, bbt.imag], axis=-1)], axis=3)
    bv = bv.reshape(dg, SSM_GC, 2, 4 * SSM_STATE)
    t_idx = np.arange(LANES)
    pcf, pcb = pwf[..., t_idx + 1], pwb[..., LANES - t_idx]
    pc = jnp.stack([pcf.real, pcf.imag, pcb.real, pcb.imag], axis=2).reshape(dg, 4, SSM_STATE, LANES)
    cft, cbt = jnp.swapaxes(cf, -1, -2), jnp.swapaxes(cb, -1, -2)
    ct = jnp.stack([cft.real, cft.imag, cbt.real, cbt.imag], axis=2).reshape(dg, 4, SSM_STATE, SSM_GC)

    a128 = jnp.concatenate([pwf[..., LANES], pwb[..., LANES]], axis=-1)
    a = jnp.stack([a128.real, a128.imag], axis=2).reshape(dg, 2, LANES)
    d = jnp.repeat(ssm_d.astype(f32).reshape(dg, SSM_GC), LANES, axis=-1).reshape(dg, 1, SSM_GC * LANES)
    return dict(pf=pf, gm=gm, pv1=pv1, pv2=pv2, bv=bv, pc=pc, ct=ct, a=a, d=d)


def _state_tables_body(pv1_ref, pv2_ref, bv_ref, pc_ref, ct_ref, wv_ref, wc_ref):
    pv1, pv2 = pv1_ref[...], pv2_ref[...]
    for cp in range(SSM_GC):
        wv_ref[cp * LANES:(cp + 1) * LANES, :] = (pv1 * bv_ref[cp, 0:1, :] + pv2 * bv_ref[cp, 1:2, :]).astype(bf16)
    pfr, pfi, pbr, pbi = pc_ref[0], pc_ref[1], pc_ref[2], pc_ref[3]
    cfr, cfi, cbr, cbi = ct_ref[0], ct_ref[1], ct_ref[2], ct_ref[3]
    for c in range(SSM_GC):
        col = slice(c, c + 1)
        tile = jnp.concatenate([cfr[:, col] * pfr - cfi[:, col] * pfi,
                                cbr[:, col] * pbr - cbi[:, col] * pbi,
                                -(cfr[:, col] * pfi + cfi[:, col] * pfr),
                                -(cbr[:, col] * pbi + cbi[:, col] * pbr)], axis=0)
        wc_ref[:, c * LANES:(c + 1) * LANES] = tile.astype(bf16)


def _state_tables(S):
    dg = S["pf"].shape[0]
    n, k = SSM_GC * LANES, 4 * SSM_STATE
    g3 = lambda a, b: pl.BlockSpec((None, a, b), lambda g: (g, 0, 0))
    g4 = lambda a, b, c: pl.BlockSpec((None, a, b, c), lambda g: (g, 0, 0, 0))
    return pl.pallas_call(
        _state_tables_body,
        grid=(dg,),
        in_specs=[g3(LANES, k), g3(LANES, k), g4(SSM_GC, 2, k), g4(4, SSM_STATE, LANES), g4(4, SSM_STATE, SSM_GC)],
        out_specs=(g3(n, k), g3(k, n)),
        out_shape=(jax.ShapeDtypeStruct((dg, n, k), bf16), jax.ShapeDtypeStruct((dg, k, n), bf16)),
        compiler_params=_cparams(("parallel",)),
        name="s5_state_tables",
    )(S["pv1"], S["pv2"], S["bv"], S["pc"], S["ct"])


def _mixffn_body(x_ref, mod_ref, yna_ref, ymlat_ref, yst_ref, gluw_ref, glub_ref, wo_ref, g2_ref,
                 wg_ref, wu_ref, wd_ref, o_ref, act_ref):
    ys = jnp.concatenate([yst_ref[c].reshape(SSM_WIDTH, LANES).T for c in range(yst_ref.shape[0])],
                         axis=0)
    ys = 0.5 * ys * (1.0 + jnp.tanh(math.sqrt(2.0 / math.pi) * (ys + 0.044715 * (ys * ys * ys))))
    gate = jax.nn.sigmoid(jnp.dot(ys.astype(bf16), gluw_ref[...], preferred_element_type=f32) + glub_ref[...])
    yssm = (ys * gate).astype(bf16)
    o1 = NA_WIDTH
    o2 = NA_WIDTH + MLA_HEADS * MLA_V
    mix = (jnp.dot(yna_ref[...], wo_ref[0:o1, :], preferred_element_type=f32)
           + lax.dot_general(ymlat_ref[...], wo_ref[o1:o2, :], _TN, preferred_element_type=f32)
           + jnp.dot(yssm, wo_ref[o2:, :], preferred_element_type=f32))
    x1 = x_ref[...] + mod_ref[2:3, :] * mix
    o_ref[...] = x1
    ms = jnp.mean(x1 * x1, axis=-1, keepdims=True)
    h2 = ((x1 * lax.rsqrt(ms + RMS_EPS) * g2_ref[...]) * (1.0 + mod_ref[4:5, :]) + mod_ref[3:4, :]).astype(bf16)

    th = 256
    for c in range(FFN_HIDDEN // th):
        cs = slice(c * th, (c + 1) * th)
        g = jnp.dot(h2, wg_ref[:, cs], preferred_element_type=f32)
        u = jnp.dot(h2, wu_ref[:, cs], preferred_element_type=f32)
        act_ref[:, cs] = (g * jax.nn.sigmoid(g) * u).astype(bf16)
    ffn = jnp.dot(act_ref[...], wd_ref[...], preferred_element_type=f32)
    o_ref[...] = o_ref[...] + mod_ref[5:6, :] * ffn


def _mixffn(x, mod, yna, ymlat, yst, layer, P):
    B, L, D = x.shape
    tm = TOKEN_TILE
    nt = L // tm
    cpt = tm // LANES
    tok = lambda w: pl.BlockSpec((None, tm, w), lambda b, i: (b, i, 0))

    def wspec(name):
        tail = P[name].shape[1:]
        return pl.BlockSpec((None,) + tail, lambda b, i: (layer,) + (0,) * len(tail), pipeline_mode=pl.Buffered(1))

    return pl.pallas_call(
        _mixffn_body,
        grid=(B, nt),
        in_specs=[tok(D), pl.BlockSpec((None, 6, D), lambda b, i: (b, 0, 0)), tok(NA_WIDTH),
                  pl.BlockSpec((None, MLA_HEADS * MLA_V, tm), lambda b, i: (b, 0, i)),
                  pl.BlockSpec((cpt, SSM_GROUPS, SSM_GC, LANES), lambda b, i: (b * nt + i, 0, 0, 0)),
                  wspec("glu_w"), wspec("glu_b"), wspec("w_out"), wspec("g2"),
                  wspec("w_gate"), wspec("w_up"), wspec("w_down")],
        out_specs=tok(D),
        out_shape=jax.ShapeDtypeStruct((B, L, D), f32),
        scratch_shapes=[pltpu.VMEM((tm, FFN_HIDDEN), bf16)],
        compiler_params=_cparams(("parallel", "parallel")),
        name="mix_ffn",
    )(x, mod, yna, ymlat, yst, P["glu_w"], P["glu_b"], P["w_out"], P["g2"], P["w_gate"], P["w_up"], P["w_down"])


def _prep_params(norm1_g, w_in, na_q_g, na_k_g, mla_cq_g, mla_ckv_g, mla_w_uq, mla_w_ukv, mla_qn_g, mla_kn_g,
                 mla_qr_g, mla_kr_g, glu_w, glu_b, w_out, norm2_g, ffn_w_gate, ffn_w_up, ffn_w_down):
    depth = w_in.shape[0]
    o1 = 3 * NA_WIDTH
    o2 = o1 + MLA_Q_LORA
    o3 = o2 + MLA_KV_LORA
    o4 = o3 + MLA_ROPE
    row = lambda a: a.astype(f32)[:, None, :]
    zeros = lambda *s: jnp.zeros(s, f32)
    P = {}
    P["g1"] = row(norm1_g)
    P["g2"] = row(norm2_g)
    P["w_na"] = w_in[:, :, :o1].astype(bf16)
    P["w_c"] = w_in[:, :, o1:o3].astype(bf16)
    P["w_kr"] = jnp.concatenate([zeros(depth, D_MODEL, MLA_NOPE), w_in[:, :, o3:o4],
                                 zeros(depth, D_MODEL, MLA_PAD - MLA_NOPE - MLA_ROPE)], axis=-1).astype(bf16)
    P["w_ut"] = jnp.swapaxes(w_in[:, :, o4:], 1, 2).astype(bf16)
    wq = mla_w_uq.reshape(depth, MLA_Q_LORA, MLA_HEADS, MLA_NOPE + MLA_ROPE)
    wq = jnp.concatenate([wq, zeros(depth, MLA_Q_LORA, MLA_HEADS, MLA_PAD - MLA_NOPE - MLA_ROPE)], axis=-1)
    P["w_uq"] = wq.reshape(depth, MLA_Q_LORA, MLA_HEADS * MLA_PAD).astype(bf16)
    wkv = mla_w_ukv.reshape(depth, MLA_KV_LORA, MLA_HEADS, MLA_NOPE + MLA_V)
    wk = jnp.concatenate([wkv[..., :MLA_NOPE], zeros(depth, MLA_KV_LORA, MLA_HEADS, MLA_PAD - MLA_NOPE)], axis=-1)
    P["w_ukv_k"] = wk.reshape(depth, MLA_KV_LORA, MLA_HEADS * MLA_PAD).astype(bf16)
    P["w_ukv_vt"] = jnp.swapaxes(wkv[..., MLA_NOPE:].reshape(depth, MLA_KV_LORA, MLA_HEADS * MLA_V), 1, 2).astype(bf16)
    P["gq_na"] = row(jnp.tile(na_q_g, (1, NA_HEADS))) * (NA_HEAD_DIM ** -0.5)
    P["gk_na"] = row(jnp.tile(na_k_g, (1, NA_HEADS)))
    P["g_cq"] = row(mla_cq_g)
    P["g_ckv"] = row(mla_ckv_g)
    scale = (MLA_NOPE + MLA_ROPE) ** -0.5 * math.log2(math.e)
    pad = MLA_PAD - MLA_NOPE - MLA_ROPE
    gq = jnp.concatenate([mla_qn_g, mla_qr_g, zeros(depth, pad)], axis=-1) * scale
    P["gq"] = row(jnp.tile(gq, (1, MLA_HEADS)))
    gk = jnp.concatenate([mla_kn_g, zeros(depth, MLA_PAD - MLA_NOPE)], axis=-1)
    P["gk"] = row(jnp.tile(gk, (1, MLA_HEADS)))
    P["gkr"] = row(jnp.concatenate([zeros(depth, MLA_NOPE), mla_kr_g, zeros(depth, pad)], axis=-1))
    invq = np.concatenate([np.full(MLA_NOPE, 1.0 / MLA_NOPE), np.full(MLA_ROPE, 1.0 / MLA_ROPE), np.zeros(pad)])
    invk = np.concatenate([np.full(MLA_NOPE, 1.0 / MLA_NOPE), np.zeros(MLA_PAD - MLA_NOPE)])
    P["invq"] = jnp.asarray(np.tile(invq, MLA_HEADS)[None, :], f32)
    P["invk"] = jnp.asarray(np.tile(invk, MLA_HEADS)[None, :], f32)
    lane = np.arange(NA_WIDTH)
    P["G_na"] = jnp.asarray((lane[:, None] // NA_HEAD_DIM) == (lane[None, :] // NA_HEAD_DIM), bf16)
    lane = np.arange(2 * MLA_PAD)
    grp = np.where(lane % MLA_PAD < MLA_NOPE, 0, np.where(lane % MLA_PAD < MLA_NOPE + MLA_ROPE, 1, 2))
    same = (lane[:, None] // MLA_PAD == lane[None, :] // MLA_PAD) & (grp[:, None] == grp[None, :]) & (grp[:, None] < 2)
    P["G_qm"] = jnp.asarray(same, bf16)
    P["glu_w"] = glu_w.astype(bf16)
    P["glu_b"] = row(glu_b)
    P["w_out"] = w_out.astype(bf16)
    P["w_gate"] = ffn_w_gate.astype(bf16)
    P["w_up"] = ffn_w_up.astype(bf16)
    P["w_down"] = ffn_w_down.astype(bf16)
    return P


def _rope_tables(length):
    inv = 1.0 / (ROPE_BASE ** (jnp.arange(0, MLA_ROPE, 2, dtype=f32) / MLA_ROPE))
    ang = jnp.arange(length, dtype=f32)[:, None] * inv[None, :]
    cos, sin = jnp.cos(ang), jnp.sin(ang)
    half = MLA_ROPE // 2
    z = lambda w: jnp.zeros((length, w), f32)
    pad = MLA_PAD - MLA_NOPE - MLA_ROPE
    cos_t = jnp.concatenate([jnp.ones((length, MLA_NOPE), f32), cos, cos, z(pad)], axis=-1)
    sin_lo = jnp.concatenate([z(MLA_NOPE), -sin, z(half), z(pad)], axis=-1)
    sin_hi = jnp.concatenate([z(MLA_NOPE), z(half), sin, z(pad)], axis=-1)
    return cos_t, sin_lo, sin_hi


def _trunk(x, mods, P, S, bias):
    B, L, _ = x.shape
    rope_tabs = _rope_tables(L)
    nj = L // LANES
    for layer in range(len(mods)):
        mod = mods[layer]
        naq, nak, nav, mq, mk, mvt, ut = _inproj(x, mod, layer, P, rope_tabs)
        yna = _na_attention(naq, nak, nav, bias, layer)
        ymla = _mla_attention(mq, mk, mvt)
        yst = _ssm(ut, layer, S, B, nj)
        x = _mixffn(x, mod, yna, ymla, yst, layer, P)
    return x


def kernel(x_prompt, x_sample, c_prompt, c_sample, ada_w, ada_b, norm1_g, w_in, na_q_g, na_k_g, na_rpb, mla_cq_g,
           mla_ckv_g, mla_w_uq, mla_w_ukv, mla_qn_g, mla_kn_g, mla_qr_g, mla_kr_g, ssm_a_re, ssm_a_im, ssm_b_re,
           ssm_b_im, ssm_c_re, ssm_c_im, ssm_log_dt, ssm_d, glu_w, glu_b, w_out, norm2_g, ffn_w_gate, ffn_w_up,
           ffn_w_down):
    depth = w_in.shape[0]
    nbp, nbs = c_prompt.shape[0], c_sample.shape[0]
    rows = -(-(nbp + nbs) // 8) * 8
    c_all = jnp.concatenate([c_prompt, c_sample, jnp.zeros((rows - nbp - nbs, D_MODEL), f32)], axis=0)
    mod = _modulation(c_all, ada_w, ada_b).reshape(depth, rows, 6, D_MODEL)
    mods_p = [mod[l, :nbp] for l in range(depth)]
    mods_s = [mod[l, nbp:nbp + nbs] for l in range(depth)]

    P = _prep_params(norm1_g, w_in, na_q_g, na_k_g, mla_cq_g, mla_ckv_g, mla_w_uq, mla_w_ukv, mla_qn_g, mla_kn_g,
                     mla_qr_g, mla_kr_g, glu_w, glu_b, w_out, norm2_g, ffn_w_gate, ffn_w_up, ffn_w_down)
    S = _ssm_tables(ssm_a_re, ssm_a_im, ssm_b_re, ssm_b_im, ssm_c_re, ssm_c_im, ssm_log_dt, ssm_d)
    S["toe"] = _toeplitz_gen(S["gm"], S["pf"])
    S["wv"], S["wc"] = _state_tables(S)
    bias = _na_bias_tables(na_rpb)

    y_prompt = _trunk(x_prompt, mods_p, P, S, bias)
    y_sample = _trunk(x_sample, mods_s, P, S, bias)
    return (y_prompt, y_sample)
```

```python
import functools
import math

import numpy as np
import jax
import jax.numpy as jnp
from jax import lax
from jax.experimental import pallas as pl
from jax.experimental.pallas import tpu as pltpu

f32 = jnp.float32
bf16 = jnp.bfloat16

D_MODEL = 1024
GRID_W = 64
NA_HEADS = 4
NA_HEAD_DIM = 64
NA_WIDTH = NA_HEADS * NA_HEAD_DIM
NA_ROWS = 8
NA_COLS = 16
MLA_HEADS = 8
MLA_NOPE = 64
MLA_ROPE = 32
MLA_V = 64
MLA_Q_LORA = 384
MLA_KV_LORA = 256
ROPE_BASE = 10000.0
SSM_GROUPS = 16
SSM_GC = 16
SSM_STATE = 64
SSM_WIDTH = SSM_GROUPS * SSM_GC
FFN_HIDDEN = 2816
RMS_EPS = 1e-6
NEG_INF = -1e30

LANES = 128
MLA_PAD = 128
NA_QCOLS = 16
NA_KCOLS = 2 * NA_COLS
NA_QROWS = 4
NA_WROWS = NA_QROWS + NA_ROWS
NA_BIAS_LANE0 = 200
TOKEN_TILE = 512
MLA_CW = 256
MLA_TQ = MLA_CW
MLA_TK = 1024
MLA_KC = 256
MLA_AHEAD = 2
MLA_SLOTS = 2 * MLA_AHEAD
MLA_TRIP = 8
MLA_VROWS = MLA_V + 16
VMEM_LIMIT = 48 * 1024 * 1024

_NT = (((1,), (1,)), ((), ()))
_TN = (((0,), (0,)), ((), ()))


def _cparams(sem):
    return pltpu.CompilerParams(dimension_semantics=sem, vmem_limit_bytes=VMEM_LIMIT)


def _layer_spec(tail, layer):
    n = len(tail)
    return pl.BlockSpec((None,) + tuple(tail), lambda *_: (layer,) + (0,) * n)


def _mod_body(c_ref, w_ref, b_ref, o_ref):
    c = c_ref[...]
    s = c * jax.nn.sigmoid(c)
    o_ref[...] = jnp.dot(s, w_ref[...], precision=lax.Precision.HIGHEST, preferred_element_type=f32) + b_ref[...]


def _modulation(c_all, ada_w, ada_b):
    depth, d, n = ada_w.shape
    tn = 1536
    rows = c_all.shape[0]
    return pl.pallas_call(
        _mod_body,
        grid=(depth, n // tn),
        in_specs=[
            pl.BlockSpec((rows, d), lambda l, j: (0, 0)),
            pl.BlockSpec((None, d, tn), lambda l, j: (l, 0, j)),
            pl.BlockSpec((None, 1, tn), lambda l, j: (l, 0, j)),
        ],
        out_specs=pl.BlockSpec((None, rows, tn), lambda l, j: (l, 0, j)),
        out_shape=jax.ShapeDtypeStruct((depth, rows, n), f32),
        compiler_params=_cparams(("arbitrary", "arbitrary")),
        name="adaln_mod",
    )(c_all, ada_w, ada_b.reshape(depth, 1, n))


def _rope(x, cos, sin_lo, sin_hi):
    return x * cos + pltpu.roll(x, LANES - 16, 1) * sin_lo + pltpu.roll(x, 16, 1) * sin_hi


def _inproj_body(x_ref, mod_ref, g1_ref, wna_ref, wc_ref, wkr_ref, wut_ref, wuq_ref, wukvk_ref, wukvvt_ref,
                 gqna_ref, gkna_ref, gcq_ref, gckv_ref, gq_ref, gk_ref, gkr_ref, invq_ref, invk_ref,
                 gna_ref, gqm_ref, cos_ref, sinlo_ref, sinhi_ref,
                 naq_ref, nak_ref, nav_ref, mqt_ref, mk_ref, mvt_ref, ut_ref):
    x = x_ref[...]
    shift1 = mod_ref[0:1, :]
    scale1 = mod_ref[1:2, :]
    ms = jnp.mean(x * x, axis=-1, keepdims=True)
    h = (x * lax.rsqrt(ms + RMS_EPS) * g1_ref[...]) * (1.0 + scale1) + shift1
    hb = h.astype(bf16)

    z = jnp.dot(hb, wna_ref[...], preferred_element_type=f32)
    q = z[:, :NA_WIDTH]
    k = z[:, NA_WIDTH:2 * NA_WIDTH]
    gna = gna_ref[...]
    ssq = jnp.dot((q * q).astype(bf16), gna, preferred_element_type=f32) * (1.0 / NA_HEAD_DIM)
    ssk = jnp.dot((k * k).astype(bf16), gna, preferred_element_type=f32) * (1.0 / NA_HEAD_DIM)
    naq_ref[...] = (q * lax.rsqrt(ssq + RMS_EPS) * gqna_ref[...]).astype(bf16)
    nak_ref[...] = (k * lax.rsqrt(ssk + RMS_EPS) * gkna_ref[...]).astype(bf16)
    nav_ref[...] = z[:, 2 * NA_WIDTH:].astype(bf16)

    zc = jnp.dot(hb, wc_ref[...], preferred_element_type=f32)
    cq = zc[:, :MLA_Q_LORA]
    ckv = zc[:, MLA_Q_LORA:]
    cqn = (cq * lax.rsqrt(jnp.mean(cq * cq, axis=-1, keepdims=True) + RMS_EPS) * gcq_ref[...]).astype(bf16)
    ckvn = (ckv * lax.rsqrt(jnp.mean(ckv * ckv, axis=-1, keepdims=True) + RMS_EPS) * gckv_ref[...]).astype(bf16)

    cos = cos_ref[...]
    sin_lo = sinlo_ref[...]
    sin_hi = sinhi_ref[...]
    gqm = gqm_ref[...]

    kr = jnp.dot(hb, wkr_ref[...], preferred_element_type=f32)
    kr_ms = jnp.sum(kr * kr, axis=-1, keepdims=True) * (1.0 / MLA_ROPE)
    kr = _rope(kr * lax.rsqrt(kr_ms + RMS_EPS) * gkr_ref[...], cos, sin_lo, sin_hi)

    qraw = jnp.dot(cqn, wuq_ref[...], preferred_element_type=f32)
    kraw = jnp.dot(ckvn, wukvk_ref[...], preferred_element_type=f32)
    for p in range(MLA_HEADS // 2):
        sl = slice(2 * p * MLA_PAD, 2 * (p + 1) * MLA_PAD)
        qs = qraw[:, sl]
        ss = jnp.dot((qs * qs).astype(bf16), gqm, preferred_element_type=f32) * invq_ref[:, sl]
        qn = qs * lax.rsqrt(ss + RMS_EPS) * gq_ref[:, sl]
        ks = kraw[:, sl]
        ss = jnp.dot((ks * ks).astype(bf16), gqm, preferred_element_type=f32) * invk_ref[:, sl]
        kn = ks * lax.rsqrt(ss + RMS_EPS) * gk_ref[:, sl]
        for hh in range(2):
            lo = (2 * p + hh) * MLA_PAD
            piece = qn[:, hh * MLA_PAD:(hh + 1) * MLA_PAD]
            mqt_ref[lo:lo + MLA_PAD, :] = _rope(piece, cos, sin_lo, sin_hi).T.astype(bf16)
            mk_ref[:, lo:lo + MLA_PAD] = (kn[:, hh * MLA_PAD:(hh + 1) * MLA_PAD] + kr).astype(bf16)
    mvt_ref[...] = lax.dot_general(wukvvt_ref[...], ckvn, _NT, preferred_element_type=f32).astype(bf16)

    ut = lax.dot_general(wut_ref[...], hb, _NT, preferred_element_type=f32)
    for c in range(ut_ref.shape[0]):
        ut_ref[c] = ut[:, c * LANES:(c + 1) * LANES].reshape(SSM_GROUPS, SSM_GC, LANES)


def _inproj(x, mod, layer, P, rope_tabs):
    B, L, D = x.shape
    tm = TOKEN_TILE
    nt = L // tm
    cpt = tm // LANES
    tok = lambda w: pl.BlockSpec((None, tm, w), lambda b, i: (b, i, 0))
    pos = pl.BlockSpec((tm, LANES), lambda b, i: (i, 0))
    const2 = lambda a: pl.BlockSpec(a.shape, lambda b, i: (0, 0))
    lw = lambda name: _layer_spec(P[name].shape[1:], layer)
    names = ["g1", "w_na", "w_c", "w_kr", "w_ut", "w_uq", "w_ukv_k", "w_ukv_vt",
             "gq_na", "gk_na", "g_cq", "g_ckv", "gq", "gk", "gkr"]
    consts = ["invq", "invk", "G_na", "G_qm"]
    in_specs = ([tok(D), pl.BlockSpec((None, 6, D), lambda b, i: (b, 0, 0))]
                + [lw(n) for n in names] + [const2(P[n]) for n in consts] + [pos, pos, pos])
    out_shapes = (
        jax.ShapeDtypeStruct((B, L, NA_WIDTH), bf16),
        jax.ShapeDtypeStruct((B, L, NA_WIDTH), bf16),
        jax.ShapeDtypeStruct((B, L, NA_WIDTH), bf16),
        jax.ShapeDtypeStruct((B, MLA_HEADS * MLA_PAD, L), bf16),
        jax.ShapeDtypeStruct((B, L, MLA_HEADS * MLA_PAD), bf16),
        jax.ShapeDtypeStruct((B, MLA_HEADS * MLA_V, L), bf16),
        jax.ShapeDtypeStruct((B * L // LANES, SSM_GROUPS, SSM_GC, LANES), f32),
    )
    chan = lambda w: pl.BlockSpec((None, w, tm), lambda b, i: (b, 0, i))
    out_specs = (tok(NA_WIDTH), tok(NA_WIDTH), tok(NA_WIDTH), chan(MLA_HEADS * MLA_PAD), tok(MLA_HEADS * MLA_PAD),
                 chan(MLA_HEADS * MLA_V),
                 pl.BlockSpec((cpt, SSM_GROUPS, SSM_GC, LANES), lambda b, i: (b * nt + i, 0, 0, 0)))
    return pl.pallas_call(
        _inproj_body,
        grid=(B, nt),
        in_specs=in_specs,
        out_specs=out_specs,
        out_shape=out_shapes,
        compiler_params=_cparams(("parallel", "parallel")),
        name="in_proj",
    )(x, mod, *[P[n] for n in names], *[P[n] for n in consts], *rope_tabs)


def _na_key_col0(cb):
    return min(max(cb * NA_QCOLS - NA_COLS // 2, 0), GRID_W - NA_KCOLS)


def _na_body(q_ref, k_ref, v_ref, b_ref, o_ref, *, rows):
    i = pl.program_id(1)
    ws = jnp.clip(NA_QROWS * i - NA_ROWS // 2, 0, rows - NA_WROWS)
    base = pl.multiple_of(ws * GRID_W, GRID_W)
    lane = lax.broadcasted_iota(jnp.int32, (1, NA_WIDTH), 1)
    hms = [(lane >= h * NA_HEAD_DIM) & (lane < (h + 1) * NA_HEAD_DIM) for h in range(NA_HEADS)]
    nq = NA_QROWS * NA_QCOLS
    for cb in range(GRID_W // NA_QCOLS):
        kc0 = _na_key_col0(cb)
        kw = jnp.concatenate([k_ref[pl.ds(base + kl * GRID_W + kc0, NA_KCOLS), :] for kl in range(NA_WROWS)], axis=0)
        vw = jnp.concatenate([v_ref[pl.ds(base + kl * GRID_W + kc0, NA_KCOLS), :] for kl in range(NA_WROWS)], axis=0)
        qb = jnp.concatenate([q_ref[rl * GRID_W + cb * NA_QCOLS:rl * GRID_W + (cb + 1) * NA_QCOLS, :]
                              for rl in range(NA_QROWS)], axis=0)
        qs = jnp.concatenate([jnp.where(hm, qb, jnp.zeros_like(qb)) for hm in hms], axis=0)
        s = lax.dot_general(qs, kw, _NT, preferred_element_type=f32) + b_ref[cb].reshape(NA_HEADS * nq, -1)
        m = jnp.max(s, axis=-1, keepdims=True)
        p = jnp.exp(s - m)
        l = jnp.sum(p, axis=-1, keepdims=True)
        o = jnp.dot(p.astype(bf16), vw, preferred_element_type=f32) / l
        ob = o[:nq]
        for h in range(1, NA_HEADS):
            ob = jnp.where(hms[h], o[h * nq:(h + 1) * nq], ob)
        for rl in range(NA_QROWS):
            o_ref[rl * GRID_W + cb * NA_QCOLS:rl * GRID_W + (cb + 1) * NA_QCOLS, :] = (
                ob[rl * NA_QCOLS:(rl + 1) * NA_QCOLS].astype(bf16))


def _na_attention(q, k, v, bias, layer):
    B, L, W = q.shape
    rows = L // GRID_W
    nblk = rows // NA_QROWS
    tq = NA_QROWS * GRID_W
    ncb = GRID_W // NA_QCOLS

    def bias_map(b, i):
        variant = jnp.where(i == 0, 0, jnp.where(i == nblk - 1, 2, 1))
        return (layer, variant, 0, 0, 0, 0)

    full = pl.BlockSpec((None, L, W), lambda b, i: (b, 0, 0))
    return pl.pallas_call(
        functools.partial(_na_body, rows=rows),
        grid=(B, nblk),
        in_specs=[pl.BlockSpec((None, tq, W), lambda b, i: (b, i, 0)), full, full,
                  pl.BlockSpec((None, None, ncb, NA_HEADS, NA_QROWS * NA_QCOLS, NA_WROWS * NA_KCOLS), bias_map)],
        out_specs=pl.BlockSpec((None, tq, W), lambda b, i: (b, i, 0)),
        out_shape=jax.ShapeDtypeStruct((B, L, W), bf16),
        compiler_params=_cparams(("parallel", "arbitrary")),
        name="na_attn",
    )(q, k, v, bias)


def _na_bias_body(r_ref, o_ref):
    r = r_ref[...]
    qc = lax.broadcasted_iota(jnp.int32, (NA_QCOLS, LANES), 0)
    lane = lax.broadcasted_iota(jnp.int32, (NA_QCOLS, LANES), 1)
    piece = lane // NA_KCOLS
    neg = jnp.full((NA_QCOLS, LANES), NEG_INF, f32)
    npc = LANES // NA_KCOLS
    for cb in range(GRID_W // NA_QCOLS):
        kc0 = _na_key_col0(cb)
        qcol = cb * NA_QCOLS + qc
        kcol = kc0 + lane - piece * NA_KCOLS
        cs = jnp.clip(qcol - NA_COLS // 2, 0, GRID_W - NA_COLS)
        cvalid = (kcol >= cs) & (kcol < cs + NA_COLS)
        tiles = []
        for dr in range(2 * NA_ROWS - 1):
            rowb = jnp.broadcast_to(r[dr:dr + 1, :], (NA_QCOLS, 2 * LANES))
            t = neg
            for k in range(npc):
                base = 2 * LANES - (NA_COLS - 1) - NA_BIAS_LANE0 + cb * NA_QCOLS - kc0 + k * NA_KCOLS
                rolled = pltpu.roll(rowb, base, 1, stride=1, stride_axis=0)[:, :LANES]
                t = jnp.where(piece == k, rolled, t)
            tiles.append(jnp.where(cvalid, t, neg))
        for v, (off, lo) in enumerate(((0, 0), (-(NA_ROWS // 2), None), (-NA_ROWS, NA_ROWS // 2))):
            for rl in range(NA_QROWS):
                vregs = []
                for j in range(NA_WROWS // npc):
                    t = neg
                    for k in range(npc):
                        kl = j * npc + k
                        d = off + kl - rl
                        ok = (-(NA_ROWS // 2) <= d < NA_ROWS // 2) if lo is None else (lo <= kl < lo + NA_ROWS)
                        if ok:
                            t = jnp.where(piece == k, tiles[d + NA_ROWS - 1], t)
                    vregs.append(t)
                o_ref[v, cb, rl * NA_QCOLS:(rl + 1) * NA_QCOLS, :] = jnp.concatenate(vregs, axis=1)


def _na_bias_tables(rpb):
    depth, nh, nr, nc = rpb.shape
    rp = jnp.pad(rpb.astype(f32),
                 ((0, 0), (0, 0), (0, 16 - nr), (NA_BIAS_LANE0, 2 * LANES - NA_BIAS_LANE0 - nc)))
    ncb, nq, nk = GRID_W // NA_QCOLS, NA_QROWS * NA_QCOLS, NA_WROWS * NA_KCOLS
    return pl.pallas_call(
        _na_bias_body,
        grid=(depth, nh),
        in_specs=[pl.BlockSpec((None, None, 16, 2 * LANES), lambda l, h: (l, h, 0, 0))],
        out_specs=pl.BlockSpec((None, 3, ncb, None, nq, nk), lambda l, h: (l, 0, 0, h, 0, 0)),
        out_shape=jax.ShapeDtypeStruct((depth, 3, ncb, nh, nq, nk), f32),
        compiler_params=_cparams(("parallel", "parallel")),
        name="na_bias",
    )(rp)


def _mla_body(qt_ref, k_ref, vt_ref, o_ref, s_scr, *, nq, nk):
    tq, tk = MLA_TQ, MLA_TK
    ones = jnp.ones((MLA_VROWS - MLA_V, MLA_KC), bf16)
    chains = [(h, c) for h in range(2) for c in range(tq // MLA_CW)]
    nch = len(chains)

    def scores(t, slot):
        qoff = pl.multiple_of((t // nk) * tq, tq)
        koff = pl.multiple_of((t % nk) * tk, tk)
        qt = qt_ref[:, pl.ds(qoff, tq)]
        kb = k_ref[pl.ds(koff, tk), :]
        cmax = []
        for n, (h, c) in enumerate(chains):
            s = jnp.dot(kb[:, h * MLA_PAD:(h + 1) * MLA_PAD],
                        qt[h * MLA_PAD:(h + 1) * MLA_PAD, c * MLA_CW:(c + 1) * MLA_CW],
                        preferred_element_type=f32)
            s_scr[slot, n] = s
            cmax.append(jnp.max(s, axis=0, keepdims=True))
        return tuple(cmax)

    nkc = tk // MLA_KC

    def step(t_s, slot_s, t_a, slot_a, cmax, state):
        if t_s is not None:
            qoff_s = pl.multiple_of((t_s // nk) * tq, tq)
            koff_s = pl.multiple_of((t_s % nk) * tk, tk)
            qt = qt_ref[:, pl.ds(qoff_s, tq)]
        j = t_a % nk
        qoff = pl.multiple_of((t_a // nk) * tq, tq)
        koff = pl.multiple_of(j * tk, tk)
        tile_start = j == 0
        mns, als = [], []
        for n in range(nch):
            m = jnp.where(tile_start, -jnp.inf, state[2 * n])
            mns.append(jnp.maximum(m, cmax[n]))
            als.append(jnp.exp2(m - mns[n]))
        parts, cnew = [None] * nch, [None] * nch
        for kc in range(nkc):
            rows = slice(kc * MLA_KC, (kc + 1) * MLA_KC)
            if t_s is not None:
                kb = k_ref[pl.ds(koff_s + kc * MLA_KC, MLA_KC), :]
                for n, (h, c) in enumerate(chains):
                    s = jnp.dot(kb[:, h * MLA_PAD:(h + 1) * MLA_PAD],
                                qt[h * MLA_PAD:(h + 1) * MLA_PAD, c * MLA_CW:(c + 1) * MLA_CW],
                                preferred_element_type=f32)
                    s_scr[slot_s, n, rows, :] = s
                    cm = jnp.max(s, axis=0, keepdims=True)
                    cnew[n] = cm if cnew[n] is None else jnp.maximum(cnew[n], cm)
            vt = vt_ref[:, pl.ds(koff + kc * MLA_KC, MLA_KC)]
            for n, (h, c) in enumerate(chains):
                p = jnp.exp2(s_scr[slot_a, n, rows, :] - mns[n]).astype(bf16)
                vte = jnp.concatenate([vt[h * MLA_V:(h + 1) * MLA_V], ones], axis=0)
                d = jnp.dot(vte, p, preferred_element_type=f32)
                parts[n] = d if parts[n] is None else parts[n] + d
        new = []
        for n, (h, c) in enumerate(chains):
            acc = als[n] * state[2 * n + 1] + parts[n]
            o_ref[h * MLA_V:(h + 1) * MLA_V, pl.ds(qoff + c * MLA_CW, MLA_CW)] = (
                acc[:MLA_V] / acc[MLA_V:MLA_V + 1]).astype(bf16)
            new += [mns[n], acc]
        return (tuple(cnew) if t_s is not None else ()), tuple(new)

    nt = nq * nk
    ahead, ns = MLA_AHEAD, MLA_SLOTS

    def trip(t0, carry, last):
        pend, state = list(carry[:nch * ahead]), carry[nch * ahead:]
        for u in range(MLA_TRIP):
            t_s = None if (last and u + ahead >= MLA_TRIP) else t0 + u + ahead
            cn, state = step(t_s, (u + ahead) % ns, t0 + u, u % ns, tuple(pend[:nch]), state)
            pend = pend[nch:] + list(cn)
        return tuple(pend) + state

    m0 = jnp.full((1, MLA_CW), -jnp.inf, f32)
    a0 = jnp.zeros((MLA_VROWS, MLA_CW), f32)
    carry = ()
    for t in range(ahead):
        carry += scores(t, t)
    carry = lax.fori_loop(0, nt // MLA_TRIP - 1, lambda i, c: trip(i * MLA_TRIP, c, False), carry + (m0, a0) * nch)
    trip(nt - MLA_TRIP, carry, True)


def _mla_attention(mqt, mk, mvt):
    B, L, _ = mk.shape
    npair = MLA_HEADS // 2
    nq, nk = L // MLA_TQ, L // MLA_TK
    assert (nq * nk) % MLA_TRIP == 0 and MLA_TRIP % MLA_SLOTS == 0
    mode = dict(pipeline_mode=pl.Buffered(1)) if L * 2 * MLA_PAD * 2 > 4 * 1024 * 1024 else {}
    return pl.pallas_call(
        functools.partial(_mla_body, nq=nq, nk=nk),
        grid=(B, npair),
        in_specs=[pl.BlockSpec((None, 2 * MLA_PAD, L), lambda b, p: (b, p, 0), **mode),
                  pl.BlockSpec((None, L, 2 * MLA_PAD), lambda b, p: (b, 0, p), **mode),
                  pl.BlockSpec((None, 2 * MLA_V, L), lambda b, p: (b, p, 0), **mode)],
        out_specs=pl.BlockSpec((None, 2 * MLA_V, L), lambda b, p: (b, p, 0)),
        out_shape=jax.ShapeDtypeStruct((B, MLA_HEADS * MLA_V, L), bf16),
        scratch_shapes=[pltpu.VMEM((MLA_SLOTS, 2 * MLA_TQ // MLA_CW, MLA_TK, MLA_CW), f32)],
        compiler_params=_cparams(("parallel", "parallel")),
        name="mla_attn",
    )(mqt, mk, mvt)


def _toeplitz_body(g_ref, pf_ref, o_ref):
    kern = jnp.dot(g_ref[...], pf_ref[...], precision=lax.Precision.HIGHEST, preferred_element_type=f32)
    bits = lax.bitcast_convert_type(kern.astype(bf16).astype(f32), jnp.int32)
    for cp in range(SSM_GC):
        for c in range(0, SSM_GC, 2):
            r = cp * SSM_GC + c
            packed = lax.shift_right_logical(bits[r:r + 1, :], 16) | bits[r + 1:r + 2, :]
            rowb = jnp.broadcast_to(packed, (LANES, 2 * LANES))
            toe = pltpu.roll(rowb, LANES + 1, 1, stride=1, stride_axis=0)[:, :LANES]
            even = lax.bitcast_convert_type(lax.shift_left(toe, 16), f32)
            odd = lax.bitcast_convert_type(toe & jnp.int32(-65536), f32)
            o_ref[cp * LANES:(cp + 1) * LANES, c * LANES:(c + 1) * LANES] = even.astype(bf16)
            o_ref[cp * LANES:(cp + 1) * LANES, (c + 1) * LANES:(c + 2) * LANES] = odd.astype(bf16)


def _toeplitz_gen(gm, pf):
    dg = pf.shape[0]
    n = SSM_GC * LANES
    k = 4 * SSM_STATE
    return pl.pallas_call(
        _toeplitz_body,
        grid=(dg,),
        in_specs=[pl.BlockSpec((None, SSM_GC * SSM_GC, k), lambda g: (g, 0, 0)),
                  pl.BlockSpec((None, k, 2 * LANES), lambda g: (g, 0, 0))],
        out_specs=pl.BlockSpec((None, n, n), lambda g: (g, 0, 0)),
        out_shape=jax.ShapeDtypeStruct((dg, n, n), bf16),
        compiler_params=_cparams(("parallel",)),
        name="s5_toeplitz",
    )(gm, pf)


def _ssm_body(u_ref, t_ref, wv_ref, wc_ref, a_ref, d_ref, o_ref,
              vre, vim, xfre, xfim, xbre, xbim, *, nb, nj):
    m = nb * nj
    u32 = jnp.concatenate([u_ref[:, c, :] for c in range(SSM_GC)], axis=1)
    ub = u32.astype(bf16)
    y = jnp.dot(ub, t_ref[...], preferred_element_type=f32)
    v = jnp.dot(ub, wv_ref[...], preferred_element_type=f32)
    vre[...] = v[:, :LANES]
    vim[...] = v[:, LANES:]
    are = a_ref[0:1, :]
    aim = a_ref[1:2, :]
    is_fwd = lax.broadcasted_iota(jnp.int32, (nb, LANES), 1) < SSM_STATE

    def rows(j):
        return pl.ds(j, nb, stride=nj) if nb > 1 else pl.ds(j, 1)

    def step(k, carry):
        xr, xi = carry
        rf, rb = rows(k), rows(nj - 1 - k)
        xfre[rf, :] = xr
        xfim[rf, :] = xi
        xbre[rb, :] = xr
        xbim[rb, :] = xi
        vr = jnp.where(is_fwd, vre[rf, :], vre[rb, :])
        vi = jnp.where(is_fwd, vim[rf, :], vim[rb, :])
        return are * xr - aim * xi + vr, are * xi + aim * xr + vi

    z = jnp.zeros((nb, LANES), f32)
    lax.fori_loop(0, nj, step, (z, z))
    fwd_m = lax.broadcasted_iota(jnp.int32, (m, LANES), 1) < SSM_STATE
    xin = jnp.concatenate([jnp.where(fwd_m, xfre[...], xbre[...]),
                           jnp.where(fwd_m, xfim[...], xbim[...])], axis=1).astype(bf16)
    y = y + jnp.dot(xin, wc_ref[...], preferred_element_type=f32) + u32 * d_ref[...]
    for c in range(SSM_GC):
        o_ref[:, c, :] = y[:, c * LANES:(c + 1) * LANES]


def _ssm(ut, layer, S, nb, nj):
    m = nb * nj
    n = SSM_GC * LANES
    gspec = lambda tail: pl.BlockSpec((None,) + tail, lambda g: (layer * SSM_GROUPS + g,) + (0,) * len(tail))
    io = pl.BlockSpec((m, None, SSM_GC, LANES), lambda g: (0, g, 0, 0))
    out = pl.pallas_call(
        functools.partial(_ssm_body, nb=nb, nj=nj),
        grid=(SSM_GROUPS,),
        in_specs=[io, gspec((n, n)), gspec((n, 4 * SSM_STATE)), gspec((4 * SSM_STATE, n)),
                  gspec((2, LANES)), gspec((1, n))],
        out_specs=io,
        out_shape=jax.ShapeDtypeStruct((m, SSM_GROUPS, SSM_GC, LANES), f32),
        scratch_shapes=[pltpu.VMEM((m, LANES), f32)] * 6,
        compiler_params=_cparams(("parallel",)),
        name="s5_conv",
    )(ut, S["toe"], S["wv"], S["wc"], S["a"], S["d"])
    return out


def _ssm_tables(a_re, a_im, b_re, b_im, c_re, c_im, log_dt, ssm_d):
    depth = a_re.shape[0]
    dg = depth * SSM_GROUPS
    lam = lax.complex(a_re.astype(f32), a_im.astype(f32))
    dt = jnp.exp(log_dt.astype(f32))[..., None]
    lam_dt = lam * dt
    lam_bar = jnp.exp(lam_dt)
    b_bar = ((lam_bar - 1.0) / lam)[..., None] * lax.complex(b_re.astype(f32), b_im.astype(f32))
    cmat = lax.complex(c_re.astype(f32), c_im.astype(f32))
    steps = jnp.arange(LANES + 1, dtype=f32)
    pw = jnp.exp(lam_dt[..., None] * steps)
    pwf, pwb = pw[:, 0], pw[:, 1]

    e = np.arange(2 * LANES)
    f_ok = (e >= LANES - 1) & (e <= 2 * LANES - 2)
    b_ok = e <= LANES - 1
    pf_f = jnp.where(f_ok, pwf[..., np.clip(e - (LANES - 1), 0, LANES - 1)], 0.0)
    pf_b = jnp.where(b_ok, pwb[..., np.clip(LANES - 1 - e, 0, LANES - 1)], 0.0)
    pf = jnp.stack([pf_f.real, pf_f.imag, pf_b.real, pf_b.imag], axis=2)
    pf = pf.reshape(dg, 4 * SSM_STATE, 2 * LANES)

    bbf, bbb = b_bar[:, 0], b_bar[:, 1]
    cf, cb = cmat[:, 0], cmat[:, 1]
    gf = cf[:, :, None, :, :] * jnp.swapaxes(bbf, -1, -2)[:, :, :, None, :]
    gb = cb[:, :, None, :, :] * jnp.swapaxes(bbb, -1, -2)[:, :, :, None, :]
    gm = jnp.concatenate([gf.real, -gf.imag, gb.real, -gb.imag], axis=-1)
    gm = gm.reshape(dg, SSM_GC * SSM_GC, 4 * SSM_STATE)

    s_idx = np.arange(LANES)
    pvf = jnp.swapaxes(pwf[..., LANES - 1 - s_idx], -1, -2)
    pvb = jnp.swapaxes(pwb[..., s_idx], -1, -2)
    pv1 = jnp.concatenate([pvf.real, pvb.real, pvf.imag, pvb.imag], axis=-1).reshape(dg, LANES, 4 * SSM_STATE)
    pv2 = jnp.concatenate([pvf.imag, pvb.imag, pvf.real, pvb.real], axis=-1).reshape(dg, LANES, 4 * SSM_STATE)
    bft, bbt = jnp.swapaxes(bbf, -1, -2), jnp.swapaxes(bbb, -1, -2)
    bv = jnp.stack([jnp.concatenate([bft.real, bbt.real, bft.real, bbt.real], axis=-1),
                    jnp.concatenate([-bft.imag, -bbt.imag, bft.imag, bbt.imag], axis=-1)], axis=3)
    bv = bv.reshape(dg, SSM_GC, 2, 4 * SSM_STATE)
    t_idx = np.arange(LANES)
    pcf, pcb = pwf[..., t_idx + 1], pwb[..., LANES - t_idx]
    pc = jnp.stack([pcf.real, pcf.imag, pcb.real, pcb.imag], axis=2).reshape(dg, 4, SSM_STATE, LANES)
    cft, cbt = jnp.swapaxes(cf, -1, -2), jnp.swapaxes(cb, -1, -2)
    ct = jnp.stack([cft.real, cft.imag, cbt.real, cbt.imag], axis=2).reshape(dg, 4, SSM_STATE, SSM_GC)

    a128 = jnp.concatenate([pwf[..., LANES], pwb[..., LANES]], axis=-1)
    a = jnp.stack([a128.real, a128.imag], axis=2).reshape(dg, 2, LANES)
    d = jnp.repeat(ssm_d.astype(f32).reshape(dg, SSM_GC), LANES, axis=-1).reshape(dg, 1, SSM_GC * LANES)
    return dict(pf=pf, gm=gm, pv1=pv1, pv2=pv2, bv=bv, pc=pc, ct=ct, a=a, d=d)


def _state_tables_body(pv1_ref, pv2_ref, bv_ref, pc_ref, ct_ref, wv_ref, wc_ref):
    pv1, pv2 = pv1_ref[...], pv2_ref[...]
    for cp in range(SSM_GC):
        wv_ref[cp * LANES:(cp + 1) * LANES, :] = (pv1 * bv_ref[cp, 0:1, :] + pv2 * bv_ref[cp, 1:2, :]).astype(bf16)
    pfr, pfi, pbr, pbi = pc_ref[0], pc_ref[1], pc_ref[2], pc_ref[3]
    cfr, cfi, cbr, cbi = ct_ref[0], ct_ref[1], ct_ref[2], ct_ref[3]
    for c in range(SSM_GC):
        col = slice(c, c + 1)
        tile = jnp.concatenate([cfr[:, col] * pfr - cfi[:, col] * pfi,
                                cbr[:, col] * pbr - cbi[:, col] * pbi,
                                -(cfr[:, col] * pfi + cfi[:, col] * pfr),
                                -(cbr[:, col] * pbi + cbi[:, col] * pbr)], axis=0)
        wc_ref[:, c * LANES:(c + 1) * LANES] = tile.astype(bf16)


def _state_tables(S):
    dg = S["pf"].shape[0]
    n, k = SSM_GC * LANES, 4 * SSM_STATE
    g3 = lambda a, b: pl.BlockSpec((None, a, b), lambda g: (g, 0, 0))
    g4 = lambda a, b, c: pl.BlockSpec((None, a, b, c), lambda g: (g, 0, 0, 0))
    return pl.pallas_call(
        _state_tables_body,
        grid=(dg,),
        in_specs=[g3(LANES, k), g3(LANES, k), g4(SSM_GC, 2, k), g4(4, SSM_STATE, LANES), g4(4, SSM_STATE, SSM_GC)],
        out_specs=(g3(n, k), g3(k, n)),
        out_shape=(jax.ShapeDtypeStruct((dg, n, k), bf16), jax.ShapeDtypeStruct((dg, k, n), bf16)),
        compiler_params=_cparams(("parallel",)),
        name="s5_state_tables",
    )(S["pv1"], S["pv2"], S["bv"], S["pc"], S["ct"])


def _mixffn_body(x_ref, mod_ref, yna_ref, ymlat_ref, yst_ref, gluw_ref, glub_ref, wo_ref, g2_ref,
                 wg_ref, wu_ref, wd_ref, o_ref, act_ref):
    ys = jnp.concatenate([yst_ref[c].reshape(SSM_WIDTH, LANES).T for c in range(yst_ref.shape[0])],
                         axis=0)
    ys = 0.5 * ys * (1.0 + jnp.tanh(math.sqrt(2.0 / math.pi) * (ys + 0.044715 * (ys * ys * ys))))
    gate = jax.nn.sigmoid(jnp.dot(ys.astype(bf16), gluw_ref[...], preferred_element_type=f32) + glub_ref[...])
    yssm = (ys * gate).astype(bf16)
    o1 = NA_WIDTH
    o2 = NA_WIDTH + MLA_HEADS * MLA_V
    mix = (jnp.dot(yna_ref[...], wo_ref[0:o1, :], preferred_element_type=f32)
           + lax.dot_general(ymlat_ref[...], wo_ref[o1:o2, :], _TN, preferred_element_type=f32)
           + jnp.dot(yssm, wo_ref[o2:, :], preferred_element_type=f32))
    x1 = x_ref[...] + mod_ref[2:3, :] * mix
    o_ref[...] = x1
    ms = jnp.mean(x1 * x1, axis=-1, keepdims=True)
    h2 = ((x1 * lax.rsqrt(ms + RMS_EPS) * g2_ref[...]) * (1.0 + mod_ref[4:5, :]) + mod_ref[3:4, :]).astype(bf16)

    th = 256
    for c in range(FFN_HIDDEN // th):
        cs = slice(c * th, (c + 1) * th)
        g = jnp.dot(h2, wg_ref[:, cs], preferred_element_type=f32)
        u = jnp.dot(h2, wu_ref[:, cs], preferred_element_type=f32)
        act_ref[:, cs] = (g * jax.nn.sigmoid(g) * u).astype(bf16)
    ffn = jnp.dot(act_ref[...], wd_ref[...], preferred_element_type=f32)
    o_ref[...] = o_ref[...] + mod_ref[5:6, :] * ffn


def _mixffn(x, mod, yna, ymlat, yst, layer, P):
    B, L, D = x.shape
    tm = TOKEN_TILE
    nt = L // tm
    cpt = tm // LANES
    tok = lambda w: pl.BlockSpec((None, tm, w), lambda b, i: (b, i, 0))

    def wspec(name):
        tail = P[name].shape[1:]
        return pl.BlockSpec((None,) + tail, lambda b, i: (layer,) + (0,) * len(tail), pipeline_mode=pl.Buffered(1))

    return pl.pallas_call(
        _mixffn_body,
        grid=(B, nt),
        in_specs=[tok(D), pl.BlockSpec((None, 6, D), lambda b, i: (b, 0, 0)), tok(NA_WIDTH),
                  pl.BlockSpec((None, MLA_HEADS * MLA_V, tm), lambda b, i: (b, 0, i)),
                  pl.BlockSpec((cpt, SSM_GROUPS, SSM_GC, LANES), lambda b, i: (b * nt + i, 0, 0, 0)),
                  wspec("glu_w"), wspec("glu_b"), wspec("w_out"), wspec("g2"),
                  wspec("w_gate"), wspec("w_up"), wspec("w_down")],
        out_specs=tok(D),
        out_shape=jax.ShapeDtypeStruct((B, L, D), f32),
        scratch_shapes=[pltpu.VMEM((tm, FFN_HIDDEN), bf16)],
        compiler_params=_cparams(("parallel", "parallel")),
        name="mix_ffn",
    )(x, mod, yna, ymlat, yst, P["glu_w"], P["glu_b"], P["w_out"], P["g2"], P["w_gate"], P["w_up"], P["w_down"])


def _prep_params(norm1_g, w_in, na_q_g, na_k_g, mla_cq_g, mla_ckv_g, mla_w_uq, mla_w_ukv, mla_qn_g, mla_kn_g,
                 mla_qr_g, mla_kr_g, glu_w, glu_b, w_out, norm2_g, ffn_w_gate, ffn_w_up, ffn_w_down):
    depth = w_in.shape[0]
    o1 = 3 * NA_WIDTH
    o2 = o1 + MLA_Q_LORA
    o3 = o2 + MLA_KV_LORA
    o4 = o3 + MLA_ROPE
    row = lambda a: a.astype(f32)[:, None, :]
    zeros = lambda *s: jnp.zeros(s, f32)
    P = {}
    P["g1"] = row(norm1_g)
    P["g2"] = row(norm2_g)
    P["w_na"] = w_in[:, :, :o1].astype(bf16)
    P["w_c"] = w_in[:, :, o1:o3].astype(bf16)
    P["w_kr"] = jnp.concatenate([zeros(depth, D_MODEL, MLA_NOPE), w_in[:, :, o3:o4],
                                 zeros(depth, D_MODEL, MLA_PAD - MLA_NOPE - MLA_ROPE)], axis=-1).astype(bf16)
    P["w_ut"] = jnp.swapaxes(w_in[:, :, o4:], 1, 2).astype(bf16)
    wq = mla_w_uq.reshape(depth, MLA_Q_LORA, MLA_HEADS, MLA_NOPE + MLA_ROPE)
    wq = jnp.concatenate([wq, zeros(depth, MLA_Q_LORA, MLA_HEADS, MLA_PAD - MLA_NOPE - MLA_ROPE)], axis=-1)
    P["w_uq"] = wq.reshape(depth, MLA_Q_LORA, MLA_HEADS * MLA_PAD).astype(bf16)
    wkv = mla_w_ukv.reshape(depth, MLA_KV_LORA, MLA_HEADS, MLA_NOPE + MLA_V)
    wk = jnp.concatenate([wkv[..., :MLA_NOPE], zeros(depth, MLA_KV_LORA, MLA_HEADS, MLA_PAD - MLA_NOPE)], axis=-1)
    P["w_ukv_k"] = wk.reshape(depth, MLA_KV_LORA, MLA_HEADS * MLA_PAD).astype(bf16)
    P["w_ukv_vt"] = jnp.swapaxes(wkv[..., MLA_NOPE:].reshape(depth, MLA_KV_LORA, MLA_HEADS * MLA_V), 1, 2).astype(bf16)
    P["gq_na"] = row(jnp.tile(na_q_g, (1, NA_HEADS))) * (NA_HEAD_DIM ** -0.5)
    P["gk_na"] = row(jnp.tile(na_k_g, (1, NA_HEADS)))
    P["g_cq"] = row(mla_cq_g)
    P["g_ckv"] = row(mla_ckv_g)
    scale = (MLA_NOPE + MLA_ROPE) ** -0.5 * math.log2(math.e)
    pad = MLA_PAD - MLA_NOPE - MLA_ROPE
    gq = jnp.concatenate([mla_qn_g, mla_qr_g, zeros(depth, pad)], axis=-1) * scale
    P["gq"] = row(jnp.tile(gq, (1, MLA_HEADS)))
    gk = jnp.concatenate([mla_kn_g, zeros(depth, MLA_PAD - MLA_NOPE)], axis=-1)
    P["gk"] = row(jnp.tile(gk, (1, MLA_HEADS)))
    P["gkr"] = row(jnp.concatenate([zeros(depth, MLA_NOPE), mla_kr_g, zeros(depth, pad)], axis=-1))
    invq = np.concatenate([np.full(MLA_NOPE, 1.0 / MLA_NOPE), np.full(MLA_ROPE, 1.0 / MLA_ROPE), np.zeros(pad)])
    invk = np.concatenate([np.full(MLA_NOPE, 1.0 / MLA_NOPE), np.zeros(MLA_PAD - MLA_NOPE)])
    P["invq"] = jnp.asarray(np.tile(invq, MLA_HEADS)[None, :], f32)
    P["invk"] = jnp.asarray(np.tile(invk, MLA_HEADS)[None, :], f32)
    lane = np.arange(NA_WIDTH)
    P["G_na"] = jnp.asarray((lane[:, None] // NA_HEAD_DIM) == (lane[None, :] // NA_HEAD_DIM), bf16)
    lane = np.arange(2 * MLA_PAD)
    grp = np.where(lane % MLA_PAD < MLA_NOPE, 0, np.where(lane % MLA_PAD < MLA_NOPE + MLA_ROPE, 1, 2))
    same = (lane[:, None] // MLA_PAD == lane[None, :] // MLA_PAD) & (grp[:, None] == grp[None, :]) & (grp[:, None] < 2)
    P["G_qm"] = jnp.asarray(same, bf16)
    P["glu_w"] = glu_w.astype(bf16)
    P["glu_b"] = row(glu_b)
    P["w_out"] = w_out.astype(bf16)
    P["w_gate"] = ffn_w_gate.astype(bf16)
    P["w_up"] = ffn_w_up.astype(bf16)
    P["w_down"] = ffn_w_down.astype(bf16)
    return P


def _rope_tables(length):
    inv = 1.0 / (ROPE_BASE ** (jnp.arange(0, MLA_ROPE, 2, dtype=f32) / MLA_ROPE))
    ang = jnp.arange(length, dtype=f32)[:, None] * inv[None, :]
    cos, sin = jnp.cos(ang), jnp.sin(ang)
    half = MLA_ROPE // 2
    z = lambda w: jnp.zeros((length, w), f32)
    pad = MLA_PAD - MLA_NOPE - MLA_ROPE
    cos_t = jnp.concatenate([jnp.ones((length, MLA_NOPE), f32), cos, cos, z(pad)], axis=-1)
    sin_lo = jnp.concatenate([z(MLA_NOPE), -sin, z(half), z(pad)], axis=-1)
    sin_hi = jnp.concatenate([z(MLA_NOPE), z(half), sin, z(pad)], axis=-1)
    return cos_t, sin_lo, sin_hi


def _trunk(x, mods, P, S, bias):
    B, L, _ = x.shape
    rope_tabs = _rope_tables(L)
    nj = L // LANES
    for layer in range(len(mods)):
        mod = mods[layer]
        naq, nak, nav, mq, mk, mvt, ut = _inproj(x, mod, layer, P, rope_tabs)
        yna = _na_attention(naq, nak, nav, bias, layer)
        ymla = _mla_attention(mq, mk, mvt)
        yst = _ssm(ut, layer, S, B, nj)
        x = _mixffn(x, mod, yna, ymla, yst, layer, P)
    return x


def kernel(x_prompt, x_sample, c_prompt, c_sample, ada_w, ada_b, norm1_g, w_in, na_q_g, na_k_g, na_rpb, mla_cq_g,
           mla_ckv_g, mla_w_uq, mla_w_ukv, mla_qn_g, mla_kn_g, mla_qr_g, mla_kr_g, ssm_a_re, ssm_a_im, ssm_b_re,
           ssm_b_im, ssm_c_re, ssm_c_im, ssm_log_dt, ssm_d, glu_w, glu_b, w_out, norm2_g, ffn_w_gate, ffn_w_up,
           ffn_w_down):
    depth = w_in.shape[0]
    nbp, nbs = c_prompt.shape[0], c_sample.shape[0]
    rows = -(-(nbp + nbs) // 8) * 8
    c_all = jnp.concatenate([c_prompt, c_sample, jnp.zeros((rows - nbp - nbs, D_MODEL), f32)], axis=0)
    mod = _modulation(c_all, ada_w, ada_b).reshape(depth, rows, 6, D_MODEL)
    mods_p = [mod[l, :nbp] for l in range(depth)]
    mods_s = [mod[l, nbp:nbp + nbs] for l in range(depth)]

    P = _prep_params(norm1_g, w_in, na_q_g, na_k_g, mla_cq_g, mla_ckv_g, mla_w_uq, mla_w_ukv, mla_qn_g, mla_kn_g,
                     mla_qr_g, mla_kr_g, glu_w, glu_b, w_out, norm2_g, ffn_w_gate, ffn_w_up, ffn_w_down)
    S = _ssm_tables(ssm_a_re, ssm_a_im, ssm_b_re, ssm_b_im, ssm_c_re, ssm_c_im, ssm_log_dt, ssm_d)
    S["toe"] = _toeplitz_gen(S["gm"], S["pf"])
    S["wv"], S["wc"] = _state_tables(S)
    bias = _na_bias_tables(na_rpb)

    y_prompt = _trunk(x_prompt, mods_p, P, S, bias)
    y_sample = _trunk(x_sample, mods_s, P, S, bias)
    return (y_prompt, y_sample)
```

```python
import functools
import math

import numpy as np
import jax
import jax.numpy as jnp
from jax import lax
from jax.experimental import pallas as pl
from jax.experimental.pallas import tpu as pltpu

f32 = jnp.float32
bf16 = jnp.bfloat16

D_MODEL = 1024
GRID_W = 64
NA_HEADS = 4
NA_HEAD_DIM = 64
NA_WIDTH = NA_HEADS * NA_HEAD_DIM
NA_ROWS = 8
NA_COLS = 16
MLA_HEADS = 8
MLA_NOPE = 64
MLA_ROPE = 32
MLA_V = 64
MLA_Q_LORA = 384
MLA_KV_LORA = 256
ROPE_BASE = 10000.0
SSM_GROUPS = 16
SSM_GC = 16
SSM_STATE = 64
SSM_WIDTH = SSM_GROUPS * SSM_GC
FFN_HIDDEN = 2816
RMS_EPS = 1e-6
NEG_INF = -1e30

LANES = 128
MLA_PAD = 128
NA_QCOLS = 16
NA_KCOLS = 2 * NA_COLS
NA_QROWS = 4
NA_SUB = 2
NA_WROWS = NA_QROWS + NA_ROWS
NA_BIAS_LANE0 = 200
TOKEN_TILE = 512
MLA_CW = 256
MLA_TQ = MLA_CW
MLA_TK = 1024
MLA_KC = 256
MLA_AHEAD = 2
MLA_SLOTS = 2 * MLA_AHEAD
MLA_TRIP = 8
MLA_VROWS = MLA_V + 16
VMEM_LIMIT = 48 * 1024 * 1024

_NT = (((1,), (1,)), ((), ()))
_TN = (((0,), (0,)), ((), ()))


def _cparams(sem):
    return pltpu.CompilerParams(dimension_semantics=sem, vmem_limit_bytes=VMEM_LIMIT)


def _layer_spec(tail, layer):
    n = len(tail)
    return pl.BlockSpec((None,) + tuple(tail), lambda *_: (layer,) + (0,) * n)


def _mod_body(c_ref, w_ref, b_ref, o_ref):
    c = c_ref[...]
    s = c * jax.nn.sigmoid(c)
    o_ref[...] = jnp.dot(s, w_ref[...], precision=lax.Precision.HIGHEST, preferred_element_type=f32) + b_ref[...]


def _modulation(c_all, ada_w, ada_b):
    depth, d, n = ada_w.shape
    tn = 1536
    rows = c_all.shape[0]
    return pl.pallas_call(
        _mod_body,
        grid=(depth, n // tn),
        in_specs=[
            pl.BlockSpec((rows, d), lambda l, j: (0, 0)),
            pl.BlockSpec((None, d, tn), lambda l, j: (l, 0, j)),
            pl.BlockSpec((None, 1, tn), lambda l, j: (l, 0, j)),
        ],
        out_specs=pl.BlockSpec((None, rows, tn), lambda l, j: (l, 0, j)),
        out_shape=jax.ShapeDtypeStruct((depth, rows, n), f32),
        compiler_params=_cparams(("arbitrary", "arbitrary")),
        name="adaln_mod",
    )(c_all, ada_w, ada_b.reshape(depth, 1, n))


def _rope(x, cos, sin_lo, sin_hi):
    return x * cos + pltpu.roll(x, LANES - 16, 1) * sin_lo + pltpu.roll(x, 16, 1) * sin_hi


def _inproj_body(x_ref, mod_ref, g1_ref, wna_ref, wc_ref, wkr_ref, wut_ref, wuq_ref, wukvk_ref, wukvvt_ref,
                 gqna_ref, gkna_ref, gcq_ref, gckv_ref, gq_ref, gk_ref, gkr_ref, invq_ref, invk_ref,
                 gna_ref, gqm_ref, cos_ref, sinlo_ref, sinhi_ref,
                 naq_ref, nak_ref, nav_ref, mqt_ref, mk_ref, mvt_ref, ut_ref):
    x = x_ref[...]
    shift1 = mod_ref[0:1, :]
    scale1 = mod_ref[1:2, :]
    ms = jnp.mean(x * x, axis=-1, keepdims=True)
    h = (x * lax.rsqrt(ms + RMS_EPS) * g1_ref[...]) * (1.0 + scale1) + shift1
    hb = h.astype(bf16)

    z = jnp.dot(hb, wna_ref[...], preferred_element_type=f32)
    q = z[:, :NA_WIDTH]
    k = z[:, NA_WIDTH:2 * NA_WIDTH]
    gna = gna_ref[...]
    ssq = jnp.dot((q * q).astype(bf16), gna, preferred_element_type=f32) * (1.0 / NA_HEAD_DIM)
    ssk = jnp.dot((k * k).astype(bf16), gna, preferred_element_type=f32) * (1.0 / NA_HEAD_DIM)
    naq_ref[...] = (q * lax.rsqrt(ssq + RMS_EPS) * gqna_ref[...]).astype(bf16)
    nak_ref[...] = (k * lax.rsqrt(ssk + RMS_EPS) * gkna_ref[...]).astype(bf16)
    nav_ref[...] = z[:, 2 * NA_WIDTH:].astype(bf16)

    zc = jnp.dot(hb, wc_ref[...], preferred_element_type=f32)
    cq = zc[:, :MLA_Q_LORA]
    ckv = zc[:, MLA_Q_LORA:]
    cqn = (cq * lax.rsqrt(jnp.mean(cq * cq, axis=-1, keepdims=True) + RMS_EPS) * gcq_ref[...]).astype(bf16)
    ckvn = (ckv * lax.rsqrt(jnp.mean(ckv * ckv, axis=-1, keepdims=True) + RMS_EPS) * gckv_ref[...]).astype(bf16)

    cos = cos_ref[...]
    sin_lo = sinlo_ref[...]
    sin_hi = sinhi_ref[...]
    gqm = gqm_ref[...]

    kr = jnp.dot(hb, wkr_ref[...], preferred_element_type=f32)
    kr_ms = jnp.sum(kr * kr, axis=-1, keepdims=True) * (1.0 / MLA_ROPE)
    kr = _rope(kr * lax.rsqrt(kr_ms + RMS_EPS) * gkr_ref[...], cos, sin_lo, sin_hi)

    qraw = jnp.dot(cqn, wuq_ref[...], preferred_element_type=f32)
    kraw = jnp.dot(ckvn, wukvk_ref[...], preferred_element_type=f32)
    for p in range(MLA_HEADS // 2):
        sl = slice(2 * p * MLA_PAD, 2 * (p + 1) * MLA_PAD)
        qs = qraw[:, sl]
        ss = jnp.dot((qs * qs).astype(bf16), gqm, preferred_element_type=f32) * invq_ref[:, sl]
        qn = qs * lax.rsqrt(ss + RMS_EPS) * gq_ref[:, sl]
        ks = kraw[:, sl]
        ss = jnp.dot((ks * ks).astype(bf16), gqm, preferred_element_type=f32) * invk_ref[:, sl]
        kn = ks * lax.rsqrt(ss + RMS_EPS) * gk_ref[:, sl]
        for hh in range(2):
            lo = (2 * p + hh) * MLA_PAD
            piece = qn[:, hh * MLA_PAD:(hh + 1) * MLA_PAD]
            mqt_ref[lo:lo + MLA_PAD, :] = _rope(piece, cos, sin_lo, sin_hi).T.astype(bf16)
            mk_ref[:, lo:lo + MLA_PAD] = (kn[:, hh * MLA_PAD:(hh + 1) * MLA_PAD] + kr).astype(bf16)
    mvt_ref[...] = lax.dot_general(wukvvt_ref[...], ckvn, _NT, preferred_element_type=f32).astype(bf16)

    ut = lax.dot_general(wut_ref[...], hb, _NT, preferred_element_type=f32)
    for c in range(ut_ref.shape[0]):
        ut_ref[c] = ut[:, c * LANES:(c + 1) * LANES].reshape(SSM_GROUPS, SSM_GC, LANES)


def _inproj(x, mod, layer, P, rope_tabs):
    B, L, D = x.shape
    tm = TOKEN_TILE
    nt = L // tm
    cpt = tm // LANES
    tok = lambda w: pl.BlockSpec((None, tm, w), lambda b, i: (b, i, 0))
    pos = pl.BlockSpec((tm, LANES), lambda b, i: (i, 0))
    const2 = lambda a: pl.BlockSpec(a.shape, lambda b, i: (0, 0))
    lw = lambda name: _layer_spec(P[name].shape[1:], layer)
    names = ["g1", "w_na", "w_c", "w_kr", "w_ut", "w_uq", "w_ukv_k", "w_ukv_vt",
             "gq_na", "gk_na", "g_cq", "g_ckv", "gq", "gk", "gkr"]
    consts = ["invq", "invk", "G_na", "G_qm"]
    in_specs = ([tok(D), pl.BlockSpec((None, 6, D), lambda b, i: (b, 0, 0))]
                + [lw(n) for n in names] + [const2(P[n]) for n in consts] + [pos, pos, pos])
    out_shapes = (
        jax.ShapeDtypeStruct((B, L, NA_WIDTH), bf16),
        jax.ShapeDtypeStruct((B, L, NA_WIDTH), bf16),
        jax.ShapeDtypeStruct((B, L, NA_WIDTH), bf16),
        jax.ShapeDtypeStruct((B, MLA_HEADS * MLA_PAD, L), bf16),
        jax.ShapeDtypeStruct((B, L, MLA_HEADS * MLA_PAD), bf16),
        jax.ShapeDtypeStruct((B, MLA_HEADS * MLA_V, L), bf16),
        jax.ShapeDtypeStruct((B * L // LANES, SSM_GROUPS, SSM_GC, LANES), f32),
    )
    chan = lambda w: pl.BlockSpec((None, w, tm), lambda b, i: (b, 0, i))
    out_specs = (tok(NA_WIDTH), tok(NA_WIDTH), tok(NA_WIDTH), chan(MLA_HEADS * MLA_PAD), tok(MLA_HEADS * MLA_PAD),
                 chan(MLA_HEADS * MLA_V),
                 pl.BlockSpec((cpt, SSM_GROUPS, SSM_GC, LANES), lambda b, i: (b * nt + i, 0, 0, 0)))
    return pl.pallas_call(
        _inproj_body,
        grid=(B, nt),
        in_specs=in_specs,
        out_specs=out_specs,
        out_shape=out_shapes,
        compiler_params=_cparams(("parallel", "parallel")),
        name="in_proj",
    )(x, mod, *[P[n] for n in names], *[P[n] for n in consts], *rope_tabs)


def _na_key_col0(cb):
    return min(max(cb * NA_QCOLS - NA_COLS // 2, 0), GRID_W - NA_KCOLS)


def _na_body(q_ref, k_ref, v_ref, b_ref, o_ref, *, rows):
    nblk = rows // NA_QROWS
    lane = lax.broadcasted_iota(jnp.int32, (1, NA_WIDTH), 1)
    hms = [(lane >= h * NA_HEAD_DIM) & (lane < (h + 1) * NA_HEAD_DIM) for h in range(NA_HEADS)]
    nq = NA_QROWS * NA_QCOLS
    for sb in range(NA_SUB):
        blk = pl.program_id(1) * NA_SUB + sb
        variant = jnp.where(blk == 0, 0, jnp.where(blk == nblk - 1, 2, 1))
        ws = jnp.clip(NA_QROWS * blk - NA_ROWS // 2, 0, rows - NA_WROWS)
        base = pl.multiple_of(ws * GRID_W, GRID_W)
        q0 = sb * NA_QROWS * GRID_W
        for cb in range(GRID_W // NA_QCOLS):
            kc0 = _na_key_col0(cb)
            kw = jnp.concatenate([k_ref[pl.ds(base + kl * GRID_W + kc0, NA_KCOLS), :] for kl in range(NA_WROWS)],
                                 axis=0)
            vw = jnp.concatenate([v_ref[pl.ds(base + kl * GRID_W + kc0, NA_KCOLS), :] for kl in range(NA_WROWS)],
                                 axis=0)
            qrows = [slice(q0 + rl * GRID_W + cb * NA_QCOLS, q0 + rl * GRID_W + (cb + 1) * NA_QCOLS)
                     for rl in range(NA_QROWS)]
            qb = jnp.concatenate([q_ref[r, :] for r in qrows], axis=0)
            qs = jnp.concatenate([jnp.where(hm, qb, jnp.zeros_like(qb)) for hm in hms], axis=0)
            s = lax.dot_general(qs, kw, _NT, preferred_element_type=f32) + b_ref[variant, cb].reshape(NA_HEADS * nq, -1)
            m = jnp.max(s, axis=-1, keepdims=True)
            p = jnp.exp(s - m)
            l = jnp.sum(p, axis=-1, keepdims=True)
            o = jnp.dot(p.astype(bf16), vw, preferred_element_type=f32) / l
            ob = o[:nq]
            for h in range(1, NA_HEADS):
                ob = jnp.where(hms[h], o[h * nq:(h + 1) * nq], ob)
            for rl, r in enumerate(qrows):
                o_ref[r, :] = ob[rl * NA_QCOLS:(rl + 1) * NA_QCOLS].astype(bf16)


def _na_attention(q, k, v, bias, layer):
    B, L, W = q.shape
    rows = L // GRID_W
    tq = NA_SUB * NA_QROWS * GRID_W
    ncb = GRID_W // NA_QCOLS
    mode = dict(pipeline_mode=pl.Buffered(1)) if L * W * 2 > 4 * 1024 * 1024 else {}
    full = pl.BlockSpec((None, L, W), lambda b, i: (b, 0, 0), **mode)
    return pl.pallas_call(
        functools.partial(_na_body, rows=rows),
        grid=(B, L // tq),
        in_specs=[pl.BlockSpec((None, tq, W), lambda b, i: (b, i, 0)), full, full,
                  pl.BlockSpec((None, 3, ncb, NA_HEADS, NA_QROWS * NA_QCOLS, NA_WROWS * NA_KCOLS),
                               lambda b, i: (layer, 0, 0, 0, 0, 0), pipeline_mode=pl.Buffered(1))],
        out_specs=pl.BlockSpec((None, tq, W), lambda b, i: (b, i, 0)),
        out_shape=jax.ShapeDtypeStruct((B, L, W), bf16),
        compiler_params=_cparams(("parallel", "arbitrary")),
        name="na_attn",
    )(q, k, v, bias)


def _na_bias_body(r_ref, o_ref):
    r = r_ref[...]
    qc = lax.broadcasted_iota(jnp.int32, (NA_QCOLS, LANES), 0)
    lane = lax.broadcasted_iota(jnp.int32, (NA_QCOLS, LANES), 1)
    piece = lane // NA_KCOLS
    neg = jnp.full((NA_QCOLS, LANES), NEG_INF, f32)
    npc = LANES // NA_KCOLS
    for cb in range(GRID_W // NA_QCOLS):
        kc0 = _na_key_col0(cb)
        qcol = cb * NA_QCOLS + qc
        kcol = kc0 + lane - piece * NA_KCOLS
        cs = jnp.clip(qcol - NA_COLS // 2, 0, GRID_W - NA_COLS)
        cvalid = (kcol >= cs) & (kcol < cs + NA_COLS)
        tiles = []
        for dr in range(2 * NA_ROWS - 1):
            rowb = jnp.broadcast_to(r[dr:dr + 1, :], (NA_QCOLS, 2 * LANES))
            t = neg
            for k in range(npc):
                base = 2 * LANES - (NA_COLS - 1) - NA_BIAS_LANE0 + cb * NA_QCOLS - kc0 + k * NA_KCOLS
                rolled = pltpu.roll(rowb, base, 1, stride=1, stride_axis=0)[:, :LANES]
                t = jnp.where(piece == k, rolled, t)
            tiles.append(jnp.where(cvalid, t, neg))
        for v, (off, lo) in enumerate(((0, 0), (-(NA_ROWS // 2), None), (-NA_ROWS, NA_ROWS // 2))):
            for rl in range(NA_QROWS):
                vregs = []
                for j in range(NA_WROWS // npc):
                    t = neg
                    for k in range(npc):
                        kl = j * npc + k
                        d = off + kl - rl
                        ok = (-(NA_ROWS // 2) <= d < NA_ROWS // 2) if lo is None else (lo <= kl < lo + NA_ROWS)
                        if ok:
                            t = jnp.where(piece == k, tiles[d + NA_ROWS - 1], t)
                    vregs.append(t)
                o_ref[v, cb, rl * NA_QCOLS:(rl + 1) * NA_QCOLS, :] = jnp.concatenate(vregs, axis=1)


def _na_bias_tables(rpb):
    depth, nh, nr, nc = rpb.shape
    rp = jnp.pad(rpb.astype(f32),
                 ((0, 0), (0, 0), (0, 16 - nr), (NA_BIAS_LANE0, 2 * LANES - NA_BIAS_LANE0 - nc)))
    ncb, nq, nk = GRID_W // NA_QCOLS, NA_QROWS * NA_QCOLS, NA_WROWS * NA_KCOLS
    return pl.pallas_call(
        _na_bias_body,
        grid=(depth, nh),
        in_specs=[pl.BlockSpec((None, None, 16, 2 * LANES), lambda l, h: (l, h, 0, 0))],
        out_specs=pl.BlockSpec((None, 3, ncb, None, nq, nk), lambda l, h: (l, 0, 0, h, 0, 0)),
        out_shape=jax.ShapeDtypeStruct((depth, 3, ncb, nh, nq, nk), f32),
        compiler_params=_cparams(("parallel", "parallel")),
        name="na_bias",
    )(rp)


def _mla_body(qt_ref, k_ref, vt_ref, o_ref, s_scr, *, nq, nk):
    tq, tk = MLA_TQ, MLA_TK
    ones = jnp.ones((MLA_VROWS - MLA_V, MLA_KC), bf16)
    chains = [(h, c) for h in range(2) for c in range(tq // MLA_CW)]
    nch = len(chains)

    def scores(t, slot):
        qoff = pl.multiple_of((t // nk) * tq, tq)
        koff = pl.multiple_of((t % nk) * tk, tk)
        qt = qt_ref[:, pl.ds(qoff, tq)]
        kb = k_ref[pl.ds(koff, tk), :]
        cmax = []
        for n, (h, c) in enumerate(chains):
            s = jnp.dot(kb[:, h * MLA_PAD:(h + 1) * MLA_PAD],
                        qt[h * MLA_PAD:(h + 1) * MLA_PAD, c * MLA_CW:(c + 1) * MLA_CW],
                        preferred_element_type=f32)
            s_scr[slot, n] = s
            cmax.append(jnp.max(s, axis=0, keepdims=True))
        return tuple(cmax)

    nkc = tk // MLA_KC

    def step(t_s, slot_s, t_a, slot_a, cmax, state):
        if t_s is not None:
            qoff_s = pl.multiple_of((t_s // nk) * tq, tq)
            koff_s = pl.multiple_of((t_s % nk) * tk, tk)
            qt = qt_ref[:, pl.ds(qoff_s, tq)]
        j = t_a % nk
        qoff = pl.multiple_of((t_a // nk) * tq, tq)
        koff = pl.multiple_of(j * tk, tk)
        tile_start = j == 0
        mns, als = [], []
        for n in range(nch):
            m = jnp.where(tile_start, -jnp.inf, state[2 * n])
            mns.append(jnp.maximum(m, cmax[n]))
            als.append(jnp.exp2(m - mns[n]))
        parts, cnew = [None] * nch, [None] * nch
        for kc in range(nkc):
            rows = slice(kc * MLA_KC, (kc + 1) * MLA_KC)
            if t_s is not None:
                kb = k_ref[pl.ds(koff_s + kc * MLA_KC, MLA_KC), :]
                for n, (h, c) in enumerate(chains):
                    s = jnp.dot(kb[:, h * MLA_PAD:(h + 1) * MLA_PAD],
                                qt[h * MLA_PAD:(h + 1) * MLA_PAD, c * MLA_CW:(c + 1) * MLA_CW],
                                preferred_element_type=f32)
                    s_scr[slot_s, n, rows, :] = s
                    cm = jnp.max(s, axis=0, keepdims=True)
                    cnew[n] = cm if cnew[n] is None else jnp.maximum(cnew[n], cm)
            vt = vt_ref[:, pl.ds(koff + kc * MLA_KC, MLA_KC)]
            for n, (h, c) in enumerate(chains):
                p = jnp.exp2(s_scr[slot_a, n, rows, :] - mns[n]).astype(bf16)
                vte = jnp.concatenate([vt[h * MLA_V:(h + 1) * MLA_V], ones], axis=0)
                d = jnp.dot(vte, p, preferred_element_type=f32)
                parts[n] = d if parts[n] is None else parts[n] + d
        new = []
        for n, (h, c) in enumerate(chains):
            acc = als[n] * state[2 * n + 1] + parts[n]
            o_ref[h * MLA_V:(h + 1) * MLA_V, pl.ds(qoff + c * MLA_CW, MLA_CW)] = (
                acc[:MLA_V] / acc[MLA_V:MLA_V + 1]).astype(bf16)
            new += [mns[n], acc]
        return (tuple(cnew) if t_s is not None else ()), tuple(new)

    nt = nq * nk
    ahead, ns = MLA_AHEAD, MLA_SLOTS

    def trip(t0, carry, last):
        pend, state = list(carry[:nch * ahead]), carry[nch * ahead:]
        for u in range(MLA_TRIP):
            t_s = None if (last and u + ahead >= MLA_TRIP) else t0 + u + ahead
            cn, state = step(t_s, (u + ahead) % ns, t0 + u, u % ns, tuple(pend[:nch]), state)
            pend = pend[nch:] + list(cn)
        return tuple(pend) + state

    m0 = jnp.full((1, MLA_CW), -jnp.inf, f32)
    a0 = jnp.zeros((MLA_VROWS, MLA_CW), f32)
    carry = ()
    for t in range(ahead):
        carry += scores(t, t)
    carry = lax.fori_loop(0, nt // MLA_TRIP - 1, lambda i, c: trip(i * MLA_TRIP, c, False), carry + (m0, a0) * nch)
    trip(nt - MLA_TRIP, carry, True)


def _mla_attention(mqt, mk, mvt):
    B, L, _ = mk.shape
    npair = MLA_HEADS // 2
    nq, nk = L // MLA_TQ, L // MLA_TK
    assert (nq * nk) % MLA_TRIP == 0 and MLA_TRIP % MLA_SLOTS == 0
    mode = dict(pipeline_mode=pl.Buffered(1)) if L * 2 * MLA_PAD * 2 > 4 * 1024 * 1024 else {}
    return pl.pallas_call(
        functools.partial(_mla_body, nq=nq, nk=nk),
        grid=(B, npair),
        in_specs=[pl.BlockSpec((None, 2 * MLA_PAD, L), lambda b, p: (b, p, 0), **mode),
                  pl.BlockSpec((None, L, 2 * MLA_PAD), lambda b, p: (b, 0, p), **mode),
                  pl.BlockSpec((None, 2 * MLA_V, L), lambda b, p: (b, p, 0), **mode)],
        out_specs=pl.BlockSpec((None, 2 * MLA_V, L), lambda b, p: (b, p, 0)),
        out_shape=jax.ShapeDtypeStruct((B, MLA_HEADS * MLA_V, L), bf16),
        scratch_shapes=[pltpu.VMEM((MLA_SLOTS, 2 * MLA_TQ // MLA_CW, MLA_TK, MLA_CW), f32)],
        compiler_params=_cparams(("parallel", "parallel")),
        name="mla_attn",
    )(mqt, mk, mvt)


def _toeplitz_body(g_ref, pf_ref, o_ref):
    kern = jnp.dot(g_ref[...], pf_ref[...], precision=lax.Precision.HIGHEST, preferred_element_type=f32)
    bits = lax.bitcast_convert_type(kern.astype(bf16).astype(f32), jnp.int32)
    for cp in range(SSM_GC):
        for c in range(0, SSM_GC, 2):
            r = cp * SSM_GC + c
            packed = lax.shift_right_logical(bits[r:r + 1, :], 16) | bits[r + 1:r + 2, :]
            rowb = jnp.broadcast_to(packed, (LANES, 2 * LANES))
            toe = pltpu.roll(rowb, LANES + 1, 1, stride=1, stride_axis=0)[:, :LANES]
            even = lax.bitcast_convert_type(lax.shift_left(toe, 16), f32)
            odd = lax.bitcast_convert_type(toe & jnp.int32(-65536), f32)
            o_ref[cp * LANES:(cp + 1) * LANES, c * LANES:(c + 1) * LANES] = even.astype(bf16)
            o_ref[cp * LANES:(cp + 1) * LANES, (c + 1) * LANES:(c + 2) * LANES] = odd.astype(bf16)


def _toeplitz_gen(gm, pf):
    dg = pf.shape[0]
    n = SSM_GC * LANES
    k = 4 * SSM_STATE
    return pl.pallas_call(
        _toeplitz_body,
        grid=(dg,),
        in_specs=[pl.BlockSpec((None, SSM_GC * SSM_GC, k), lambda g: (g, 0, 0)),
                  pl.BlockSpec((None, k, 2 * LANES), lambda g: (g, 0, 0))],
        out_specs=pl.BlockSpec((None, n, n), lambda g: (g, 0, 0)),
        out_shape=jax.ShapeDtypeStruct((dg, n, n), bf16),
        compiler_params=_cparams(("parallel",)),
        name="s5_toeplitz",
    )(gm, pf)


def _ssm_body(*refs, segs):
    ns = len(segs)
    u_refs, (t_ref, wv_ref, wc_ref, a_ref, d_ref) = refs[:ns], refs[ns:ns + 5]
    o_refs = refs[ns + 5:2 * ns + 5]
    vre, vim, xfre, xfim, xbre, xbim = refs[2 * ns + 5:]
    u32 = jnp.concatenate([jnp.concatenate([u_ref[:, c, :] for c in range(SSM_GC)], axis=1) for u_ref in u_refs],
                          axis=0)
    m = u32.shape[0]
    ub = u32.astype(bf16)
    y = jnp.dot(ub, t_ref[...], preferred_element_type=f32)
    v = jnp.dot(ub, wv_ref[...], preferred_element_type=f32)
    vre[...] = v[:, :LANES]
    vim[...] = v[:, LANES:]
    are = a_ref[0:1, :]
    aim = a_ref[1:2, :]
    row0 = 0
    for nb, nj in segs:
        is_fwd = lax.broadcasted_iota(jnp.int32, (nb, LANES), 1) < SSM_STATE

        def rows(j, nb=nb, nj=nj, row0=row0):
            return pl.ds(row0 + j, nb, stride=nj) if nb > 1 else pl.ds(row0 + j, 1)

        def step(k, carry, nj=nj, rows=rows, is_fwd=is_fwd):
            xr, xi = carry
            rf, rb = rows(k), rows(nj - 1 - k)
            xfre[rf, :] = xr
            xfim[rf, :] = xi
            xbre[rb, :] = xr
            xbim[rb, :] = xi
            vr = jnp.where(is_fwd, vre[rf, :], vre[rb, :])
            vi = jnp.where(is_fwd, vim[rf, :], vim[rb, :])
            return are * xr - aim * xi + vr, are * xi + aim * xr + vi

        z = jnp.zeros((nb, LANES), f32)
        lax.fori_loop(0, nj, step, (z, z))
        row0 += nb * nj
    fwd_m = lax.broadcasted_iota(jnp.int32, (m, LANES), 1) < SSM_STATE
    xin = jnp.concatenate([jnp.where(fwd_m, xfre[...], xbre[...]),
                           jnp.where(fwd_m, xfim[...], xbim[...])], axis=1).astype(bf16)
    y = y + jnp.dot(xin, wc_ref[...], preferred_element_type=f32) + u32 * d_ref[...]
    row0 = 0
    for (nb, nj), o_ref in zip(segs, o_refs):
        for c in range(SSM_GC):
            o_ref[:, c, :] = y[row0:row0 + nb * nj, c * LANES:(c + 1) * LANES]
        row0 += nb * nj


def _ssm(uts, layer, S, segs):
    n = SSM_GC * LANES
    ms = [nb * nj for nb, nj in segs]
    gspec = lambda tail: pl.BlockSpec((None,) + tail, lambda g: (layer * SSM_GROUPS + g,) + (0,) * len(tail))
    io = [pl.BlockSpec((m, None, SSM_GC, LANES), lambda g: (0, g, 0, 0)) for m in ms]
    return pl.pallas_call(
        functools.partial(_ssm_body, segs=tuple(segs)),
        grid=(SSM_GROUPS,),
        in_specs=io + [gspec((n, n)), gspec((n, 4 * SSM_STATE)), gspec((4 * SSM_STATE, n)),
                       gspec((2, LANES)), gspec((1, n))],
        out_specs=io,
        out_shape=[jax.ShapeDtypeStruct((m, SSM_GROUPS, SSM_GC, LANES), f32) for m in ms],
        scratch_shapes=[pltpu.VMEM((sum(ms), LANES), f32)] * 6,
        compiler_params=_cparams(("parallel",)),
        name="s5_conv",
    )(*uts, S["toe"], S["wv"], S["wc"], S["a"], S["d"])


def _ssm_tables(a_re, a_im, b_re, b_im, c_re, c_im, log_dt, ssm_d):
    depth = a_re.shape[0]
    dg = depth * SSM_GROUPS
    lam = lax.complex(a_re.astype(f32), a_im.astype(f32))
    dt = jnp.exp(log_dt.astype(f32))[..., None]
    lam_dt = lam * dt
    lam_bar = jnp.exp(lam_dt)
    b_bar = ((lam_bar - 1.0) / lam)[..., None] * lax.complex(b_re.astype(f32), b_im.astype(f32))
    cmat = lax.complex(c_re.astype(f32), c_im.astype(f32))
    steps = jnp.arange(LANES + 1, dtype=f32)
    pw = jnp.exp(lam_dt[..., None] * steps)
    pwf, pwb = pw[:, 0], pw[:, 1]

    e = np.arange(2 * LANES)
    f_ok = (e >= LANES - 1) & (e <= 2 * LANES - 2)
    b_ok = e <= LANES - 1
    pf_f = jnp.where(f_ok, pwf[..., np.clip(e - (LANES - 1), 0, LANES - 1)], 0.0)
    pf_b = jnp.where(b_ok, pwb[..., np.clip(LANES - 1 - e, 0, LANES - 1)], 0.0)
    pf = jnp.stack([pf_f.real, pf_f.imag, pf_b.real, pf_b.imag], axis=2)
    pf = pf.reshape(dg, 4 * SSM_STATE, 2 * LANES)

    bbf, bbb = b_bar[:, 0], b_bar[:, 1]
    cf, cb = cmat[:, 0], cmat[:, 1]
    gf = cf[:, :, None, :, :] * jnp.swapaxes(bbf, -1, -2)[:, :, :, None, :]
    gb = cb[:, :, None, :, :] * jnp.swapaxes(bbb, -1, -2)[:, :, :, None, :]
    gm = jnp.concatenate([gf.real, -gf.imag, gb.real, -gb.imag], axis=-1)
    gm = gm.reshape(dg, SSM_GC * SSM_GC, 4 * SSM_STATE)

    s_idx = np.arange(LANES)
    pvf = jnp.swapaxes(pwf[..., LANES - 1 - s_idx], -1, -2)
    pvb = jnp.swapaxes(pwb[..., s_idx], -1, -2)
    pv1 = jnp.concatenate([pvf.real, pvb.real, pvf.imag, pvb.imag], axis=-1).reshape(dg, LANES, 4 * SSM_STATE)
    pv2 = jnp.concatenate([pvf.imag, pvb.imag, pvf.real, pvb.real], axis=-1).reshape(dg, LANES, 4 * SSM_STATE)
    bft, bbt = jnp.swapaxes(bbf, -1, -2), jnp.swapaxes(bbb, -1, -2)
    bv = jnp.stack([jnp.concatenate([bft.real, bbt.real, bft.real, bbt.real], axis=-1),
                    jnp.concatenate([-bft.imag, -bbt.imag, bft.imag, bbt.imag], axis=-1)], axis=3)
    bv = bv.reshape(dg, SSM_GC, 2, 4 * SSM_STATE)
    t_idx = np.arange(LANES)
    pcf, pcb = pwf[..., t_idx + 1], pwb[..., LANES - t_idx]
    pc = jnp.stack([pcf.real, pcf.imag, pcb.real, pcb.imag], axis=2).reshape(dg, 4, SSM_STATE, LANES)
    cft, cbt = jnp.swapaxes(cf, -1, -2), jnp.swapaxes(cb, -1, -2)
    ct = jnp.stack([cft.real, cft.imag, cbt.real, cbt.imag], axis=2).reshape(dg, 4, SSM_STATE, SSM_GC)

    a128 = jnp.concatenate([pwf[..., LANES], pwb[..., LANES]], axis=-1)
    a = jnp.stack([a128.real, a128.imag], axis=2).reshape(dg, 2, LANES)
    d = jnp.repeat(ssm_d.astype(f32).reshape(dg, SSM_GC), LANES, axis=-1).reshape(dg, 1, SSM_GC * LANES)
    return dict(pf=pf, gm=gm, pv1=pv1, pv2=pv2, bv=bv, pc=pc, ct=ct, a=a, d=d)


def _state_tables_body(pv1_ref, pv2_ref, bv_ref, pc_ref, ct_ref, wv_ref, wc_ref):
    pv1, pv2 = pv1_ref[...], pv2_ref[...]
    for cp in range(SSM_GC):
        wv_ref[cp * LANES:(cp + 1) * LANES, :] = (pv1 * bv_ref[cp, 0:1, :] + pv2 * bv_ref[cp, 1:2, :]).astype(bf16)
    pfr, pfi, pbr, pbi = pc_ref[0], pc_ref[1], pc_ref[2], pc_ref[3]
    cfr, cfi, cbr, cbi = ct_ref[0], ct_ref[1], ct_ref[2], ct_ref[3]
    for c in range(SSM_GC):
        col = slice(c, c + 1)
        tile = jnp.concatenate([cfr[:, col] * pfr - cfi[:, col] * pfi,
                                cbr[:, col] * pbr - cbi[:, col] * pbi,
                                -(cfr[:, col] * pfi + cfi[:, col] * pfr),
                                -(cbr[:, col] * pbi + cbi[:, col] * pbr)], axis=0)
        wc_ref[:, c * LANES:(c + 1) * LANES] = tile.astype(bf16)


def _state_tables(S):
    dg = S["pf"].shape[0]
    n, k = SSM_GC * LANES, 4 * SSM_STATE
    g3 = lambda a, b: pl.BlockSpec((None, a, b), lambda g: (g, 0, 0))
    g4 = lambda a, b, c: pl.BlockSpec((None, a, b, c), lambda g: (g, 0, 0, 0))
    return pl.pallas_call(
        _state_tables_body,
        grid=(dg,),
        in_specs=[g3(LANES, k), g3(LANES, k), g4(SSM_GC, 2, k), g4(4, SSM_STATE, LANES), g4(4, SSM_STATE, SSM_GC)],
        out_specs=(g3(n, k), g3(k, n)),
        out_shape=(jax.ShapeDtypeStruct((dg, n, k), bf16), jax.ShapeDtypeStruct((dg, k, n), bf16)),
        compiler_params=_cparams(("parallel",)),
        name="s5_state_tables",
    )(S["pv1"], S["pv2"], S["bv"], S["pc"], S["ct"])


def _mixffn_body(x_ref, mod_ref, yna_ref, ymlat_ref, yst_ref, gluw_ref, glub_ref, wo_ref, g2_ref,
                 wg_ref, wu_ref, wd_ref, o_ref, act_ref):
    ys = jnp.concatenate([yst_ref[c].reshape(SSM_WIDTH, LANES).T for c in range(yst_ref.shape[0])],
                         axis=0)
    ys = 0.5 * ys * (1.0 + jnp.tanh(math.sqrt(2.0 / math.pi) * (ys + 0.044715 * (ys * ys * ys))))
    gate = jax.nn.sigmoid(jnp.dot(ys.astype(bf16), gluw_ref[...], preferred_element_type=f32) + glub_ref[...])
    yssm = (ys * gate).astype(bf16)
    o1 = NA_WIDTH
    o2 = NA_WIDTH + MLA_HEADS * MLA_V
    mix = (jnp.dot(yna_ref[...], wo_ref[0:o1, :], preferred_element_type=f32)
           + lax.dot_general(ymlat_ref[...], wo_ref[o1:o2, :], _TN, preferred_element_type=f32)
           + jnp.dot(yssm, wo_ref[o2:, :], preferred_element_type=f32))
    x1 = x_ref[...] + mod_ref[2:3, :] * mix
    o_ref[...] = x1
    ms = jnp.mean(x1 * x1, axis=-1, keepdims=True)
    h2 = ((x1 * lax.rsqrt(ms + RMS_EPS) * g2_ref[...]) * (1.0 + mod_ref[4:5, :]) + mod_ref[3:4, :]).astype(bf16)

    th = 256
    for c in range(FFN_HIDDEN // th):
        cs = slice(c * th, (c + 1) * th)
        g = jnp.dot(h2, wg_ref[:, cs], preferred_element_type=f32)
        u = jnp.dot(h2, wu_ref[:, cs], preferred_element_type=f32)
        act_ref[:, cs] = (g * jax.nn.sigmoid(g) * u).astype(bf16)
    ffn = jnp.dot(act_ref[...], wd_ref[...], preferred_element_type=f32)
    o_ref[...] = o_ref[...] + mod_ref[5:6, :] * ffn


def _mixffn(x, mod, yna, ymlat, yst, layer, P):
    B, L, D = x.shape
    tm = TOKEN_TILE
    nt = L // tm
    cpt = tm // LANES
    tok = lambda w: pl.BlockSpec((None, tm, w), lambda b, i: (b, i, 0))

    def wspec(name):
        tail = P[name].shape[1:]
        return pl.BlockSpec((None,) + tail, lambda b, i: (layer,) + (0,) * len(tail), pipeline_mode=pl.Buffered(1))

    return pl.pallas_call(
        _mixffn_body,
        grid=(B, nt),
        in_specs=[tok(D), pl.BlockSpec((None, 6, D), lambda b, i: (b, 0, 0)), tok(NA_WIDTH),
                  pl.BlockSpec((None, MLA_HEADS * MLA_V, tm), lambda b, i: (b, 0, i)),
                  pl.BlockSpec((cpt, SSM_GROUPS, SSM_GC, LANES), lambda b, i: (b * nt + i, 0, 0, 0)),
                  wspec("glu_w"), wspec("glu_b"), wspec("w_out"), wspec("g2"),
                  wspec("w_gate"), wspec("w_up"), wspec("w_down")],
        out_specs=tok(D),
        out_shape=jax.ShapeDtypeStruct((B, L, D), f32),
        scratch_shapes=[pltpu.VMEM((tm, FFN_HIDDEN), bf16)],
        compiler_params=_cparams(("parallel", "parallel")),
        name="mix_ffn",
    )(x, mod, yna, ymlat, yst, P["glu_w"], P["glu_b"], P["w_out"], P["g2"], P["w_gate"], P["w_up"], P["w_down"])


def _prep_params(norm1_g, w_in, na_q_g, na_k_g, mla_cq_g, mla_ckv_g, mla_w_uq, mla_w_ukv, mla_qn_g, mla_kn_g,
                 mla_qr_g, mla_kr_g, glu_w, glu_b, w_out, norm2_g, ffn_w_gate, ffn_w_up, ffn_w_down):
    depth = w_in.shape[0]
    o1 = 3 * NA_WIDTH
    o2 = o1 + MLA_Q_LORA
    o3 = o2 + MLA_KV_LORA
    o4 = o3 + MLA_ROPE
    row = lambda a: a.astype(f32)[:, None, :]
    zeros = lambda *s: jnp.zeros(s, f32)
    P = {}
    P["g1"] = row(norm1_g)
    P["g2"] = row(norm2_g)
    P["w_na"] = w_in[:, :, :o1].astype(bf16)
    P["w_c"] = w_in[:, :, o1:o3].astype(bf16)
    P["w_kr"] = jnp.concatenate([zeros(depth, D_MODEL, MLA_NOPE), w_in[:, :, o3:o4],
                                 zeros(depth, D_MODEL, MLA_PAD - MLA_NOPE - MLA_ROPE)], axis=-1).astype(bf16)
    P["w_ut"] = jnp.swapaxes(w_in[:, :, o4:], 1, 2).astype(bf16)
    wq = mla_w_uq.reshape(depth, MLA_Q_LORA, MLA_HEADS, MLA_NOPE + MLA_ROPE)
    wq = jnp.concatenate([wq, zeros(depth, MLA_Q_LORA, MLA_HEADS, MLA_PAD - MLA_NOPE - MLA_ROPE)], axis=-1)
    P["w_uq"] = wq.reshape(depth, MLA_Q_LORA, MLA_HEADS * MLA_PAD).astype(bf16)
    wkv = mla_w_ukv.reshape(depth, MLA_KV_LORA, MLA_HEADS, MLA_NOPE + MLA_V)
    wk = jnp.concatenate([wkv[..., :MLA_NOPE], zeros(depth, MLA_KV_LORA, MLA_HEADS, MLA_PAD - MLA_NOPE)], axis=-1)
    P["w_ukv_k"] = wk.reshape(depth, MLA_KV_LORA, MLA_HEADS * MLA_PAD).astype(bf16)
    P["w_ukv_vt"] = jnp.swapaxes(wkv[..., MLA_NOPE:].reshape(depth, MLA_KV_LORA, MLA_HEADS * MLA_V), 1, 2).astype(bf16)
    P["gq_na"] = row(jnp.tile(na_q_g, (1, NA_HEADS))) * (NA_HEAD_DIM ** -0.5)
    P["gk_na"] = row(jnp.tile(na_k_g, (1, NA_HEADS)))
    P["g_cq"] = row(mla_cq_g)
    P["g_ckv"] = row(mla_ckv_g)
    scale = (MLA_NOPE + MLA_ROPE) ** -0.5 * math.log2(math.e)
    pad = MLA_PAD - MLA_NOPE - MLA_ROPE
    gq = jnp.concatenate([mla_qn_g, mla_qr_g, zeros(depth, pad)], axis=-1) * scale
    P["gq"] = row(jnp.tile(gq, (1, MLA_HEADS)))
    gk = jnp.concatenate([mla_kn_g, zeros(depth, MLA_PAD - MLA_NOPE)], axis=-1)
    P["gk"] = row(jnp.tile(gk, (1, MLA_HEADS)))
    P["gkr"] = row(jnp.concatenate([zeros(depth, MLA_NOPE), mla_kr_g, zeros(depth, pad)], axis=-1))
    invq = np.concatenate([np.full(MLA_NOPE, 1.0 / MLA_NOPE), np.full(MLA_ROPE, 1.0 / MLA_ROPE), np.zeros(pad)])
    invk = np.concatenate([np.full(MLA_NOPE, 1.0 / MLA_NOPE), np.zeros(MLA_PAD - MLA_NOPE)])
    P["invq"] = jnp.asarray(np.tile(invq, MLA_HEADS)[None, :], f32)
    P["invk"] = jnp.asarray(np.tile(invk, MLA_HEADS)[None, :], f32)
    lane = np.arange(NA_WIDTH)
    P["G_na"] = jnp.asarray((lane[:, None] // NA_HEAD_DIM) == (lane[None, :] // NA_HEAD_DIM), bf16)
    lane = np.arange(2 * MLA_PAD)
    grp = np.where(lane % MLA_PAD < MLA_NOPE, 0, np.where(lane % MLA_PAD < MLA_NOPE + MLA_ROPE, 1, 2))
    same = (lane[:, None] // MLA_PAD == lane[None, :] // MLA_PAD) & (grp[:, None] == grp[None, :]) & (grp[:, None] < 2)
    P["G_qm"] = jnp.asarray(same, bf16)
    P["glu_w"] = glu_w.astype(bf16)
    P["glu_b"] = row(glu_b)
    P["w_out"] = w_out.astype(bf16)
    P["w_gate"] = ffn_w_gate.astype(bf16)
    P["w_up"] = ffn_w_up.astype(bf16)
    P["w_down"] = ffn_w_down.astype(bf16)
    return P


def _rope_tables(length):
    inv = 1.0 / (ROPE_BASE ** (jnp.arange(0, MLA_ROPE, 2, dtype=f32) / MLA_ROPE))
    ang = jnp.arange(length, dtype=f32)[:, None] * inv[None, :]
    cos, sin = jnp.cos(ang), jnp.sin(ang)
    half = MLA_ROPE // 2
    z = lambda w: jnp.zeros((length, w), f32)
    pad = MLA_PAD - MLA_NOPE - MLA_ROPE
    cos_t = jnp.concatenate([jnp.ones((length, MLA_NOPE), f32), cos, cos, z(pad)], axis=-1)
    sin_lo = jnp.concatenate([z(MLA_NOPE), -sin, z(half), z(pad)], axis=-1)
    sin_hi = jnp.concatenate([z(MLA_NOPE), z(half), sin, z(pad)], axis=-1)
    return cos_t, sin_lo, sin_hi


def _trunks(xs, mods, P, S, bias):
    xs = list(xs)
    ropes = [_rope_tables(x.shape[1]) for x in xs]
    segs = [(x.shape[0], x.shape[1] // LANES) for x in xs]
    for layer in range(len(mods[0])):
        heads, uts = [], []
        for x, mod, rope in zip(xs, mods, ropes):
            naq, nak, nav, mqt, mk, mvt, ut = _inproj(x, mod[layer], layer, P, rope)
            heads.append((_na_attention(naq, nak, nav, bias, layer), _mla_attention(mqt, mk, mvt)))
            uts.append(ut)
        ysts = _ssm(uts, layer, S, segs)
        xs = [_mixffn(x, mod[layer], yna, ymla, yst, layer, P)
              for x, mod, (yna, ymla), yst in zip(xs, mods, heads, ysts)]
    return xs


def kernel(x_prompt, x_sample, c_prompt, c_sample, ada_w, ada_b, norm1_g, w_in, na_q_g, na_k_g, na_rpb, mla_cq_g,
           mla_ckv_g, mla_w_uq, mla_w_ukv, mla_qn_g, mla_kn_g, mla_qr_g, mla_kr_g, ssm_a_re, ssm_a_im, ssm_b_re,
           ssm_b_im, ssm_c_re, ssm_c_im, ssm_log_dt, ssm_d, glu_w, glu_b, w_out, norm2_g, ffn_w_gate, ffn_w_up,
           ffn_w_down):
    depth = w_in.shape[0]
    nbp, nbs = c_prompt.shape[0], c_sample.shape[0]
    rows = -(-(nbp + nbs) // 8) * 8
    c_all = jnp.concatenate([c_prompt, c_sample, jnp.zeros((rows - nbp - nbs, D_MODEL), f32)], axis=0)
    mod = _modulation(c_all, ada_w, ada_b).reshape(depth, rows, 6, D_MODEL)
    mods_p = [mod[l, :nbp] for l in range(depth)]
    mods_s = [mod[l, nbp:nbp + nbs] for l in range(depth)]

    P = _prep_params(norm1_g, w_in, na_q_g, na_k_g, mla_cq_g, mla_ckv_g, mla_w_uq, mla_w_ukv, mla_qn_g, mla_kn_g,
                     mla_qr_g, mla_kr_g, glu_w, glu_b, w_out, norm2_g, ffn_w_gate, ffn_w_up, ffn_w_down)
    S = _ssm_tables(ssm_a_re, ssm_a_im, ssm_b_re, ssm_b_im, ssm_c_re, ssm_c_im, ssm_log_dt, ssm_d)
    S["toe"] = _toeplitz_gen(S["gm"], S["pf"])
    S["wv"], S["wc"] = _state_tables(S)
    bias = _na_bias_tables(na_rpb)

    y_prompt, y_sample = _trunks((x_prompt, x_sample), (mods_p, mods_s), P, S, bias)
    return (y_prompt, y_sample)
```

```python
import functools
import math

import numpy as np
import jax
import jax.numpy as jnp
from jax import lax
from jax.experimental import pallas as pl
from jax.experimental.pallas import tpu as pltpu

f32 = jnp.float32
bf16 = jnp.bfloat16

D_MODEL = 1024
GRID_W = 64
NA_HEADS = 4
NA_HEAD_DIM = 64
NA_WIDTH = NA_HEADS * NA_HEAD_DIM
NA_ROWS = 8
NA_COLS = 16
MLA_HEADS = 8
MLA_NOPE = 64
MLA_ROPE = 32
MLA_V = 64
MLA_Q_LORA = 384
MLA_KV_LORA = 256
ROPE_BASE = 10000.0
SSM_GROUPS = 16
SSM_GC = 16
SSM_STATE = 64
SSM_WIDTH = SSM_GROUPS * SSM_GC
FFN_HIDDEN = 2816
RMS_EPS = 1e-6
NEG_INF = -1e30

LANES = 128
MLA_PAD = 128
NA_QCOLS = 16
NA_KCOLS = 2 * NA_COLS
NA_QROWS = 4
NA_SUB = 4
NA_WROWS = NA_QROWS + NA_ROWS
NA_BIAS_LANE0 = 200
TOKEN_TILE = 512
MLA_CW = 256
MLA_TQ = MLA_CW
MLA_TK = 1024
MLA_KC = 256
MLA_AHEAD = 2
MLA_SLOTS = 2 * MLA_AHEAD
MLA_TRIP = 8
MLA_VROWS = MLA_V + 16
VMEM_LIMIT = 48 * 1024 * 1024

_NT = (((1,), (1,)), ((), ()))
_TN = (((0,), (0,)), ((), ()))


def _cparams(sem):
    return pltpu.CompilerParams(dimension_semantics=sem, vmem_limit_bytes=VMEM_LIMIT)


def _layer_spec(tail, layer):
    n = len(tail)
    return pl.BlockSpec((None,) + tuple(tail), lambda *_: (layer,) + (0,) * n)


def _mod_body(c_ref, w_ref, b_ref, o_ref):
    c = c_ref[...]
    s = c * jax.nn.sigmoid(c)
    o_ref[...] = jnp.dot(s, w_ref[...], precision=lax.Precision.HIGHEST, preferred_element_type=f32) + b_ref[...]


def _modulation(c_all, ada_w, ada_b):
    depth, d, n = ada_w.shape
    tn = 1536
    rows = c_all.shape[0]
    return pl.pallas_call(
        _mod_body,
        grid=(depth, n // tn),
        in_specs=[
            pl.BlockSpec((rows, d), lambda l, j: (0, 0)),
            pl.BlockSpec((None, d, tn), lambda l, j: (l, 0, j)),
            pl.BlockSpec((None, 1, tn), lambda l, j: (l, 0, j)),
        ],
        out_specs=pl.BlockSpec((None, rows, tn), lambda l, j: (l, 0, j)),
        out_shape=jax.ShapeDtypeStruct((depth, rows, n), f32),
        compiler_params=_cparams(("arbitrary", "arbitrary")),
        name="adaln_mod",
    )(c_all, ada_w, ada_b.reshape(depth, 1, n))


def _rope(x, cos, sin_lo, sin_hi):
    return x * cos + pltpu.roll(x, LANES - 16, 1) * sin_lo + pltpu.roll(x, 16, 1) * sin_hi


def _inproj_body(x_ref, mod_ref, g1_ref, wna_ref, wc_ref, wut_ref, wuq_ref, wukvk_ref, wukvvt_ref,
                 gqna_ref, gkna_ref, gcq_ref, gckv_ref, gq_ref, gk_ref, gkr_ref, invq_ref, invk_ref,
                 gna_ref, gqm_ref, cos_ref, sinlo_ref, sinhi_ref,
                 naq_ref, nak_ref, nav_ref, mqt_ref, mk_ref, mvt_ref, ut_ref):
    x = x_ref[...]
    shift1 = mod_ref[0:1, :]
    scale1 = mod_ref[1:2, :]
    ms = jnp.mean(x * x, axis=-1, keepdims=True)
    h = (x * lax.rsqrt(ms + RMS_EPS) * g1_ref[...]) * (1.0 + scale1) + shift1
    hb = h.astype(bf16)

    z = jnp.dot(hb, wna_ref[...], preferred_element_type=f32)
    q = z[:, :NA_WIDTH]
    k = z[:, NA_WIDTH:2 * NA_WIDTH]
    gna = gna_ref[...]
    ssq = jnp.dot((q * q).astype(bf16), gna, preferred_element_type=f32) * (1.0 / NA_HEAD_DIM)
    ssk = jnp.dot((k * k).astype(bf16), gna, preferred_element_type=f32) * (1.0 / NA_HEAD_DIM)
    naq_ref[...] = (q * lax.rsqrt(ssq + RMS_EPS) * gqna_ref[...]).astype(bf16)
    nak_ref[...] = (k * lax.rsqrt(ssk + RMS_EPS) * gkna_ref[...]).astype(bf16)
    nav_ref[...] = z[:, 2 * NA_WIDTH:].astype(bf16)

    zc = jnp.dot(hb, wc_ref[...], preferred_element_type=f32)
    cq = zc[:, :MLA_Q_LORA]
    ckv = zc[:, MLA_Q_LORA:MLA_Q_LORA + MLA_KV_LORA]
    cqn = (cq * lax.rsqrt(jnp.mean(cq * cq, axis=-1, keepdims=True) + RMS_EPS) * gcq_ref[...]).astype(bf16)
    ckvn = (ckv * lax.rsqrt(jnp.mean(ckv * ckv, axis=-1, keepdims=True) + RMS_EPS) * gckv_ref[...]).astype(bf16)

    cos = cos_ref[...]
    sin_lo = sinlo_ref[...]
    sin_hi = sinhi_ref[...]
    gqm = gqm_ref[...]

    kr = zc[:, MLA_Q_LORA + MLA_KV_LORA:]
    kr_ms = jnp.sum(kr * kr, axis=-1, keepdims=True) * (1.0 / MLA_ROPE)
    kr = _rope(kr * lax.rsqrt(kr_ms + RMS_EPS) * gkr_ref[...], cos, sin_lo, sin_hi)

    qraw = jnp.dot(cqn, wuq_ref[...], preferred_element_type=f32)
    kraw = jnp.dot(ckvn, wukvk_ref[...], preferred_element_type=f32)
    for p in range(MLA_HEADS // 2):
        sl = slice(2 * p * MLA_PAD, 2 * (p + 1) * MLA_PAD)
        qs = qraw[:, sl]
        ss = jnp.dot((qs * qs).astype(bf16), gqm, preferred_element_type=f32) * invq_ref[:, sl]
        qn = qs * lax.rsqrt(ss + RMS_EPS) * gq_ref[:, sl]
        ks = kraw[:, sl]
        ss = jnp.dot((ks * ks).astype(bf16), gqm, preferred_element_type=f32) * invk_ref[:, sl]
        kn = ks * lax.rsqrt(ss + RMS_EPS) * gk_ref[:, sl]
        for hh in range(2):
            lo = (2 * p + hh) * MLA_PAD
            piece = qn[:, hh * MLA_PAD:(hh + 1) * MLA_PAD]
            mqt_ref[lo:lo + MLA_PAD, :] = _rope(piece, cos, sin_lo, sin_hi).T.astype(bf16)
            mk_ref[:, lo:lo + MLA_PAD] = (kn[:, hh * MLA_PAD:(hh + 1) * MLA_PAD] + kr).astype(bf16)
    mvt_ref[...] = lax.dot_general(wukvvt_ref[...], ckvn, _NT, preferred_element_type=f32).astype(bf16)

    ut = lax.dot_general(wut_ref[...], hb, _NT, preferred_element_type=f32)
    for c in range(ut_ref.shape[0]):
        ut_ref[c] = ut[:, c * LANES:(c + 1) * LANES].reshape(SSM_GROUPS, SSM_GC, LANES)


def _inproj(x, mod, layer, P, rope_tabs):
    B, L, D = x.shape
    tm = TOKEN_TILE
    nt = L // tm
    cpt = tm // LANES
    tok = lambda w: pl.BlockSpec((None, tm, w), lambda b, i: (b, i, 0))
    pos = pl.BlockSpec((tm, LANES), lambda b, i: (i, 0))
    const2 = lambda a: pl.BlockSpec(a.shape, lambda b, i: (0, 0))
    lw = lambda name: _layer_spec(P[name].shape[1:], layer)
    names = ["g1", "w_na", "w_c", "w_ut", "w_uq", "w_ukv_k", "w_ukv_vt",
             "gq_na", "gk_na", "g_cq", "g_ckv", "gq", "gk", "gkr"]
    consts = ["invq", "invk", "G_na", "G_qm"]
    in_specs = ([tok(D), pl.BlockSpec((None, 6, D), lambda b, i: (b, 0, 0))]
                + [lw(n) for n in names] + [const2(P[n]) for n in consts] + [pos, pos, pos])
    out_shapes = (
        jax.ShapeDtypeStruct((B, L, NA_WIDTH), bf16),
        jax.ShapeDtypeStruct((B, L, NA_WIDTH), bf16),
        jax.ShapeDtypeStruct((B, L, NA_WIDTH), bf16),
        jax.ShapeDtypeStruct((B, MLA_HEADS * MLA_PAD, L), bf16),
        jax.ShapeDtypeStruct((B, L, MLA_HEADS * MLA_PAD), bf16),
        jax.ShapeDtypeStruct((B, MLA_HEADS * MLA_V, L), bf16),
        jax.ShapeDtypeStruct((B * L // LANES, SSM_GROUPS, SSM_GC, LANES), f32),
    )
    chan = lambda w: pl.BlockSpec((None, w, tm), lambda b, i: (b, 0, i))
    out_specs = (tok(NA_WIDTH), tok(NA_WIDTH), tok(NA_WIDTH), chan(MLA_HEADS * MLA_PAD), tok(MLA_HEADS * MLA_PAD),
                 chan(MLA_HEADS * MLA_V),
                 pl.BlockSpec((cpt, SSM_GROUPS, SSM_GC, LANES), lambda b, i: (b * nt + i, 0, 0, 0)))
    return pl.pallas_call(
        _inproj_body,
        grid=(B, nt),
        in_specs=in_specs,
        out_specs=out_specs,
        out_shape=out_shapes,
        compiler_params=_cparams(("parallel", "parallel")),
        name="in_proj",
    )(x, mod, *[P[n] for n in names], *[P[n] for n in consts], *rope_tabs)


def _na_key_col0(cb):
    return min(max(cb * NA_QCOLS - NA_COLS // 2, 0), GRID_W - NA_KCOLS)


def _na_body(q_ref, k_ref, v_ref, b_ref, o_ref, *, rows):
    nblk = rows // NA_QROWS
    lane = lax.broadcasted_iota(jnp.int32, (1, NA_WIDTH), 1)
    hms = [(lane >= h * NA_HEAD_DIM) & (lane < (h + 1) * NA_HEAD_DIM) for h in range(NA_HEADS)]
    nq = NA_QROWS * NA_QCOLS
    for sb in range(NA_SUB):
        blk = pl.program_id(1) * NA_SUB + sb
        variant = jnp.where(blk == 0, 0, jnp.where(blk == nblk - 1, 2, 1))
        ws = jnp.clip(NA_QROWS * blk - NA_ROWS // 2, 0, rows - NA_WROWS)
        base = pl.multiple_of(ws * GRID_W, GRID_W)
        q0 = sb * NA_QROWS * GRID_W
        for cb in range(GRID_W // NA_QCOLS):
            kc0 = _na_key_col0(cb)
            kw = jnp.concatenate([k_ref[pl.ds(base + kl * GRID_W + kc0, NA_KCOLS), :] for kl in range(NA_WROWS)],
                                 axis=0)
            vw = jnp.concatenate([v_ref[pl.ds(base + kl * GRID_W + kc0, NA_KCOLS), :] for kl in range(NA_WROWS)],
                                 axis=0)
            qrows = [slice(q0 + rl * GRID_W + cb * NA_QCOLS, q0 + rl * GRID_W + (cb + 1) * NA_QCOLS)
                     for rl in range(NA_QROWS)]
            qb = jnp.concatenate([q_ref[r, :] for r in qrows], axis=0)
            qs = jnp.concatenate([jnp.where(hm, qb, jnp.zeros_like(qb)) for hm in hms], axis=0)
            s = lax.dot_general(qs, kw, _NT, preferred_element_type=f32) + b_ref[variant, cb].reshape(NA_HEADS * nq, -1)
            m = jnp.max(s, axis=-1, keepdims=True)
            p = jnp.exp2(s - m)
            l = jnp.sum(p, axis=-1, keepdims=True)
            o = jnp.dot(p.astype(bf16), vw, preferred_element_type=f32) / l
            ob = o[:nq]
            for h in range(1, NA_HEADS):
                ob = jnp.where(hms[h], o[h * nq:(h + 1) * nq], ob)
            for rl, r in enumerate(qrows):
                o_ref[r, :] = ob[rl * NA_QCOLS:(rl + 1) * NA_QCOLS].astype(bf16)


def _na_attention(q, k, v, bias, layer):
    B, L, W = q.shape
    rows = L // GRID_W
    tq = NA_SUB * NA_QROWS * GRID_W
    ncb = GRID_W // NA_QCOLS
    mode = dict(pipeline_mode=pl.Buffered(1)) if L * W * 2 > 4 * 1024 * 1024 else {}
    full = pl.BlockSpec((None, L, W), lambda b, i: (b, 0, 0), **mode)
    return pl.pallas_call(
        functools.partial(_na_body, rows=rows),
        grid=(B, L // tq),
        in_specs=[pl.BlockSpec((None, tq, W), lambda b, i: (b, i, 0)), full, full,
                  pl.BlockSpec((None, 3, ncb, NA_HEADS, NA_QROWS * NA_QCOLS, NA_WROWS * NA_KCOLS),
                               lambda b, i: (layer, 0, 0, 0, 0, 0), pipeline_mode=pl.Buffered(1))],
        out_specs=pl.BlockSpec((None, tq, W), lambda b, i: (b, i, 0)),
        out_shape=jax.ShapeDtypeStruct((B, L, W), bf16),
        compiler_params=_cparams(("parallel", "arbitrary")),
        name="na_attn",
    )(q, k, v, bias)


def _na_bias_body(r_ref, o_ref):
    r = r_ref[...]
    qc = lax.broadcasted_iota(jnp.int32, (NA_QCOLS, LANES), 0)
    lane = lax.broadcasted_iota(jnp.int32, (NA_QCOLS, LANES), 1)
    piece = lane // NA_KCOLS
    neg = jnp.full((NA_QCOLS, LANES), NEG_INF, f32)
    npc = LANES // NA_KCOLS
    for cb in range(GRID_W // NA_QCOLS):
        kc0 = _na_key_col0(cb)
        qcol = cb * NA_QCOLS + qc
        kcol = kc0 + lane - piece * NA_KCOLS
        cs = jnp.clip(qcol - NA_COLS // 2, 0, GRID_W - NA_COLS)
        cvalid = (kcol >= cs) & (kcol < cs + NA_COLS)
        tiles = []
        for dr in range(2 * NA_ROWS - 1):
            rowb = jnp.broadcast_to(r[dr:dr + 1, :], (NA_QCOLS, 2 * LANES))
            t = neg
            for k in range(npc):
                base = 2 * LANES - (NA_COLS - 1) - NA_BIAS_LANE0 + cb * NA_QCOLS - kc0 + k * NA_KCOLS
                rolled = pltpu.roll(rowb, base, 1, stride=1, stride_axis=0)[:, :LANES]
                t = jnp.where(piece == k, rolled, t)
            tiles.append(jnp.where(cvalid, t, neg))
        for v, (off, lo) in enumerate(((0, 0), (-(NA_ROWS // 2), None), (-NA_ROWS, NA_ROWS // 2))):
            for rl in range(NA_QROWS):
                vregs = []
                for j in range(NA_WROWS // npc):
                    t = neg
                    for k in range(npc):
                        kl = j * npc + k
                        d = off + kl - rl
                        ok = (-(NA_ROWS // 2) <= d < NA_ROWS // 2) if lo is None else (lo <= kl < lo + NA_ROWS)
                        if ok:
                            t = jnp.where(piece == k, tiles[d + NA_ROWS - 1], t)
                    vregs.append(t)
                o_ref[v, cb, rl * NA_QCOLS:(rl + 1) * NA_QCOLS, :] = jnp.concatenate(vregs, axis=1)


def _na_bias_tables(rpb):
    depth, nh, nr, nc = rpb.shape
    rp = jnp.pad(rpb.astype(f32) * math.log2(math.e),
                 ((0, 0), (0, 0), (0, 16 - nr), (NA_BIAS_LANE0, 2 * LANES - NA_BIAS_LANE0 - nc)))
    ncb, nq, nk = GRID_W // NA_QCOLS, NA_QROWS * NA_QCOLS, NA_WROWS * NA_KCOLS
    return pl.pallas_call(
        _na_bias_body,
        grid=(depth, nh),
        in_specs=[pl.BlockSpec((None, None, 16, 2 * LANES), lambda l, h: (l, h, 0, 0))],
        out_specs=pl.BlockSpec((None, 3, ncb, None, nq, nk), lambda l, h: (l, 0, 0, h, 0, 0)),
        out_shape=jax.ShapeDtypeStruct((depth, 3, ncb, nh, nq, nk), f32),
        compiler_params=_cparams(("parallel", "parallel")),
        name="na_bias",
    )(rp)


def _mla_body(qt_ref, k_ref, vt_ref, o_ref, s_scr, *, nq, nk):
    tq, tk = MLA_TQ, MLA_TK
    ones = jnp.ones((MLA_VROWS - MLA_V, MLA_KC), bf16)
    chains = [(h, c) for h in range(2) for c in range(tq // MLA_CW)]
    nch = len(chains)

    def scores(t, slot):
        qoff = pl.multiple_of((t // nk) * tq, tq)
        koff = pl.multiple_of((t % nk) * tk, tk)
        qt = qt_ref[:, pl.ds(qoff, tq)]
        kb = k_ref[pl.ds(koff, tk), :]
        cmax = []
        for n, (h, c) in enumerate(chains):
            s = jnp.dot(kb[:, h * MLA_PAD:(h + 1) * MLA_PAD],
                        qt[h * MLA_PAD:(h + 1) * MLA_PAD, c * MLA_CW:(c + 1) * MLA_CW],
                        preferred_element_type=f32)
            s_scr[slot, n] = s
            cmax.append(jnp.max(s, axis=0, keepdims=True))
        return tuple(cmax)

    nkc = tk // MLA_KC

    def step(t_s, slot_s, t_a, slot_a, cmax, state):
        if t_s is not None:
            qoff_s = pl.multiple_of((t_s // nk) * tq, tq)
            koff_s = pl.multiple_of((t_s % nk) * tk, tk)
            qt = qt_ref[:, pl.ds(qoff_s, tq)]
        j = t_a % nk
        qoff = pl.multiple_of((t_a // nk) * tq, tq)
        koff = pl.multiple_of(j * tk, tk)
        tile_start = j == 0
        mns, als = [], []
        for n in range(nch):
            m = jnp.where(tile_start, -jnp.inf, state[2 * n])
            mns.append(jnp.maximum(m, cmax[n]))
            als.append(jnp.exp2(m - mns[n]))
        parts, cnew = [None] * nch, [None] * nch
        for kc in range(nkc):
            rows = slice(kc * MLA_KC, (kc + 1) * MLA_KC)
            if t_s is not None:
                kb = k_ref[pl.ds(koff_s + kc * MLA_KC, MLA_KC), :]
                for n, (h, c) in enumerate(chains):
                    s = jnp.dot(kb[:, h * MLA_PAD:(h + 1) * MLA_PAD],
                                qt[h * MLA_PAD:(h + 1) * MLA_PAD, c * MLA_CW:(c + 1) * MLA_CW],
                                preferred_element_type=f32)
                    s_scr[slot_s, n, rows, :] = s
                    cm = jnp.max(s, axis=0, keepdims=True)
                    cnew[n] = cm if cnew[n] is None else jnp.maximum(cnew[n], cm)
            vt = vt_ref[:, pl.ds(koff + kc * MLA_KC, MLA_KC)]
            for n, (h, c) in enumerate(chains):
                p = jnp.exp2(s_scr[slot_a, n, rows, :] - mns[n]).astype(bf16)
                vte = jnp.concatenate([vt[h * MLA_V:(h + 1) * MLA_V], ones], axis=0)
                d = jnp.dot(vte, p, preferred_element_type=f32)
                parts[n] = d if parts[n] is None else parts[n] + d
        new = []
        for n, (h, c) in enumerate(chains):
            acc = als[n] * state[2 * n + 1] + parts[n]
            o_ref[h * MLA_V:(h + 1) * MLA_V, pl.ds(qoff + c * MLA_CW, MLA_CW)] = (
                acc[:MLA_V] / acc[MLA_V:MLA_V + 1]).astype(bf16)
            new += [mns[n], acc]
        return (tuple(cnew) if t_s is not None else ()), tuple(new)

    nt = nq * nk
    ahead, ns = MLA_AHEAD, MLA_SLOTS

    def trip(t0, carry, last):
        pend, state = list(carry[:nch * ahead]), carry[nch * ahead:]
        for u in range(MLA_TRIP):
            t_s = None if (last and u + ahead >= MLA_TRIP) else t0 + u + ahead
            cn, state = step(t_s, (u + ahead) % ns, t0 + u, u % ns, tuple(pend[:nch]), state)
            pend = pend[nch:] + list(cn)
        return tuple(pend) + state

    m0 = jnp.full((1, MLA_CW), -jnp.inf, f32)
    a0 = jnp.zeros((MLA_VROWS, MLA_CW), f32)
    carry = ()
    for t in range(ahead):
        carry += scores(t, t)
    carry = lax.fori_loop(0, nt // MLA_TRIP - 1, lambda i, c: trip(i * MLA_TRIP, c, False), carry + (m0, a0) * nch)
    trip(nt - MLA_TRIP, carry, True)


def _mla_attention(mqt, mk, mvt):
    B, L, _ = mk.shape
    npair = MLA_HEADS // 2
    nq, nk = L // MLA_TQ, L // MLA_TK
    assert (nq * nk) % MLA_TRIP == 0 and MLA_TRIP % MLA_SLOTS == 0
    mode = dict(pipeline_mode=pl.Buffered(1)) if L * 2 * MLA_PAD * 2 > 4 * 1024 * 1024 else {}
    return pl.pallas_call(
        functools.partial(_mla_body, nq=nq, nk=nk),
        grid=(B, npair),
        in_specs=[pl.BlockSpec((None, 2 * MLA_PAD, L), lambda b, p: (b, p, 0), **mode),
                  pl.BlockSpec((None, L, 2 * MLA_PAD), lambda b, p: (b, 0, p), **mode),
                  pl.BlockSpec((None, 2 * MLA_V, L), lambda b, p: (b, p, 0), **mode)],
        out_specs=pl.BlockSpec((None, 2 * MLA_V, L), lambda b, p: (b, p, 0)),
        out_shape=jax.ShapeDtypeStruct((B, MLA_HEADS * MLA_V, L), bf16),
        scratch_shapes=[pltpu.VMEM((MLA_SLOTS, 2 * MLA_TQ // MLA_CW, MLA_TK, MLA_CW), f32)],
        compiler_params=_cparams(("parallel", "parallel")),
        name="mla_attn",
    )(mqt, mk, mvt)


def _toeplitz_body(g_ref, pf_ref, o_ref):
    kern = jnp.dot(g_ref[...], pf_ref[...], precision=lax.Precision.HIGHEST, preferred_element_type=f32)
    bits = lax.bitcast_convert_type(kern.astype(bf16).astype(f32), jnp.int32)
    for cp in range(SSM_GC):
        for c in range(0, SSM_GC, 2):
            r = cp * SSM_GC + c
            packed = lax.shift_right_logical(bits[r:r + 1, :], 16) | bits[r + 1:r + 2, :]
            rowb = jnp.broadcast_to(packed, (LANES, 2 * LANES))
            toe = pltpu.roll(rowb, LANES + 1, 1, stride=1, stride_axis=0)[:, :LANES]
            even = lax.bitcast_convert_type(lax.shift_left(toe, 16), f32)
            odd = lax.bitcast_convert_type(toe & jnp.int32(-65536), f32)
            o_ref[cp * LANES:(cp + 1) * LANES, c * LANES:(c + 1) * LANES] = even.astype(bf16)
            o_ref[cp * LANES:(cp + 1) * LANES, (c + 1) * LANES:(c + 2) * LANES] = odd.astype(bf16)


def _toeplitz_gen(gm, pf):
    dg = pf.shape[0]
    n = SSM_GC * LANES
    k = 4 * SSM_STATE
    return pl.pallas_call(
        _toeplitz_body,
        grid=(dg,),
        in_specs=[pl.BlockSpec((None, SSM_GC * SSM_GC, k), lambda g: (g, 0, 0)),
                  pl.BlockSpec((None, k, 2 * LANES), lambda g: (g, 0, 0))],
        out_specs=pl.BlockSpec((None, n, n), lambda g: (g, 0, 0)),
        out_shape=jax.ShapeDtypeStruct((dg, n, n), bf16),
        compiler_params=_cparams(("parallel",)),
        name="s5_toeplitz",
    )(gm, pf)


def _ssm_body(*refs, segs):
    ns = len(segs)
    u_refs, (t_ref, wv_ref, wc_ref, a_ref, d_ref) = refs[:ns], refs[ns:ns + 5]
    o_refs = refs[ns + 5:2 * ns + 5]
    vre, vim, xfre, xfim, xbre, xbim = refs[2 * ns + 5:]
    u32 = jnp.concatenate([jnp.concatenate([u_ref[:, c, :] for c in range(SSM_GC)], axis=1) for u_ref in u_refs],
                          axis=0)
    m = u32.shape[0]
    ub = u32.astype(bf16)
    y = jnp.dot(ub, t_ref[...], preferred_element_type=f32)
    v = jnp.dot(ub, wv_ref[...], preferred_element_type=f32)
    vre[...] = v[:, :LANES]
    vim[...] = v[:, LANES:]
    are = a_ref[0:1, :]
    aim = a_ref[1:2, :]
    row0 = 0
    for nb, nj in segs:
        is_fwd = lax.broadcasted_iota(jnp.int32, (nb, LANES), 1) < SSM_STATE

        def rows(j, nb=nb, nj=nj, row0=row0):
            return pl.ds(row0 + j, nb, stride=nj) if nb > 1 else pl.ds(row0 + j, 1)

        def step(k, carry, nj=nj, rows=rows, is_fwd=is_fwd):
            xr, xi = carry
            rf, rb = rows(k), rows(nj - 1 - k)
            xfre[rf, :] = xr
            xfim[rf, :] = xi
            xbre[rb, :] = xr
            xbim[rb, :] = xi
            vr = jnp.where(is_fwd, vre[rf, :], vre[rb, :])
            vi = jnp.where(is_fwd, vim[rf, :], vim[rb, :])
            return are * xr - aim * xi + vr, are * xi + aim * xr + vi

        z = jnp.zeros((nb, LANES), f32)
        lax.fori_loop(0, nj, step, (z, z))
        row0 += nb * nj
    fwd_m = lax.broadcasted_iota(jnp.int32, (m, LANES), 1) < SSM_STATE
    xin = jnp.concatenate([jnp.where(fwd_m, xfre[...], xbre[...]),
                           jnp.where(fwd_m, xfim[...], xbim[...])], axis=1).astype(bf16)
    y = y + jnp.dot(xin, wc_ref[...], preferred_element_type=f32) + u32 * d_ref[...]
    row0 = 0
    for (nb, nj), o_ref in zip(segs, o_refs):
        for c in range(SSM_GC):
            o_ref[:, c, :] = y[row0:row0 + nb * nj, c * LANES:(c + 1) * LANES]
        row0 += nb * nj


def _ssm(uts, layer, S, segs):
    n = SSM_GC * LANES
    ms = [nb * nj for nb, nj in segs]
    gspec = lambda tail: pl.BlockSpec((None,) + tail, lambda g: (layer * SSM_GROUPS + g,) + (0,) * len(tail))
    io = [pl.BlockSpec((m, None, SSM_GC, LANES), lambda g: (0, g, 0, 0)) for m in ms]
    return pl.pallas_call(
        functools.partial(_ssm_body, segs=tuple(segs)),
        grid=(SSM_GROUPS,),
        in_specs=io + [gspec((n, n)), gspec((n, 4 * SSM_STATE)), gspec((4 * SSM_STATE, n)),
                       gspec((2, LANES)), gspec((1, n))],
        out_specs=io,
        out_shape=[jax.ShapeDtypeStruct((m, SSM_GROUPS, SSM_GC, LANES), f32) for m in ms],
        scratch_shapes=[pltpu.VMEM((sum(ms), LANES), f32)] * 6,
        compiler_params=_cparams(("parallel",)),
        name="s5_conv",
    )(*uts, S["toe"], S["wv"], S["wc"], S["a"], S["d"])


def _ssm_tables(a_re, a_im, b_re, b_im, c_re, c_im, log_dt, ssm_d):
    depth = a_re.shape[0]
    dg = depth * SSM_GROUPS
    lam = lax.complex(a_re.astype(f32), a_im.astype(f32))
    dt = jnp.exp(log_dt.astype(f32))[..., None]
    lam_dt = lam * dt
    lam_bar = jnp.exp(lam_dt)
    b_bar = ((lam_bar - 1.0) / lam)[..., None] * lax.complex(b_re.astype(f32), b_im.astype(f32))
    cmat = lax.complex(c_re.astype(f32), c_im.astype(f32))
    steps = jnp.arange(LANES + 1, dtype=f32)
    pw = jnp.exp(lam_dt[..., None] * steps)
    pwf, pwb = pw[:, 0], pw[:, 1]

    e = np.arange(2 * LANES)
    f_ok = (e >= LANES - 1) & (e <= 2 * LANES - 2)
    b_ok = e <= LANES - 1
    pf_f = jnp.where(f_ok, pwf[..., np.clip(e - (LANES - 1), 0, LANES - 1)], 0.0)
    pf_b = jnp.where(b_ok, pwb[..., np.clip(LANES - 1 - e, 0, LANES - 1)], 0.0)
    pf = jnp.stack([pf_f.real, pf_f.imag, pf_b.real, pf_b.imag], axis=2)
    pf = pf.reshape(dg, 4 * SSM_STATE, 2 * LANES)

    bbf, bbb = b_bar[:, 0], b_bar[:, 1]
    cf, cb = cmat[:, 0], cmat[:, 1]
    gf = cf[:, :, None, :, :] * jnp.swapaxes(bbf, -1, -2)[:, :, :, None, :]
    gb = cb[:, :, None, :, :] * jnp.swapaxes(bbb, -1, -2)[:, :, :, None, :]
    gm = jnp.concatenate([gf.real, -gf.imag, gb.real, -gb.imag], axis=-1)
    gm = gm.reshape(dg, SSM_GC * SSM_GC, 4 * SSM_STATE)

    s_idx = np.arange(LANES)
    pvf = jnp.swapaxes(pwf[..., LANES - 1 - s_idx], -1, -2)
    pvb = jnp.swapaxes(pwb[..., s_idx], -1, -2)
    pv1 = jnp.concatenate([pvf.real, pvb.real, pvf.imag, pvb.imag], axis=-1).reshape(dg, LANES, 4 * SSM_STATE)
    pv2 = jnp.concatenate([pvf.imag, pvb.imag, pvf.real, pvb.real], axis=-1).reshape(dg, LANES, 4 * SSM_STATE)
    bft, bbt = jnp.swapaxes(bbf, -1, -2), jnp.swapaxes(bbb, -1, -2)
    bv = jnp.stack([jnp.concatenate([bft.real, bbt.real, bft.real, bbt.real], axis=-1),
                    jnp.concatenate([-bft.imag, -bbt.imag, bft.imag, bbt.imag], axis=-1)], axis=3)
    bv = bv.reshape(dg, SSM_GC, 2, 4 * SSM_STATE)
    t_idx = np.arange(LANES)
    pcf, pcb = pwf[..., t_idx + 1], pwb[..., LANES - t_idx]
    pc = jnp.stack([pcf.real, pcf.imag, pcb.real, pcb.imag], axis=2).reshape(dg, 4, SSM_STATE, LANES)
    cft, cbt = jnp.swapaxes(cf, -1, -2), jnp.swapaxes(cb, -1, -2)
    ct = jnp.stack([cft.real, cft.imag, cbt.real, cbt.imag], axis=2).reshape(dg, 4, SSM_STATE, SSM_GC)

    a128 = jnp.concatenate([pwf[..., LANES], pwb[..., LANES]], axis=-1)
    a = jnp.stack([a128.real, a128.imag], axis=2).reshape(dg, 2, LANES)
    d = jnp.repeat(ssm_d.astype(f32).reshape(dg, SSM_GC), LANES, axis=-1).reshape(dg, 1, SSM_GC * LANES)
    return dict(pf=pf, gm=gm, pv1=pv1, pv2=pv2, bv=bv, pc=pc, ct=ct, a=a, d=d)


def _state_tables_body(pv1_ref, pv2_ref, bv_ref, pc_ref, ct_ref, wv_ref, wc_ref):
    pv1, pv2 = pv1_ref[...], pv2_ref[...]
    for cp in range(SSM_GC):
        wv_ref[cp * LANES:(cp + 1) * LANES, :] = (pv1 * bv_ref[cp, 0:1, :] + pv2 * bv_ref[cp, 1:2, :]).astype(bf16)
    pfr, pfi, pbr, pbi = pc_ref[0], pc_ref[1], pc_ref[2], pc_ref[3]
    cfr, cfi, cbr, cbi = ct_ref[0], ct_ref[1], ct_ref[2], ct_ref[3]
    for c in range(SSM_GC):
        col = slice(c, c + 1)
        tile = jnp.concatenate([cfr[:, col] * pfr - cfi[:, col] * pfi,
                                cbr[:, col] * pbr - cbi[:, col] * pbi,
                                -(cfr[:, col] * pfi + cfi[:, col] * pfr),
                                -(cbr[:, col] * pbi + cbi[:, col] * pbr)], axis=0)
        wc_ref[:, c * LANES:(c + 1) * LANES] = tile.astype(bf16)


def _state_tables(S):
    dg = S["pf"].shape[0]
    n, k = SSM_GC * LANES, 4 * SSM_STATE
    g3 = lambda a, b: pl.BlockSpec((None, a, b), lambda g: (g, 0, 0))
    g4 = lambda a, b, c: pl.BlockSpec((None, a, b, c), lambda g: (g, 0, 0, 0))
    return pl.pallas_call(
        _state_tables_body,
        grid=(dg,),
        in_specs=[g3(LANES, k), g3(LANES, k), g4(SSM_GC, 2, k), g4(4, SSM_STATE, LANES), g4(4, SSM_STATE, SSM_GC)],
        out_specs=(g3(n, k), g3(k, n)),
        out_shape=(jax.ShapeDtypeStruct((dg, n, k), bf16), jax.ShapeDtypeStruct((dg, k, n), bf16)),
        compiler_params=_cparams(("parallel",)),
        name="s5_state_tables",
    )(S["pv1"], S["pv2"], S["bv"], S["pc"], S["ct"])


def _mixffn_body(x_ref, mod_ref, yna_ref, ymlat_ref, yst_ref, gluw_ref, glub_ref, wo_ref, g2_ref,
                 wg_ref, wu_ref, wd_ref, o_ref, act_ref):
    ys = jnp.concatenate([yst_ref[c].reshape(SSM_WIDTH, LANES).T for c in range(yst_ref.shape[0])],
                         axis=0)
    ys = 0.5 * ys * (1.0 + jnp.tanh(math.sqrt(2.0 / math.pi) * (ys + 0.044715 * (ys * ys * ys))))
    gate = jax.nn.sigmoid(jnp.dot(ys.astype(bf16), gluw_ref[...], preferred_element_type=f32) + glub_ref[...])
    yssm = (ys * gate).astype(bf16)
    o1 = NA_WIDTH
    o2 = NA_WIDTH + MLA_HEADS * MLA_V
    mix = (jnp.dot(yna_ref[...], wo_ref[0:o1, :], preferred_element_type=f32)
           + lax.dot_general(ymlat_ref[...], wo_ref[o1:o2, :], _TN, preferred_element_type=f32)
           + jnp.dot(yssm, wo_ref[o2:, :], preferred_element_type=f32))
    x1 = x_ref[...] + mod_ref[2:3, :] * mix
    o_ref[...] = x1
    ms = jnp.mean(x1 * x1, axis=-1, keepdims=True)
    h2 = ((x1 * lax.rsqrt(ms + RMS_EPS) * g2_ref[...]) * (1.0 + mod_ref[4:5, :]) + mod_ref[3:4, :]).astype(bf16)

    th = 256
    for c in range(FFN_HIDDEN // th):
        cs = slice(c * th, (c + 1) * th)
        g = jnp.dot(h2, wg_ref[:, cs], preferred_element_type=f32)
        u = jnp.dot(h2, wu_ref[:, cs], preferred_element_type=f32)
        act_ref[:, cs] = (g * jax.nn.sigmoid(g) * u).astype(bf16)
    ffn = jnp.dot(act_ref[...], wd_ref[...], preferred_element_type=f32)
    o_ref[...] = o_ref[...] + mod_ref[5:6, :] * ffn


def _mixffn(x, mod, yna, ymlat, yst, layer, P):
    B, L, D = x.shape
    tm = TOKEN_TILE
    nt = L // tm
    cpt = tm // LANES
    tok = lambda w: pl.BlockSpec((None, tm, w), lambda b, i: (b, i, 0))

    def wspec(name):
        tail = P[name].shape[1:]
        return pl.BlockSpec((None,) + tail, lambda b, i: (layer,) + (0,) * len(tail), pipeline_mode=pl.Buffered(1))

    return pl.pallas_call(
        _mixffn_body,
        grid=(B, nt),
        in_specs=[tok(D), pl.BlockSpec((None, 6, D), lambda b, i: (b, 0, 0)), tok(NA_WIDTH),
                  pl.BlockSpec((None, MLA_HEADS * MLA_V, tm), lambda b, i: (b, 0, i)),
                  pl.BlockSpec((cpt, SSM_GROUPS, SSM_GC, LANES), lambda b, i: (b * nt + i, 0, 0, 0)),
                  wspec("glu_w"), wspec("glu_b"), wspec("w_out"), wspec("g2"),
                  wspec("w_gate"), wspec("w_up"), wspec("w_down")],
        out_specs=tok(D),
        out_shape=jax.ShapeDtypeStruct((B, L, D), f32),
        scratch_shapes=[pltpu.VMEM((tm, FFN_HIDDEN), bf16)],
        compiler_params=_cparams(("parallel", "parallel")),
        name="mix_ffn",
    )(x, mod, yna, ymlat, yst, P["glu_w"], P["glu_b"], P["w_out"], P["g2"], P["w_gate"], P["w_up"], P["w_down"])


def _prep_params(norm1_g, w_in, na_q_g, na_k_g, mla_cq_g, mla_ckv_g, mla_w_uq, mla_w_ukv, mla_qn_g, mla_kn_g,
                 mla_qr_g, mla_kr_g, glu_w, glu_b, w_out, norm2_g, ffn_w_gate, ffn_w_up, ffn_w_down):
    depth = w_in.shape[0]
    o1 = 3 * NA_WIDTH
    o2 = o1 + MLA_Q_LORA
    o3 = o2 + MLA_KV_LORA
    o4 = o3 + MLA_ROPE
    row = lambda a: a.astype(f32)[:, None, :]
    zeros = lambda *s: jnp.zeros(s, f32)
    P = {}
    P["g1"] = row(norm1_g)
    P["g2"] = row(norm2_g)
    P["w_na"] = w_in[:, :, :o1].astype(bf16)
    P["w_c"] = jnp.concatenate([w_in[:, :, o1:o3], zeros(depth, D_MODEL, MLA_NOPE), w_in[:, :, o3:o4],
                                zeros(depth, D_MODEL, MLA_PAD - MLA_NOPE - MLA_ROPE)], axis=-1).astype(bf16)
    P["w_ut"] = jnp.swapaxes(w_in[:, :, o4:], 1, 2).astype(bf16)
    wq = mla_w_uq.reshape(depth, MLA_Q_LORA, MLA_HEADS, MLA_NOPE + MLA_ROPE)
    wq = jnp.concatenate([wq, zeros(depth, MLA_Q_LORA, MLA_HEADS, MLA_PAD - MLA_NOPE - MLA_ROPE)], axis=-1)
    P["w_uq"] = wq.reshape(depth, MLA_Q_LORA, MLA_HEADS * MLA_PAD).astype(bf16)
    wkv = mla_w_ukv.reshape(depth, MLA_KV_LORA, MLA_HEADS, MLA_NOPE + MLA_V)
    wk = jnp.concatenate([wkv[..., :MLA_NOPE], zeros(depth, MLA_KV_LORA, MLA_HEADS, MLA_PAD - MLA_NOPE)], axis=-1)
    P["w_ukv_k"] = wk.reshape(depth, MLA_KV_LORA, MLA_HEADS * MLA_PAD).astype(bf16)
    P["w_ukv_vt"] = jnp.swapaxes(wkv[..., MLA_NOPE:].reshape(depth, MLA_KV_LORA, MLA_HEADS * MLA_V), 1, 2).astype(bf16)
    P["gq_na"] = row(jnp.tile(na_q_g, (1, NA_HEADS))) * (NA_HEAD_DIM ** -0.5 * math.log2(math.e))
    P["gk_na"] = row(jnp.tile(na_k_g, (1, NA_HEADS)))
    P["g_cq"] = row(mla_cq_g)
    P["g_ckv"] = row(mla_ckv_g)
    scale = (MLA_NOPE + MLA_ROPE) ** -0.5 * math.log2(math.e)
    pad = MLA_PAD - MLA_NOPE - MLA_ROPE
    gq = jnp.concatenate([mla_qn_g, mla_qr_g, zeros(depth, pad)], axis=-1) * scale
    P["gq"] = row(jnp.tile(gq, (1, MLA_HEADS)))
    gk = jnp.concatenate([mla_kn_g, zeros(depth, MLA_PAD - MLA_NOPE)], axis=-1)
    P["gk"] = row(jnp.tile(gk, (1, MLA_HEADS)))
    P["gkr"] = row(jnp.concatenate([zeros(depth, MLA_NOPE), mla_kr_g, zeros(depth, pad)], axis=-1))
    invq = np.concatenate([np.full(MLA_NOPE, 1.0 / MLA_NOPE), np.full(MLA_ROPE, 1.0 / MLA_ROPE), np.zeros(pad)])
    invk = np.concatenate([np.full(MLA_NOPE, 1.0 / MLA_NOPE), np.zeros(MLA_PAD - MLA_NOPE)])
    P["invq"] = jnp.asarray(np.tile(invq, MLA_HEADS)[None, :], f32)
    P["invk"] = jnp.asarray(np.tile(invk, MLA_HEADS)[None, :], f32)
    lane = np.arange(NA_WIDTH)
    P["G_na"] = jnp.asarray((lane[:, None] // NA_HEAD_DIM) == (lane[None, :] // NA_HEAD_DIM), bf16)
    lane = np.arange(2 * MLA_PAD)
    grp = np.where(lane % MLA_PAD < MLA_NOPE, 0, np.where(lane % MLA_PAD < MLA_NOPE + MLA_ROPE, 1, 2))
    same = (lane[:, None] // MLA_PAD == lane[None, :] // MLA_PAD) & (grp[:, None] == grp[None, :]) & (grp[:, None] < 2)
    P["G_qm"] = jnp.asarray(same, bf16)
    P["glu_w"] = glu_w.astype(bf16)
    P["glu_b"] = row(glu_b)
    P["w_out"] = w_out.astype(bf16)
    P["w_gate"] = ffn_w_gate.astype(bf16)
    P["w_up"] = ffn_w_up.astype(bf16)
    P["w_down"] = ffn_w_down.astype(bf16)
    return P


def _rope_tables(length):
    inv = 1.0 / (ROPE_BASE ** (jnp.arange(0, MLA_ROPE, 2, dtype=f32) / MLA_ROPE))
    ang = jnp.arange(length, dtype=f32)[:, None] * inv[None, :]
    cos, sin = jnp.cos(ang), jnp.sin(ang)
    half = MLA_ROPE // 2
    z = lambda w: jnp.zeros((length, w), f32)
    pad = MLA_PAD - MLA_NOPE - MLA_ROPE
    cos_t = jnp.concatenate([jnp.ones((length, MLA_NOPE), f32), cos, cos, z(pad)], axis=-1)
    sin_lo = jnp.concatenate([z(MLA_NOPE), -sin, z(half), z(pad)], axis=-1)
    sin_hi = jnp.concatenate([z(MLA_NOPE), z(half), sin, z(pad)], axis=-1)
    return cos_t, sin_lo, sin_hi


def _trunks(xs, mods, P, S, bias):
    xs = list(xs)
    ropes = [_rope_tables(x.shape[1]) for x in xs]
    segs = [(x.shape[0], x.shape[1] // LANES) for x in xs]
    for layer in range(len(mods[0])):
        heads, uts = [], []
        for x, mod, rope in zip(xs, mods, ropes):
            naq, nak, nav, mqt, mk, mvt, ut = _inproj(x, mod[layer], layer, P, rope)
            heads.append((_na_attention(naq, nak, nav, bias, layer), _mla_attention(mqt, mk, mvt)))
            uts.append(ut)
        ysts = _ssm(uts, layer, S, segs)
        xs = [_mixffn(x, mod[layer], yna, ymla, yst, layer, P)
              for x, mod, (yna, ymla), yst in zip(xs, mods, heads, ysts)]
    return xs


def kernel(x_prompt, x_sample, c_prompt, c_sample, ada_w, ada_b, norm1_g, w_in, na_q_g, na_k_g, na_rpb, mla_cq_g,
           mla_ckv_g, mla_w_uq, mla_w_ukv, mla_qn_g, mla_kn_g, mla_qr_g, mla_kr_g, ssm_a_re, ssm_a_im, ssm_b_re,
           ssm_b_im, ssm_c_re, ssm_c_im, ssm_log_dt, ssm_d, glu_w, glu_b, w_out, norm2_g, ffn_w_gate, ffn_w_up,
           ffn_w_down):
    depth = w_in.shape[0]
    nbp, nbs = c_prompt.shape[0], c_sample.shape[0]
    rows = -(-(nbp + nbs) // 8) * 8
    c_all = jnp.concatenate([c_prompt, c_sample, jnp.zeros((rows - nbp - nbs, D_MODEL), f32)], axis=0)
    mod = _modulation(c_all, ada_w, ada_b).reshape(depth, rows, 6, D_MODEL)
    mods_p = [mod[l, :nbp] for l in range(depth)]
    mods_s = [mod[l, nbp:nbp + nbs] for l in range(depth)]

    P = _prep_params(norm1_g, w_in, na_q_g, na_k_g, mla_cq_g, mla_ckv_g, mla_w_uq, mla_w_ukv, mla_qn_g, mla_kn_g,
                     mla_qr_g, mla_kr_g, glu_w, glu_b, w_out, norm2_g, ffn_w_gate, ffn_w_up, ffn_w_down)
    S = _ssm_tables(ssm_a_re, ssm_a_im, ssm_b_re, ssm_b_im, ssm_c_re, ssm_c_im, ssm_log_dt, ssm_d)
    S["toe"] = _toeplitz_gen(S["gm"], S["pf"])
    S["wv"], S["wc"] = _state_tables(S)
    bias = _na_bias_tables(na_rpb)

    y_prompt, y_sample = _trunks((x_prompt, x_sample), (mods_p, mods_s), P, S, bias)
    return (y_prompt, y_sample)
```

```python
import functools
import math

import numpy as np
import jax
import jax.numpy as jnp
from jax import lax
from jax.experimental import pallas as pl
from jax.experimental.pallas import tpu as pltpu

f32 = jnp.float32
bf16 = jnp.bfloat16

D_MODEL = 1024
GRID_W = 64
NA_HEADS = 4
NA_HEAD_DIM = 64
NA_WIDTH = NA_HEADS * NA_HEAD_DIM
NA_ROWS = 8
NA_COLS = 16
MLA_HEADS = 8
MLA_NOPE = 64
MLA_ROPE = 32
MLA_V = 64
MLA_Q_LORA = 384
MLA_KV_LORA = 256
ROPE_BASE = 10000.0
SSM_GROUPS = 16
SSM_GC = 16
SSM_STATE = 64
SSM_WIDTH = SSM_GROUPS * SSM_GC
FFN_HIDDEN = 2816
RMS_EPS = 1e-6
NEG_INF = -1e30

LANES = 128
MLA_PAD = 128
NA_QCOLS = 16
NA_KCOLS = 2 * NA_COLS
NA_QROWS = 4
NA_SUB = 4
NA_WROWS = NA_QROWS + NA_ROWS
NA_BIAS_LANE0 = 200
TOKEN_TILE = 512
MXU_TILE = 256
BF16_SUBLANES = 16
MLA_CW = MXU_TILE
MLA_TQ = MLA_CW
MLA_TK = 1024
MLA_KC = MXU_TILE
MLA_AHEAD = 2
MLA_SLOTS = 2 * MLA_AHEAD
MLA_TRIP = 16
MLA_VROWS = MLA_V + BF16_SUBLANES
MOD_TILE = 1536
VMEM_LIMIT = 48 * 1024 * 1024
SINGLE_BUFFER_BYTES = 4 * 1024 * 1024

_NT = (((1,), (1,)), ((), ()))
_TN = (((0,), (0,)), ((), ()))


def _cparams(sem):
    return pltpu.CompilerParams(dimension_semantics=sem, vmem_limit_bytes=VMEM_LIMIT)


def _layer_spec(tail, layer):
    n = len(tail)
    return pl.BlockSpec((None,) + tuple(tail), lambda *_: (layer,) + (0,) * n)


def _mod_body(c_ref, w_ref, b_ref, o_ref):
    c = c_ref[...]
    s = c * jax.nn.sigmoid(c)
    o_ref[...] = jnp.dot(s, w_ref[...], precision=lax.Precision.HIGHEST, preferred_element_type=f32) + b_ref[...]


def _modulation(c_all, ada_w, ada_b):
    depth, d, n = ada_w.shape
    tn = MOD_TILE
    rows = c_all.shape[0]
    return pl.pallas_call(
        _mod_body,
        grid=(depth, n // tn),
        in_specs=[
            pl.BlockSpec((rows, d), lambda l, j: (0, 0)),
            pl.BlockSpec((None, d, tn), lambda l, j: (l, 0, j)),
            pl.BlockSpec((None, 1, tn), lambda l, j: (l, 0, j)),
        ],
        out_specs=pl.BlockSpec((None, rows, tn), lambda l, j: (l, 0, j)),
        out_shape=jax.ShapeDtypeStruct((depth, rows, n), f32),
        compiler_params=_cparams(("arbitrary", "arbitrary")),
        name="adaln_mod",
    )(c_all, ada_w, ada_b.reshape(depth, 1, n))


def _rope(x, cos, sin_lo, sin_hi):
    return x * cos + pltpu.roll(x, LANES - 16, 1) * sin_lo + pltpu.roll(x, 16, 1) * sin_hi


def _inproj_body(x_ref, mod_ref, g1_ref, wna_ref, wc_ref, wut_ref, wuq_ref, wukvk_ref, wukvvt_ref,
                 gqna_ref, gkna_ref, gcq_ref, gckv_ref, gq_ref, gk_ref, gkr_ref, invq_ref, invk_ref,
                 gna_ref, gqm_ref, cos_ref, sinlo_ref, sinhi_ref,
                 naq_ref, nak_ref, nav_ref, mqt_ref, mk_ref, mvt_ref, ut_ref):
    x = x_ref[...]
    shift1 = mod_ref[0:1, :]
    scale1 = mod_ref[1:2, :]
    ms = jnp.mean(x * x, axis=-1, keepdims=True)
    h = (x * lax.rsqrt(ms + RMS_EPS) * g1_ref[...]) * (1.0 + scale1) + shift1
    hb = h.astype(bf16)

    z = jnp.dot(hb, wna_ref[...], preferred_element_type=f32)
    q = z[:, :NA_WIDTH]
    k = z[:, NA_WIDTH:2 * NA_WIDTH]
    gna = gna_ref[...]
    ssq = jnp.dot((q * q).astype(bf16), gna, preferred_element_type=f32) * (1.0 / NA_HEAD_DIM)
    ssk = jnp.dot((k * k).astype(bf16), gna, preferred_element_type=f32) * (1.0 / NA_HEAD_DIM)
    naq_ref[...] = (q * lax.rsqrt(ssq + RMS_EPS) * gqna_ref[...]).astype(bf16)
    nak_ref[...] = (k * lax.rsqrt(ssk + RMS_EPS) * gkna_ref[...]).astype(bf16)
    nav_ref[...] = z[:, 2 * NA_WIDTH:].astype(bf16)

    zc = jnp.dot(hb, wc_ref[...], preferred_element_type=f32)
    cq = zc[:, :MLA_Q_LORA]
    ckv = zc[:, MLA_Q_LORA:MLA_Q_LORA + MLA_KV_LORA]
    cqn = (cq * lax.rsqrt(jnp.mean(cq * cq, axis=-1, keepdims=True) + RMS_EPS) * gcq_ref[...]).astype(bf16)
    ckvn = (ckv * lax.rsqrt(jnp.mean(ckv * ckv, axis=-1, keepdims=True) + RMS_EPS) * gckv_ref[...]).astype(bf16)

    cos = cos_ref[...]
    sin_lo = sinlo_ref[...]
    sin_hi = sinhi_ref[...]
    gqm = gqm_ref[...]

    kr = zc[:, MLA_Q_LORA + MLA_KV_LORA:]
    kr_ms = jnp.sum(kr * kr, axis=-1, keepdims=True) * (1.0 / MLA_ROPE)
    kr = _rope(kr * lax.rsqrt(kr_ms + RMS_EPS) * gkr_ref[...], cos, sin_lo, sin_hi)

    qraw = jnp.dot(cqn, wuq_ref[...], preferred_element_type=f32)
    kraw = jnp.dot(ckvn, wukvk_ref[...], preferred_element_type=f32)
    for p in range(MLA_HEADS // 2):
        sl = slice(2 * p * MLA_PAD, 2 * (p + 1) * MLA_PAD)
        qs = qraw[:, sl]
        ss = jnp.dot((qs * qs).astype(bf16), gqm, preferred_element_type=f32) * invq_ref[:, sl]
        qn = qs * lax.rsqrt(ss + RMS_EPS) * gq_ref[:, sl]
        ks = kraw[:, sl]
        ss = jnp.dot((ks * ks).astype(bf16), gqm, preferred_element_type=f32) * invk_ref[:, sl]
        kn = ks * lax.rsqrt(ss + RMS_EPS) * gk_ref[:, sl]
        for hh in range(2):
            lo = (2 * p + hh) * MLA_PAD
            piece = qn[:, hh * MLA_PAD:(hh + 1) * MLA_PAD]
            mqt_ref[lo:lo + MLA_PAD, :] = _rope(piece, cos, sin_lo, sin_hi).T.astype(bf16)
            mk_ref[:, lo:lo + MLA_PAD] = (kn[:, hh * MLA_PAD:(hh + 1) * MLA_PAD] + kr).astype(bf16)
    mvt_ref[...] = lax.dot_general(wukvvt_ref[...], ckvn, _NT, preferred_element_type=f32).astype(bf16)

    ut = lax.dot_general(wut_ref[...], hb, _NT, preferred_element_type=f32)
    for c in range(ut_ref.shape[0]):
        ut_ref[c] = ut[:, c * LANES:(c + 1) * LANES].reshape(SSM_GROUPS, SSM_GC, LANES)


def _inproj(x, mod, layer, P, rope_tabs):
    B, L, D = x.shape
    tm = TOKEN_TILE
    nt = L // tm
    cpt = tm // LANES
    tok = lambda w: pl.BlockSpec((None, tm, w), lambda b, i: (b, i, 0))
    pos = pl.BlockSpec((tm, LANES), lambda b, i: (i, 0))
    const2 = lambda a: pl.BlockSpec(a.shape, lambda b, i: (0, 0))
    lw = lambda name: _layer_spec(P[name].shape[1:], layer)
    names = ["g1", "w_na", "w_c", "w_ut", "w_uq", "w_ukv_k", "w_ukv_vt",
             "gq_na", "gk_na", "g_cq", "g_ckv", "gq", "gk", "gkr"]
    consts = ["invq", "invk", "G_na", "G_qm"]
    in_specs = ([tok(D), pl.BlockSpec((None, 6, D), lambda b, i: (b, 0, 0))]
                + [lw(n) for n in names] + [const2(P[n]) for n in consts] + [pos, pos, pos])
    out_shapes = (
        jax.ShapeDtypeStruct((B, L, NA_WIDTH), bf16),
        jax.ShapeDtypeStruct((B, L, NA_WIDTH), bf16),
        jax.ShapeDtypeStruct((B, L, NA_WIDTH), bf16),
        jax.ShapeDtypeStruct((B, MLA_HEADS * MLA_PAD, L), bf16),
        jax.ShapeDtypeStruct((B, L, MLA_HEADS * MLA_PAD), bf16),
        jax.ShapeDtypeStruct((B, MLA_HEADS * MLA_V, L), bf16),
        jax.ShapeDtypeStruct((B * L // LANES, SSM_GROUPS, SSM_GC, LANES), f32),
    )
    chan = lambda w: pl.BlockSpec((None, w, tm), lambda b, i: (b, 0, i))
    out_specs = (tok(NA_WIDTH), tok(NA_WIDTH), tok(NA_WIDTH), chan(MLA_HEADS * MLA_PAD), tok(MLA_HEADS * MLA_PAD),
                 chan(MLA_HEADS * MLA_V),
                 pl.BlockSpec((cpt, SSM_GROUPS, SSM_GC, LANES), lambda b, i: (b * nt + i, 0, 0, 0)))
    return pl.pallas_call(
        _inproj_body,
        grid=(B, nt),
        in_specs=in_specs,
        out_specs=out_specs,
        out_shape=out_shapes,
        compiler_params=_cparams(("parallel", "parallel")),
        name="in_proj",
    )(x, mod, *[P[n] for n in names], *[P[n] for n in consts], *rope_tabs)


def _na_key_col0(cb):
    return min(max(cb * NA_QCOLS - NA_COLS // 2, 0), GRID_W - NA_KCOLS)


def _na_body(q_ref, k_ref, v_ref, b_ref, o_ref, *, rows):
    nblk = rows // NA_QROWS
    lane = lax.broadcasted_iota(jnp.int32, (1, NA_WIDTH), 1)
    hms = [(lane >= h * NA_HEAD_DIM) & (lane < (h + 1) * NA_HEAD_DIM) for h in range(NA_HEADS)]
    nq = NA_QROWS * NA_QCOLS
    for sb in range(NA_SUB):
        blk = pl.program_id(1) * NA_SUB + sb
        variant = jnp.where(blk == 0, 0, jnp.where(blk == nblk - 1, 2, 1))
        ws = jnp.clip(NA_QROWS * blk - NA_ROWS // 2, 0, rows - NA_WROWS)
        base = pl.multiple_of(ws * GRID_W, GRID_W)
        q0 = sb * NA_QROWS * GRID_W
        for cb in range(GRID_W // NA_QCOLS):
            kc0 = _na_key_col0(cb)
            kw = jnp.concatenate([k_ref[pl.ds(base + kl * GRID_W + kc0, NA_KCOLS), :] for kl in range(NA_WROWS)],
                                 axis=0)
            vw = jnp.concatenate([v_ref[pl.ds(base + kl * GRID_W + kc0, NA_KCOLS), :] for kl in range(NA_WROWS)],
                                 axis=0)
            qrows = [slice(q0 + rl * GRID_W + cb * NA_QCOLS, q0 + rl * GRID_W + (cb + 1) * NA_QCOLS)
                     for rl in range(NA_QROWS)]
            qb = jnp.concatenate([q_ref[r, :] for r in qrows], axis=0)
            qs = jnp.concatenate([jnp.where(hm, qb, jnp.zeros_like(qb)) for hm in hms], axis=0)
            s = lax.dot_general(qs, kw, _NT, preferred_element_type=f32) + b_ref[variant, cb].reshape(NA_HEADS * nq, -1)
            m = jnp.max(s, axis=-1, keepdims=True)
            p = jnp.exp2(s - m)
            l = jnp.sum(p, axis=-1, keepdims=True)
            o = jnp.dot(p.astype(bf16), vw, preferred_element_type=f32) / l
            ob = o[:nq]
            for h in range(1, NA_HEADS):
                ob = jnp.where(hms[h], o[h * nq:(h + 1) * nq], ob)
            for rl, r in enumerate(qrows):
                o_ref[r, :] = ob[rl * NA_QCOLS:(rl + 1) * NA_QCOLS].astype(bf16)


def _na_attention(q, k, v, bias, layer):
    B, L, W = q.shape
    rows = L // GRID_W
    tq = NA_SUB * NA_QROWS * GRID_W
    ncb = GRID_W // NA_QCOLS
    mode = dict(pipeline_mode=pl.Buffered(1)) if L * W * q.dtype.itemsize > SINGLE_BUFFER_BYTES else {}
    full = pl.BlockSpec((None, L, W), lambda b, i: (b, 0, 0), **mode)
    return pl.pallas_call(
        functools.partial(_na_body, rows=rows),
        grid=(B, L // tq),
        in_specs=[pl.BlockSpec((None, tq, W), lambda b, i: (b, i, 0)), full, full,
                  pl.BlockSpec((None, 3, ncb, NA_HEADS, NA_QROWS * NA_QCOLS, NA_WROWS * NA_KCOLS),
                               lambda b, i: (layer, 0, 0, 0, 0, 0), pipeline_mode=pl.Buffered(1))],
        out_specs=pl.BlockSpec((None, tq, W), lambda b, i: (b, i, 0)),
        out_shape=jax.ShapeDtypeStruct((B, L, W), bf16),
        compiler_params=_cparams(("parallel", "arbitrary")),
        name="na_attn",
    )(q, k, v, bias)


def _na_bias_body(r_ref, o_ref):
    r = r_ref[...]
    qc = lax.broadcasted_iota(jnp.int32, (NA_QCOLS, LANES), 0)
    lane = lax.broadcasted_iota(jnp.int32, (NA_QCOLS, LANES), 1)
    piece = lane // NA_KCOLS
    neg = jnp.full((NA_QCOLS, LANES), NEG_INF, f32)
    npc = LANES // NA_KCOLS
    for cb in range(GRID_W // NA_QCOLS):
        kc0 = _na_key_col0(cb)
        qcol = cb * NA_QCOLS + qc
        kcol = kc0 + lane - piece * NA_KCOLS
        cs = jnp.clip(qcol - NA_COLS // 2, 0, GRID_W - NA_COLS)
        cvalid = (kcol >= cs) & (kcol < cs + NA_COLS)
        tiles = []
        for dr in range(2 * NA_ROWS - 1):
            rowb = jnp.broadcast_to(r[dr:dr + 1, :], (NA_QCOLS, 2 * LANES))
            t = neg
            for k in range(npc):
                base = 2 * LANES - (NA_COLS - 1) - NA_BIAS_LANE0 + cb * NA_QCOLS - kc0 + k * NA_KCOLS
                rolled = pltpu.roll(rowb, base, 1, stride=1, stride_axis=0)[:, :LANES]
                t = jnp.where(piece == k, rolled, t)
            tiles.append(jnp.where(cvalid, t, neg))
        for v, (off, lo) in enumerate(((0, 0), (-(NA_ROWS // 2), None), (-NA_ROWS, NA_ROWS // 2))):
            for rl in range(NA_QROWS):
                vregs = []
                for j in range(NA_WROWS // npc):
                    t = neg
                    for k in range(npc):
                        kl = j * npc + k
                        d = off + kl - rl
                        ok = (-(NA_ROWS // 2) <= d < NA_ROWS // 2) if lo is None else (lo <= kl < lo + NA_ROWS)
                        if ok:
                            t = jnp.where(piece == k, tiles[d + NA_ROWS - 1], t)
                    vregs.append(t)
                o_ref[v, cb, rl * NA_QCOLS:(rl + 1) * NA_QCOLS, :] = jnp.concatenate(vregs, axis=1)


def _na_bias_tables(rpb):
    depth, nh, nr, nc = rpb.shape
    rp = jnp.pad(rpb.astype(f32) * math.log2(math.e),
                 ((0, 0), (0, 0), (0, 16 - nr), (NA_BIAS_LANE0, 2 * LANES - NA_BIAS_LANE0 - nc)))
    ncb, nq, nk = GRID_W // NA_QCOLS, NA_QROWS * NA_QCOLS, NA_WROWS * NA_KCOLS
    return pl.pallas_call(
        _na_bias_body,
        grid=(depth, nh),
        in_specs=[pl.BlockSpec((None, None, 16, 2 * LANES), lambda l, h: (l, h, 0, 0))],
        out_specs=pl.BlockSpec((None, 3, ncb, None, nq, nk), lambda l, h: (l, 0, 0, h, 0, 0)),
        out_shape=jax.ShapeDtypeStruct((depth, 3, ncb, nh, nq, nk), f32),
        compiler_params=_cparams(("parallel", "parallel")),
        name="na_bias",
    )(rp)


def _mla_body(qt_ref, k_ref, vt_ref, o_ref, s_scr, *, nq, nk):
    tq, tk = MLA_TQ, MLA_TK
    ones = jnp.ones((MLA_VROWS - MLA_V, MLA_KC), bf16)
    chains = [(h, c) for h in range(2) for c in range(tq // MLA_CW)]
    nch = len(chains)

    def scores(t, slot):
        qoff = pl.multiple_of((t // nk) * tq, tq)
        koff = pl.multiple_of((t % nk) * tk, tk)
        qt = qt_ref[:, pl.ds(qoff, tq)]
        kb = k_ref[pl.ds(koff, tk), :]
        cmax = []
        for n, (h, c) in enumerate(chains):
            s = jnp.dot(kb[:, h * MLA_PAD:(h + 1) * MLA_PAD],
                        qt[h * MLA_PAD:(h + 1) * MLA_PAD, c * MLA_CW:(c + 1) * MLA_CW],
                        preferred_element_type=f32)
            s_scr[slot, n] = s
            cmax.append(jnp.max(s, axis=0, keepdims=True))
        return tuple(cmax)

    nkc = tk // MLA_KC

    def step(t_s, slot_s, t_a, slot_a, cmax, state):
        if t_s is not None:
            qoff_s = pl.multiple_of((t_s // nk) * tq, tq)
            koff_s = pl.multiple_of((t_s % nk) * tk, tk)
            qt = qt_ref[:, pl.ds(qoff_s, tq)]
        j = t_a % nk
        qoff = pl.multiple_of((t_a // nk) * tq, tq)
        koff = pl.multiple_of(j * tk, tk)
        tile_start = j == 0
        mns, als = [], []
        for n in range(nch):
            m = jnp.where(tile_start, -jnp.inf, state[2 * n])
            mns.append(jnp.maximum(m, cmax[n]))
            als.append(jnp.exp2(m - mns[n]))
        parts, cnew = [None] * nch, [None] * nch
        for kc in range(nkc):
            rows = slice(kc * MLA_KC, (kc + 1) * MLA_KC)
            if t_s is not None:
                kb = k_ref[pl.ds(koff_s + kc * MLA_KC, MLA_KC), :]
                for n, (h, c) in enumerate(chains):
                    s = jnp.dot(kb[:, h * MLA_PAD:(h + 1) * MLA_PAD],
                                qt[h * MLA_PAD:(h + 1) * MLA_PAD, c * MLA_CW:(c + 1) * MLA_CW],
                                preferred_element_type=f32)
                    s_scr[slot_s, n, rows, :] = s
                    cm = jnp.max(s, axis=0, keepdims=True)
                    cnew[n] = cm if cnew[n] is None else jnp.maximum(cnew[n], cm)
            vt = vt_ref[:, pl.ds(koff + kc * MLA_KC, MLA_KC)]
            for n, (h, c) in enumerate(chains):
                p = jnp.exp2(s_scr[slot_a, n, rows, :] - mns[n]).astype(bf16)
                vte = jnp.concatenate([vt[h * MLA_V:(h + 1) * MLA_V], ones], axis=0)
                d = jnp.dot(vte, p, preferred_element_type=f32)
                parts[n] = d if parts[n] is None else parts[n] + d
        new = []
        for n, (h, c) in enumerate(chains):
            acc = als[n] * state[2 * n + 1] + parts[n]
            o_ref[h * MLA_V:(h + 1) * MLA_V, pl.ds(qoff + c * MLA_CW, MLA_CW)] = (
                acc[:MLA_V] / acc[MLA_V:MLA_V + 1]).astype(bf16)
            new += [mns[n], acc]
        return (tuple(cnew) if t_s is not None else ()), tuple(new)

    nt = nq * nk
    ahead, ns = MLA_AHEAD, MLA_SLOTS

    def trip(t0, carry, last):
        pend, state = list(carry[:nch * ahead]), carry[nch * ahead:]
        for u in range(MLA_TRIP):
            t_s = None if (last and u + ahead >= MLA_TRIP) else t0 + u + ahead
            cn, state = step(t_s, (u + ahead) % ns, t0 + u, u % ns, tuple(pend[:nch]), state)
            pend = pend[nch:] + list(cn)
        return tuple(pend) + state

    m0 = jnp.full((1, MLA_CW), -jnp.inf, f32)
    a0 = jnp.zeros((MLA_VROWS, MLA_CW), f32)
    carry = ()
    for t in range(ahead):
        carry += scores(t, t)
    carry = lax.fori_loop(0, nt // MLA_TRIP - 1, lambda i, c: trip(i * MLA_TRIP, c, False), carry + (m0, a0) * nch)
    trip(nt - MLA_TRIP, carry, True)


def _mla_attention(mqt, mk, mvt):
    B, L, _ = mk.shape
    npair = MLA_HEADS // 2
    nq, nk = L // MLA_TQ, L // MLA_TK
    assert (nq * nk) % MLA_TRIP == 0 and MLA_TRIP % MLA_SLOTS == 0
    mode = dict(pipeline_mode=pl.Buffered(1)) if L * 2 * MLA_PAD * mk.dtype.itemsize > SINGLE_BUFFER_BYTES else {}
    return pl.pallas_call(
        functools.partial(_mla_body, nq=nq, nk=nk),
        grid=(B, npair),
        in_specs=[pl.BlockSpec((None, 2 * MLA_PAD, L), lambda b, p: (b, p, 0), **mode),
                  pl.BlockSpec((None, L, 2 * MLA_PAD), lambda b, p: (b, 0, p), **mode),
                  pl.BlockSpec((None, 2 * MLA_V, L), lambda b, p: (b, p, 0), **mode)],
        out_specs=pl.BlockSpec((None, 2 * MLA_V, L), lambda b, p: (b, p, 0)),
        out_shape=jax.ShapeDtypeStruct((B, MLA_HEADS * MLA_V, L), bf16),
        scratch_shapes=[pltpu.VMEM((MLA_SLOTS, 2 * MLA_TQ // MLA_CW, MLA_TK, MLA_CW), f32)],
        compiler_params=_cparams(("parallel", "parallel")),
        name="mla_attn",
    )(mqt, mk, mvt)


def _toeplitz_body(g_ref, pf_ref, o_ref):
    kern = jnp.dot(g_ref[...], pf_ref[...], precision=lax.Precision.HIGHEST, preferred_element_type=f32)
    bits = lax.bitcast_convert_type(kern.astype(bf16).astype(f32), jnp.int32)
    for cp in range(SSM_GC):
        for c in range(0, SSM_GC, 2):
            r = cp * SSM_GC + c
            packed = lax.shift_right_logical(bits[r:r + 1, :], 16) | bits[r + 1:r + 2, :]
            rowb = jnp.broadcast_to(packed, (LANES, 2 * LANES))
            toe = pltpu.roll(rowb, LANES + 1, 1, stride=1, stride_axis=0)[:, :LANES]
            even = lax.bitcast_convert_type(lax.shift_left(toe, 16), f32)
            odd = lax.bitcast_convert_type(toe & jnp.int32(-65536), f32)
            o_ref[cp * LANES:(cp + 1) * LANES, c * LANES:(c + 1) * LANES] = even.astype(bf16)
            o_ref[cp * LANES:(cp + 1) * LANES, (c + 1) * LANES:(c + 2) * LANES] = odd.astype(bf16)


def _toeplitz_gen(gm, pf):
    dg = pf.shape[0]
    n = SSM_GC * LANES
    k = 4 * SSM_STATE
    return pl.pallas_call(
        _toeplitz_body,
        grid=(dg,),
        in_specs=[pl.BlockSpec((None, SSM_GC * SSM_GC, k), lambda g: (g, 0, 0)),
                  pl.BlockSpec((None, k, 2 * LANES), lambda g: (g, 0, 0))],
        out_specs=pl.BlockSpec((None, n, n), lambda g: (g, 0, 0)),
        out_shape=jax.ShapeDtypeStruct((dg, n, n), bf16),
        compiler_params=_cparams(("parallel",)),
        name="s5_toeplitz",
    )(gm, pf)


def _ssm_body(*refs, segs):
    ns = len(segs)
    u_refs, (t_ref, wv_ref, wc_ref, a_ref, d_ref) = refs[:ns], refs[ns:ns + 5]
    o_refs = refs[ns + 5:2 * ns + 5]
    vre, vim, xfre, xfim, xbre, xbim = refs[2 * ns + 5:]
    u32 = jnp.concatenate([jnp.concatenate([u_ref[:, c, :] for c in range(SSM_GC)], axis=1) for u_ref in u_refs],
                          axis=0)
    m = u32.shape[0]
    ub = u32.astype(bf16)
    y = jnp.dot(ub, t_ref[...], preferred_element_type=f32)
    v = jnp.dot(ub, wv_ref[...], preferred_element_type=f32)
    vre[...] = v[:, :LANES]
    vim[...] = v[:, LANES:]
    are = a_ref[0:1, :]
    aim = a_ref[1:2, :]
    row0 = 0
    for nb, nj in segs:
        is_fwd = lax.broadcasted_iota(jnp.int32, (nb, LANES), 1) < SSM_STATE

        def rows(j, nb=nb, nj=nj, row0=row0):
            return pl.ds(row0 + j, nb, stride=nj) if nb > 1 else pl.ds(row0 + j, 1)

        def step(k, carry, nj=nj, rows=rows, is_fwd=is_fwd):
            xr, xi = carry
            rf, rb = rows(k), rows(nj - 1 - k)
            xfre[rf, :] = xr
            xfim[rf, :] = xi
            xbre[rb, :] = xr
            xbim[rb, :] = xi
            vr = jnp.where(is_fwd, vre[rf, :], vre[rb, :])
            vi = jnp.where(is_fwd, vim[rf, :], vim[rb, :])
            return are * xr - aim * xi + vr, are * xi + aim * xr + vi

        z = jnp.zeros((nb, LANES), f32)
        lax.fori_loop(0, nj, step, (z, z))
        row0 += nb * nj
    fwd_m = lax.broadcasted_iota(jnp.int32, (m, LANES), 1) < SSM_STATE
    xin = jnp.concatenate([jnp.where(fwd_m, xfre[...], xbre[...]),
                           jnp.where(fwd_m, xfim[...], xbim[...])], axis=1).astype(bf16)
    y = y + jnp.dot(xin, wc_ref[...], preferred_element_type=f32) + u32 * d_ref[...]
    row0 = 0
    for (nb, nj), o_ref in zip(segs, o_refs):
        for c in range(SSM_GC):
            o_ref[:, c, :] = y[row0:row0 + nb * nj, c * LANES:(c + 1) * LANES]
        row0 += nb * nj


def _ssm(uts, layer, S, segs):
    n = SSM_GC * LANES
    ms = [nb * nj for nb, nj in segs]
    gspec = lambda tail: pl.BlockSpec((None,) + tail, lambda g: (layer * SSM_GROUPS + g,) + (0,) * len(tail))
    io = [pl.BlockSpec((m, None, SSM_GC, LANES), lambda g: (0, g, 0, 0)) for m in ms]
    return pl.pallas_call(
        functools.partial(_ssm_body, segs=tuple(segs)),
        grid=(SSM_GROUPS,),
        in_specs=io + [gspec((n, n)), gspec((n, 4 * SSM_STATE)), gspec((4 * SSM_STATE, n)),
                       gspec((2, LANES)), gspec((1, n))],
        out_specs=io,
        out_shape=[jax.ShapeDtypeStruct((m, SSM_GROUPS, SSM_GC, LANES), f32) for m in ms],
        scratch_shapes=[pltpu.VMEM((sum(ms), LANES), f32)] * 6,
        compiler_params=_cparams(("parallel",)),
        name="s5_conv",
    )(*uts, S["toe"], S["wv"], S["wc"], S["a"], S["d"])


def _ssm_tables(a_re, a_im, b_re, b_im, c_re, c_im, log_dt, ssm_d):
    depth = a_re.shape[0]
    dg = depth * SSM_GROUPS
    lam = lax.complex(a_re.astype(f32), a_im.astype(f32))
    dt = jnp.exp(log_dt.astype(f32))[..., None]
    lam_dt = lam * dt
    lam_bar = jnp.exp(lam_dt)
    b_bar = ((lam_bar - 1.0) / lam)[..., None] * lax.complex(b_re.astype(f32), b_im.astype(f32))
    cmat = lax.complex(c_re.astype(f32), c_im.astype(f32))
    steps = jnp.arange(LANES + 1, dtype=f32)
    pw = jnp.exp(lam_dt[..., None] * steps)
    pwf, pwb = pw[:, 0], pw[:, 1]

    e = np.arange(2 * LANES)
    f_ok = (e >= LANES - 1) & (e <= 2 * LANES - 2)
    b_ok = e <= LANES - 1
    pf_f = jnp.where(f_ok, pwf[..., np.clip(e - (LANES - 1), 0, LANES - 1)], 0.0)
    pf_b = jnp.where(b_ok, pwb[..., np.clip(LANES - 1 - e, 0, LANES - 1)], 0.0)
    pf = jnp.stack([pf_f.real, pf_f.imag, pf_b.real, pf_b.imag], axis=2)
    pf = pf.reshape(dg, 4 * SSM_STATE, 2 * LANES)

    bbf, bbb = b_bar[:, 0], b_bar[:, 1]
    cf, cb = cmat[:, 0], cmat[:, 1]
    gf = cf[:, :, None, :, :] * jnp.swapaxes(bbf, -1, -2)[:, :, :, None, :]
    gb = cb[:, :, None, :, :] * jnp.swapaxes(bbb, -1, -2)[:, :, :, None, :]
    gm = jnp.concatenate([gf.real, -gf.imag, gb.real, -gb.imag], axis=-1)
    gm = gm.reshape(dg, SSM_GC * SSM_GC, 4 * SSM_STATE)

    s_idx = np.arange(LANES)
    pvf = jnp.swapaxes(pwf[..., LANES - 1 - s_idx], -1, -2)
    pvb = jnp.swapaxes(pwb[..., s_idx], -1, -2)
    pv1 = jnp.concatenate([pvf.real, pvb.real, pvf.imag, pvb.imag], axis=-1).reshape(dg, LANES, 4 * SSM_STATE)
    pv2 = jnp.concatenate([pvf.imag, pvb.imag, pvf.real, pvb.real], axis=-1).reshape(dg, LANES, 4 * SSM_STATE)
    bft, bbt = jnp.swapaxes(bbf, -1, -2), jnp.swapaxes(bbb, -1, -2)
    bv = jnp.stack([jnp.concatenate([bft.real, bbt.real, bft.real, bbt.real], axis=-1),
                    jnp.concatenate([-bft.imag, -bbt.imag, bft.imag, bbt.imag], axis=-1)], axis=3)
    bv = bv.reshape(dg, SSM_GC, 2, 4 * SSM_STATE)
    t_idx = np.arange(LANES)
    pcf, pcb = pwf[..., t_idx + 1], pwb[..., LANES - t_idx]
    pc = jnp.stack([pcf.real, pcf.imag, pcb.real, pcb.imag], axis=2).reshape(dg, 4, SSM_STATE, LANES)
    cft, cbt = jnp.swapaxes(cf, -1, -2), jnp.swapaxes(cb, -1, -2)
    ct = jnp.stack([cft.real, cft.imag, cbt.real, cbt.imag], axis=2).reshape(dg, 4, SSM_STATE, SSM_GC)

    a128 = jnp.concatenate([pwf[..., LANES], pwb[..., LANES]], axis=-1)
    a = jnp.stack([a128.real, a128.imag], axis=2).reshape(dg, 2, LANES)
    d = jnp.repeat(ssm_d.astype(f32).reshape(dg, SSM_GC), LANES, axis=-1).reshape(dg, 1, SSM_GC * LANES)
    return dict(pf=pf, gm=gm, pv1=pv1, pv2=pv2, bv=bv, pc=pc, ct=ct, a=a, d=d)


def _state_tables_body(pv1_ref, pv2_ref, bv_ref, pc_ref, ct_ref, wv_ref, wc_ref):
    pv1, pv2 = pv1_ref[...], pv2_ref[...]
    for cp in range(SSM_GC):
        wv_ref[cp * LANES:(cp + 1) * LANES, :] = (pv1 * bv_ref[cp, 0:1, :] + pv2 * bv_ref[cp, 1:2, :]).astype(bf16)
    pfr, pfi, pbr, pbi = pc_ref[0], pc_ref[1], pc_ref[2], pc_ref[3]
    cfr, cfi, cbr, cbi = ct_ref[0], ct_ref[1], ct_ref[2], ct_ref[3]
    for c in range(SSM_GC):
        col = slice(c, c + 1)
        tile = jnp.concatenate([cfr[:, col] * pfr - cfi[:, col] * pfi,
                                cbr[:, col] * pbr - cbi[:, col] * pbi,
                                -(cfr[:, col] * pfi + cfi[:, col] * pfr),
                                -(cbr[:, col] * pbi + cbi[:, col] * pbr)], axis=0)
        wc_ref[:, c * LANES:(c + 1) * LANES] = tile.astype(bf16)


def _state_tables(S):
    dg = S["pf"].shape[0]
    n, k = SSM_GC * LANES, 4 * SSM_STATE
    g3 = lambda a, b: pl.BlockSpec((None, a, b), lambda g: (g, 0, 0))
    g4 = lambda a, b, c: pl.BlockSpec((None, a, b, c), lambda g: (g, 0, 0, 0))
    return pl.pallas_call(
        _state_tables_body,
        grid=(dg,),
        in_specs=[g3(LANES, k), g3(LANES, k), g4(SSM_GC, 2, k), g4(4, SSM_STATE, LANES), g4(4, SSM_STATE, SSM_GC)],
        out_specs=(g3(n, k), g3(k, n)),
        out_shape=(jax.ShapeDtypeStruct((dg, n, k), bf16), jax.ShapeDtypeStruct((dg, k, n), bf16)),
        compiler_params=_cparams(("parallel",)),
        name="s5_state_tables",
    )(S["pv1"], S["pv2"], S["bv"], S["pc"], S["ct"])


def _mixffn_body(x_ref, mod_ref, yna_ref, ymlat_ref, yst_ref, gluw_ref, glub_ref, wo_ref, g2_ref,
                 wg_ref, wu_ref, wd_ref, o_ref, act_ref):
    ys = jnp.concatenate([yst_ref[c].reshape(SSM_WIDTH, LANES).T for c in range(yst_ref.shape[0])],
                         axis=0)
    ys = 0.5 * ys * (1.0 + jnp.tanh(math.sqrt(2.0 / math.pi) * (ys + 0.044715 * (ys * ys * ys))))
    gate = jax.nn.sigmoid(jnp.dot(ys.astype(bf16), gluw_ref[...], preferred_element_type=f32) + glub_ref[...])
    yssm = (ys * gate).astype(bf16)
    o1 = NA_WIDTH
    o2 = NA_WIDTH + MLA_HEADS * MLA_V
    mix = (jnp.dot(yna_ref[...], wo_ref[0:o1, :], preferred_element_type=f32)
           + lax.dot_general(ymlat_ref[...], wo_ref[o1:o2, :], _TN, preferred_element_type=f32)
           + jnp.dot(yssm, wo_ref[o2:, :], preferred_element_type=f32))
    x1 = x_ref[...] + mod_ref[2:3, :] * mix
    o_ref[...] = x1
    ms = jnp.mean(x1 * x1, axis=-1, keepdims=True)
    h2 = ((x1 * lax.rsqrt(ms + RMS_EPS) * g2_ref[...]) * (1.0 + mod_ref[4:5, :]) + mod_ref[3:4, :]).astype(bf16)

    th = MXU_TILE
    for c in range(FFN_HIDDEN // th):
        cs = slice(c * th, (c + 1) * th)
        g = jnp.dot(h2, wg_ref[:, cs], preferred_element_type=f32)
        u = jnp.dot(h2, wu_ref[:, cs], preferred_element_type=f32)
        act_ref[:, cs] = (g * jax.nn.sigmoid(g) * u).astype(bf16)
    ffn = jnp.dot(act_ref[...], wd_ref[...], preferred_element_type=f32)
    o_ref[...] = o_ref[...] + mod_ref[5:6, :] * ffn


def _mixffn(x, mod, yna, ymlat, yst, layer, P):
    B, L, D = x.shape
    tm = TOKEN_TILE
    nt = L // tm
    cpt = tm // LANES
    tok = lambda w: pl.BlockSpec((None, tm, w), lambda b, i: (b, i, 0))

    def wspec(name):
        tail = P[name].shape[1:]
        return pl.BlockSpec((None,) + tail, lambda b, i: (layer,) + (0,) * len(tail), pipeline_mode=pl.Buffered(1))

    return pl.pallas_call(
        _mixffn_body,
        grid=(B, nt),
        in_specs=[tok(D), pl.BlockSpec((None, 6, D), lambda b, i: (b, 0, 0)), tok(NA_WIDTH),
                  pl.BlockSpec((None, MLA_HEADS * MLA_V, tm), lambda b, i: (b, 0, i)),
                  pl.BlockSpec((cpt, SSM_GROUPS, SSM_GC, LANES), lambda b, i: (b * nt + i, 0, 0, 0)),
                  wspec("glu_w"), wspec("glu_b"), wspec("w_out"), wspec("g2"),
                  wspec("w_gate"), wspec("w_up"), wspec("w_down")],
        out_specs=tok(D),
        out_shape=jax.ShapeDtypeStruct((B, L, D), f32),
        scratch_shapes=[pltpu.VMEM((tm, FFN_HIDDEN), bf16)],
        compiler_params=_cparams(("parallel", "parallel")),
        name="mix_ffn",
    )(x, mod, yna, ymlat, yst, P["glu_w"], P["glu_b"], P["w_out"], P["g2"], P["w_gate"], P["w_up"], P["w_down"])


def _prep_params(norm1_g, w_in, na_q_g, na_k_g, mla_cq_g, mla_ckv_g, mla_w_uq, mla_w_ukv, mla_qn_g, mla_kn_g,
                 mla_qr_g, mla_kr_g, glu_w, glu_b, w_out, norm2_g, ffn_w_gate, ffn_w_up, ffn_w_down):
    depth = w_in.shape[0]
    o1 = 3 * NA_WIDTH
    o2 = o1 + MLA_Q_LORA
    o3 = o2 + MLA_KV_LORA
    o4 = o3 + MLA_ROPE
    row = lambda a: a.astype(f32)[:, None, :]
    zeros = lambda *s: jnp.zeros(s, f32)
    P = {}
    P["g1"] = row(norm1_g)
    P["g2"] = row(norm2_g)
    P["w_na"] = w_in[:, :, :o1].astype(bf16)
    P["w_c"] = jnp.concatenate([w_in[:, :, o1:o3], zeros(depth, D_MODEL, MLA_NOPE), w_in[:, :, o3:o4],
                                zeros(depth, D_MODEL, MLA_PAD - MLA_NOPE - MLA_ROPE)], axis=-1).astype(bf16)
    P["w_ut"] = jnp.swapaxes(w_in[:, :, o4:], 1, 2).astype(bf16)
    wq = mla_w_uq.reshape(depth, MLA_Q_LORA, MLA_HEADS, MLA_NOPE + MLA_ROPE)
    wq = jnp.concatenate([wq, zeros(depth, MLA_Q_LORA, MLA_HEADS, MLA_PAD - MLA_NOPE - MLA_ROPE)], axis=-1)
    P["w_uq"] = wq.reshape(depth, MLA_Q_LORA, MLA_HEADS * MLA_PAD).astype(bf16)
    wkv = mla_w_ukv.reshape(depth, MLA_KV_LORA, MLA_HEADS, MLA_NOPE + MLA_V)
    wk = jnp.concatenate([wkv[..., :MLA_NOPE], zeros(depth, MLA_KV_LORA, MLA_HEADS, MLA_PAD - MLA_NOPE)], axis=-1)
    P["w_ukv_k"] = wk.reshape(depth, MLA_KV_LORA, MLA_HEADS * MLA_PAD).astype(bf16)
    P["w_ukv_vt"] = jnp.swapaxes(wkv[..., MLA_NOPE:].reshape(depth, MLA_KV_LORA, MLA_HEADS * MLA_V), 1, 2).astype(bf16)
    P["gq_na"] = row(jnp.tile(na_q_g, (1, NA_HEADS))) * (NA_HEAD_DIM ** -0.5 * math.log2(math.e))
    P["gk_na"] = row(jnp.tile(na_k_g, (1, NA_HEADS)))
    P["g_cq"] = row(mla_cq_g)
    P["g_ckv"] = row(mla_ckv_g)
    scale = (MLA_NOPE + MLA_ROPE) ** -0.5 * math.log2(math.e)
    pad = MLA_PAD - MLA_NOPE - MLA_ROPE
    gq = jnp.concatenate([mla_qn_g, mla_qr_g, zeros(depth, pad)], axis=-1) * scale
    P["gq"] = row(jnp.tile(gq, (1, MLA_HEADS)))
    gk = jnp.concatenate([mla_kn_g, zeros(depth, MLA_PAD - MLA_NOPE)], axis=-1)
    P["gk"] = row(jnp.tile(gk, (1, MLA_HEADS)))
    P["gkr"] = row(jnp.concatenate([zeros(depth, MLA_NOPE), mla_kr_g, zeros(depth, pad)], axis=-1))
    invq = np.concatenate([np.full(MLA_NOPE, 1.0 / MLA_NOPE), np.full(MLA_ROPE, 1.0 / MLA_ROPE), np.zeros(pad)])
    invk = np.concatenate([np.full(MLA_NOPE, 1.0 / MLA_NOPE), np.zeros(MLA_PAD - MLA_NOPE)])
    P["invq"] = jnp.asarray(np.tile(invq, MLA_HEADS)[None, :], f32)
    P["invk"] = jnp.asarray(np.tile(invk, MLA_HEADS)[None, :], f32)
    lane = np.arange(NA_WIDTH)
    P["G_na"] = jnp.asarray((lane[:, None] // NA_HEAD_DIM) == (lane[None, :] // NA_HEAD_DIM), bf16)
    lane = np.arange(2 * MLA_PAD)
    grp = np.where(lane % MLA_PAD < MLA_NOPE, 0, np.where(lane % MLA_PAD < MLA_NOPE + MLA_ROPE, 1, 2))
    same = (lane[:, None] // MLA_PAD == lane[None, :] // MLA_PAD) & (grp[:, None] == grp[None, :]) & (grp[:, None] < 2)
    P["G_qm"] = jnp.asarray(same, bf16)
    P["glu_w"] = glu_w.astype(bf16)
    P["glu_b"] = row(glu_b)
    P["w_out"] = w_out.astype(bf16)
    P["w_gate"] = ffn_w_gate.astype(bf16)
    P["w_up"] = ffn_w_up.astype(bf16)
    P["w_down"] = ffn_w_down.astype(bf16)
    return P


def _rope_tables(length):
    inv = 1.0 / (ROPE_BASE ** (jnp.arange(0, MLA_ROPE, 2, dtype=f32) / MLA_ROPE))
    ang = jnp.arange(length, dtype=f32)[:, None] * inv[None, :]
    cos, sin = jnp.cos(ang), jnp.sin(ang)
    half = MLA_ROPE // 2
    z = lambda w: jnp.zeros((length, w), f32)
    pad = MLA_PAD - MLA_NOPE - MLA_ROPE
    cos_t = jnp.concatenate([jnp.ones((length, MLA_NOPE), f32), cos, cos, z(pad)], axis=-1)
    sin_lo = jnp.concatenate([z(MLA_NOPE), -sin, z(half), z(pad)], axis=-1)
    sin_hi = jnp.concatenate([z(MLA_NOPE), z(half), sin, z(pad)], axis=-1)
    return cos_t, sin_lo, sin_hi


def _trunks(xs, mods, P, S, bias):
    xs = list(xs)
    ropes = [_rope_tables(x.shape[1]) for x in xs]
    segs = [(x.shape[0], x.shape[1] // LANES) for x in xs]
    for layer in range(len(mods[0])):
        heads, uts = [], []
        for x, mod, rope in zip(xs, mods, ropes):
            naq, nak, nav, mqt, mk, mvt, ut = _inproj(x, mod[layer], layer, P, rope)
            heads.append((_na_attention(naq, nak, nav, bias, layer), _mla_attention(mqt, mk, mvt)))
            uts.append(ut)
        ysts = _ssm(uts, layer, S, segs)
        xs = [_mixffn(x, mod[layer], yna, ymla, yst, layer, P)
              for x, mod, (yna, ymla), yst in zip(xs, mods, heads, ysts)]
    return xs


def kernel(x_prompt, x_sample, c_prompt, c_sample, ada_w, ada_b, norm1_g, w_in, na_q_g, na_k_g, na_rpb, mla_cq_g,
           mla_ckv_g, mla_w_uq, mla_w_ukv, mla_qn_g, mla_kn_g, mla_qr_g, mla_kr_g, ssm_a_re, ssm_a_im, ssm_b_re,
           ssm_b_im, ssm_c_re, ssm_c_im, ssm_log_dt, ssm_d, glu_w, glu_b, w_out, norm2_g, ffn_w_gate, ffn_w_up,
           ffn_w_down):
    depth = w_in.shape[0]
    nbp, nbs = c_prompt.shape[0], c_sample.shape[0]
    rows = -(-(nbp + nbs) // 8) * 8
    c_all = jnp.concatenate([c_prompt, c_sample, jnp.zeros((rows - nbp - nbs, D_MODEL), f32)], axis=0)
    mod = _modulation(c_all, ada_w, ada_b).reshape(depth, rows, 6, D_MODEL)
    mods_p = [mod[l, :nbp] for l in range(depth)]
    mods_s = [mod[l, nbp:nbp + nbs] for l in range(depth)]

    P = _prep_params(norm1_g, w_in, na_q_g, na_k_g, mla_cq_g, mla_ckv_g, mla_w_uq, mla_w_ukv, mla_qn_g, mla_kn_g,
                     mla_qr_g, mla_kr_g, glu_w, glu_b, w_out, norm2_g, ffn_w_gate, ffn_w_up, ffn_w_down)
    S = _ssm_tables(ssm_a_re, ssm_a_im, ssm_b_re, ssm_b_im, ssm_c_re, ssm_c_im, ssm_log_dt, ssm_d)
    S["toe"] = _toeplitz_gen(S["gm"], S["pf"])
    S["wv"], S["wc"] = _state_tables(S)
    bias = _na_bias_tables(na_rpb)

    y_prompt, y_sample = _trunks((x_prompt, x_sample), (mods_p, mods_s), P, S, bias)
    return (y_prompt, y_sample)
```

```python
import functools
import math

import numpy as np
import jax
import jax.numpy as jnp
from jax import lax
from jax.experimental import pallas as pl
from jax.experimental.pallas import tpu as pltpu

f32 = jnp.float32
bf16 = jnp.bfloat16

D_MODEL = 1024
GRID_W = 64
NA_HEADS = 4
NA_HEAD_DIM = 64
NA_WIDTH = NA_HEADS * NA_HEAD_DIM
NA_ROWS = 8
NA_COLS = 16
MLA_HEADS = 8
MLA_NOPE = 64
MLA_ROPE = 32
MLA_V = 64
MLA_Q_LORA = 384
MLA_KV_LORA = 256
ROPE_BASE = 10000.0
SSM_GROUPS = 16
SSM_GC = 16
SSM_STATE = 64
SSM_WIDTH = SSM_GROUPS * SSM_GC
FFN_HIDDEN = 2816
RMS_EPS = 1e-6
NEG_INF = -1e30

LANES = 128
MLA_PAD = 128
NA_QCOLS = 16
NA_KCOLS = 2 * NA_COLS
NA_QROWS = 4
NA_SUB = 4
NA_WROWS = NA_QROWS + NA_ROWS
NA_BIAS_LANE0 = 200
TOKEN_TILE = 512
MXU_TILE = 256
BF16_SUBLANES = 16
MLA_CW = MXU_TILE
MLA_TQ = MLA_CW
MLA_TK = 1024
MLA_KC = MXU_TILE
MLA_AHEAD = 2
MLA_SLOTS = 2 * MLA_AHEAD
MLA_TRIP = 32
MLA_VROWS = MLA_V + BF16_SUBLANES
MOD_TILE = 1536
VMEM_LIMIT = 48 * 1024 * 1024
SINGLE_BUFFER_BYTES = 4 * 1024 * 1024

_NT = (((1,), (1,)), ((), ()))
_TN = (((0,), (0,)), ((), ()))


def _cparams(sem):
    return pltpu.CompilerParams(dimension_semantics=sem, vmem_limit_bytes=VMEM_LIMIT)


def _layer_spec(tail, layer):
    n = len(tail)
    return pl.BlockSpec((None,) + tuple(tail), lambda *_: (layer,) + (0,) * n)


def _mod_body(c_ref, w_ref, b_ref, o_ref):
    c = c_ref[...]
    s = c * jax.nn.sigmoid(c)
    o_ref[...] = jnp.dot(s, w_ref[...], precision=lax.Precision.HIGHEST, preferred_element_type=f32) + b_ref[...]


def _modulation(c_all, ada_w, ada_b):
    depth, d, n = ada_w.shape
    tn = MOD_TILE
    rows = c_all.shape[0]
    return pl.pallas_call(
        _mod_body,
        grid=(depth, n // tn),
        in_specs=[
            pl.BlockSpec((rows, d), lambda l, j: (0, 0)),
            pl.BlockSpec((None, d, tn), lambda l, j: (l, 0, j)),
            pl.BlockSpec((None, 1, tn), lambda l, j: (l, 0, j)),
        ],
        out_specs=pl.BlockSpec((None, rows, tn), lambda l, j: (l, 0, j)),
        out_shape=jax.ShapeDtypeStruct((depth, rows, n), f32),
        compiler_params=_cparams(("arbitrary", "arbitrary")),
        name="adaln_mod",
    )(c_all, ada_w, ada_b.reshape(depth, 1, n))


def _rope(x, cos, sin_lo, sin_hi):
    return x * cos + pltpu.roll(x, LANES - 16, 1) * sin_lo + pltpu.roll(x, 16, 1) * sin_hi


def _inproj_body(x_ref, mod_ref, g1_ref, wna_ref, wc_ref, wut_ref, wuq_ref, wukvk_ref, wukvvt_ref,
                 gqna_ref, gkna_ref, gcq_ref, gckv_ref, gq_ref, gk_ref, gkr_ref, invq_ref, invk_ref,
                 gna_ref, gqm_ref, cos_ref, sinlo_ref, sinhi_ref,
                 naq_ref, nak_ref, nav_ref, mqt_ref, mk_ref, mvt_ref, ut_ref):
    x = x_ref[...]
    shift1 = mod_ref[0:1, :]
    scale1 = mod_ref[1:2, :]
    ms = jnp.mean(x * x, axis=-1, keepdims=True)
    h = (x * lax.rsqrt(ms + RMS_EPS) * g1_ref[...]) * (1.0 + scale1) + shift1
    hb = h.astype(bf16)

    z = jnp.dot(hb, wna_ref[...], preferred_element_type=f32)
    q = z[:, :NA_WIDTH]
    k = z[:, NA_WIDTH:2 * NA_WIDTH]
    gna = gna_ref[...]
    ssq = jnp.dot((q * q).astype(bf16), gna, preferred_element_type=f32) * (1.0 / NA_HEAD_DIM)
    ssk = jnp.dot((k * k).astype(bf16), gna, preferred_element_type=f32) * (1.0 / NA_HEAD_DIM)
    naq_ref[...] = (q * lax.rsqrt(ssq + RMS_EPS) * gqna_ref[...]).astype(bf16)
    nak_ref[...] = (k * lax.rsqrt(ssk + RMS_EPS) * gkna_ref[...]).astype(bf16)
    nav_ref[...] = z[:, 2 * NA_WIDTH:].astype(bf16)

    zc = jnp.dot(hb, wc_ref[...], preferred_element_type=f32)
    cq = zc[:, :MLA_Q_LORA]
    ckv = zc[:, MLA_Q_LORA:MLA_Q_LORA + MLA_KV_LORA]
    cqn = (cq * lax.rsqrt(jnp.mean(cq * cq, axis=-1, keepdims=True) + RMS_EPS) * gcq_ref[...]).astype(bf16)
    ckvn = (ckv * lax.rsqrt(jnp.mean(ckv * ckv, axis=-1, keepdims=True) + RMS_EPS) * gckv_ref[...]).astype(bf16)

    cos = cos_ref[...]
    sin_lo = sinlo_ref[...]
    sin_hi = sinhi_ref[...]
    gqm = gqm_ref[...]

    kr = zc[:, MLA_Q_LORA + MLA_KV_LORA:]
    kr_ms = jnp.sum(kr * kr, axis=-1, keepdims=True) * (1.0 / MLA_ROPE)
    kr = _rope(kr * lax.rsqrt(kr_ms + RMS_EPS) * gkr_ref[...], cos, sin_lo, sin_hi)

    qraw = jnp.dot(cqn, wuq_ref[...], preferred_element_type=f32)
    kraw = jnp.dot(ckvn, wukvk_ref[...], preferred_element_type=f32)
    for p in range(MLA_HEADS // 2):
        sl = slice(2 * p * MLA_PAD, 2 * (p + 1) * MLA_PAD)
        qs = qraw[:, sl]
        ss = jnp.dot((qs * qs).astype(bf16), gqm, preferred_element_type=f32) * invq_ref[:, sl]
        qn = qs * lax.rsqrt(ss + RMS_EPS) * gq_ref[:, sl]
        ks = kraw[:, sl]
        ss = jnp.dot((ks * ks).astype(bf16), gqm, preferred_element_type=f32) * invk_ref[:, sl]
        kn = ks * lax.rsqrt(ss + RMS_EPS) * gk_ref[:, sl]
        for hh in range(2):
            lo = (2 * p + hh) * MLA_PAD
            piece = qn[:, hh * MLA_PAD:(hh + 1) * MLA_PAD]
            mqt_ref[lo:lo + MLA_PAD, :] = _rope(piece, cos, sin_lo, sin_hi).T.astype(bf16)
            mk_ref[:, lo:lo + MLA_PAD] = (kn[:, hh * MLA_PAD:(hh + 1) * MLA_PAD] + kr).astype(bf16)
    mvt_ref[...] = lax.dot_general(wukvvt_ref[...], ckvn, _NT, preferred_element_type=f32).astype(bf16)

    ut = lax.dot_general(wut_ref[...], hb, _NT, preferred_element_type=f32)
    for c in range(ut_ref.shape[0]):
        ut_ref[c] = ut[:, c * LANES:(c + 1) * LANES].reshape(SSM_GROUPS, SSM_GC, LANES)


def _inproj(x, mod, layer, P, rope_tabs):
    B, L, D = x.shape
    tm = TOKEN_TILE
    nt = L // tm
    cpt = tm // LANES
    tok = lambda w: pl.BlockSpec((None, tm, w), lambda b, i: (b, i, 0))
    pos = pl.BlockSpec((tm, LANES), lambda b, i: (i, 0))
    const2 = lambda a: pl.BlockSpec(a.shape, lambda b, i: (0, 0))
    lw = lambda name: _layer_spec(P[name].shape[1:], layer)
    names = ["g1", "w_na", "w_c", "w_ut", "w_uq", "w_ukv_k", "w_ukv_vt",
             "gq_na", "gk_na", "g_cq", "g_ckv", "gq", "gk", "gkr"]
    consts = ["invq", "invk", "G_na", "G_qm"]
    in_specs = ([tok(D), pl.BlockSpec((None, 6, D), lambda b, i: (b, 0, 0))]
                + [lw(n) for n in names] + [const2(P[n]) for n in consts] + [pos, pos, pos])
    out_shapes = (
        jax.ShapeDtypeStruct((B, L, NA_WIDTH), bf16),
        jax.ShapeDtypeStruct((B, L, NA_WIDTH), bf16),
        jax.ShapeDtypeStruct((B, L, NA_WIDTH), bf16),
        jax.ShapeDtypeStruct((B, MLA_HEADS * MLA_PAD, L), bf16),
        jax.ShapeDtypeStruct((B, L, MLA_HEADS * MLA_PAD), bf16),
        jax.ShapeDtypeStruct((B, MLA_HEADS * MLA_V, L), bf16),
        jax.ShapeDtypeStruct((B * L // LANES, SSM_GROUPS, SSM_GC, LANES), f32),
    )
    chan = lambda w: pl.BlockSpec((None, w, tm), lambda b, i: (b, 0, i))
    out_specs = (tok(NA_WIDTH), tok(NA_WIDTH), tok(NA_WIDTH), chan(MLA_HEADS * MLA_PAD), tok(MLA_HEADS * MLA_PAD),
                 chan(MLA_HEADS * MLA_V),
                 pl.BlockSpec((cpt, SSM_GROUPS, SSM_GC, LANES), lambda b, i: (b * nt + i, 0, 0, 0)))
    return pl.pallas_call(
        _inproj_body,
        grid=(B, nt),
        in_specs=in_specs,
        out_specs=out_specs,
        out_shape=out_shapes,
        compiler_params=_cparams(("parallel", "parallel")),
        name="in_proj",
    )(x, mod, *[P[n] for n in names], *[P[n] for n in consts], *rope_tabs)


def _na_key_col0(cb):
    return min(max(cb * NA_QCOLS - NA_COLS // 2, 0), GRID_W - NA_KCOLS)


def _na_body(q_ref, k_ref, v_ref, b_ref, o_ref, *, rows):
    nblk = rows // NA_QROWS
    lane = lax.broadcasted_iota(jnp.int32, (1, NA_WIDTH), 1)
    hms = [(lane >= h * NA_HEAD_DIM) & (lane < (h + 1) * NA_HEAD_DIM) for h in range(NA_HEADS)]
    nq = NA_QROWS * NA_QCOLS
    for sb in range(NA_SUB):
        blk = pl.program_id(1) * NA_SUB + sb
        variant = jnp.where(blk == 0, 0, jnp.where(blk == nblk - 1, 2, 1))
        ws = jnp.clip(NA_QROWS * blk - NA_ROWS // 2, 0, rows - NA_WROWS)
        base = pl.multiple_of(ws * GRID_W, GRID_W)
        q0 = sb * NA_QROWS * GRID_W
        for cb in range(GRID_W // NA_QCOLS):
            kc0 = _na_key_col0(cb)
            kw = jnp.concatenate([k_ref[pl.ds(base + kl * GRID_W + kc0, NA_KCOLS), :] for kl in range(NA_WROWS)],
                                 axis=0)
            vw = jnp.concatenate([v_ref[pl.ds(base + kl * GRID_W + kc0, NA_KCOLS), :] for kl in range(NA_WROWS)],
                                 axis=0)
            qrows = [slice(q0 + rl * GRID_W + cb * NA_QCOLS, q0 + rl * GRID_W + (cb + 1) * NA_QCOLS)
                     for rl in range(NA_QROWS)]
            qb = jnp.concatenate([q_ref[r, :] for r in qrows], axis=0)
            qs = jnp.concatenate([jnp.where(hm, qb, jnp.zeros_like(qb)) for hm in hms], axis=0)
            s = lax.dot_general(qs, kw, _NT, preferred_element_type=f32) + b_ref[variant, cb].reshape(NA_HEADS * nq, -1)
            m = jnp.max(s, axis=-1, keepdims=True)
            p = jnp.exp2(s - m)
            l = jnp.sum(p, axis=-1, keepdims=True)
            o = jnp.dot(p.astype(bf16), vw, preferred_element_type=f32) / l
            ob = o[:nq]
            for h in range(1, NA_HEADS):
                ob = jnp.where(hms[h], o[h * nq:(h + 1) * nq], ob)
            for rl, r in enumerate(qrows):
                o_ref[r, :] = ob[rl * NA_QCOLS:(rl + 1) * NA_QCOLS].astype(bf16)


def _na_attention(q, k, v, bias, layer):
    B, L, W = q.shape
    rows = L // GRID_W
    tq = NA_SUB * NA_QROWS * GRID_W
    ncb = GRID_W // NA_QCOLS
    mode = dict(pipeline_mode=pl.Buffered(1)) if L * W * q.dtype.itemsize > SINGLE_BUFFER_BYTES else {}
    full = pl.BlockSpec((None, L, W), lambda b, i: (b, 0, 0), **mode)
    return pl.pallas_call(
        functools.partial(_na_body, rows=rows),
        grid=(B, L // tq),
        in_specs=[pl.BlockSpec((None, tq, W), lambda b, i: (b, i, 0)), full, full,
                  pl.BlockSpec((None, 3, ncb, NA_HEADS, NA_QROWS * NA_QCOLS, NA_WROWS * NA_KCOLS),
                               lambda b, i: (layer, 0, 0, 0, 0, 0), pipeline_mode=pl.Buffered(1))],
        out_specs=pl.BlockSpec((None, tq, W), lambda b, i: (b, i, 0)),
        out_shape=jax.ShapeDtypeStruct((B, L, W), bf16),
        compiler_params=_cparams(("parallel", "arbitrary")),
        name="na_attn",
    )(q, k, v, bias)


def _na_bias_body(r_ref, o_ref):
    r = r_ref[...]
    qc = lax.broadcasted_iota(jnp.int32, (NA_QCOLS, LANES), 0)
    lane = lax.broadcasted_iota(jnp.int32, (NA_QCOLS, LANES), 1)
    piece = lane // NA_KCOLS
    neg = jnp.full((NA_QCOLS, LANES), NEG_INF, f32)
    npc = LANES // NA_KCOLS
    for cb in range(GRID_W // NA_QCOLS):
        kc0 = _na_key_col0(cb)
        qcol = cb * NA_QCOLS + qc
        kcol = kc0 + lane - piece * NA_KCOLS
        cs = jnp.clip(qcol - NA_COLS // 2, 0, GRID_W - NA_COLS)
        cvalid = (kcol >= cs) & (kcol < cs + NA_COLS)
        tiles = []
        for dr in range(2 * NA_ROWS - 1):
            rowb = jnp.broadcast_to(r[dr:dr + 1, :], (NA_QCOLS, 2 * LANES))
            t = neg
            for k in range(npc):
                base = 2 * LANES - (NA_COLS - 1) - NA_BIAS_LANE0 + cb * NA_QCOLS - kc0 + k * NA_KCOLS
                rolled = pltpu.roll(rowb, base, 1, stride=1, stride_axis=0)[:, :LANES]
                t = jnp.where(piece == k, rolled, t)
            tiles.append(jnp.where(cvalid, t, neg))
        for v, (off, lo) in enumerate(((0, 0), (-(NA_ROWS // 2), None), (-NA_ROWS, NA_ROWS // 2))):
            for rl in range(NA_QROWS):
                vregs = []
                for j in range(NA_WROWS // npc):
                    t = neg
                    for k in range(npc):
                        kl = j * npc + k
                        d = off + kl - rl
                        ok = (-(NA_ROWS // 2) <= d < NA_ROWS // 2) if lo is None else (lo <= kl < lo + NA_ROWS)
                        if ok:
                            t = jnp.where(piece == k, tiles[d + NA_ROWS - 1], t)
                    vregs.append(t)
                o_ref[v, cb, rl * NA_QCOLS:(rl + 1) * NA_QCOLS, :] = jnp.concatenate(vregs, axis=1)


def _na_bias_tables(rpb):
    depth, nh, nr, nc = rpb.shape
    rp = jnp.pad(rpb.astype(f32) * math.log2(math.e),
                 ((0, 0), (0, 0), (0, 16 - nr), (NA_BIAS_LANE0, 2 * LANES - NA_BIAS_LANE0 - nc)))
    ncb, nq, nk = GRID_W // NA_QCOLS, NA_QROWS * NA_QCOLS, NA_WROWS * NA_KCOLS
    return pl.pallas_call(
        _na_bias_body,
        grid=(depth, nh),
        in_specs=[pl.BlockSpec((None, None, 16, 2 * LANES), lambda l, h: (l, h, 0, 0))],
        out_specs=pl.BlockSpec((None, 3, ncb, None, nq, nk), lambda l, h: (l, 0, 0, h, 0, 0)),
        out_shape=jax.ShapeDtypeStruct((depth, 3, ncb, nh, nq, nk), f32),
        compiler_params=_cparams(("parallel", "parallel")),
        name="na_bias",
    )(rp)


def _mla_body(qt_ref, k_ref, vt_ref, o_ref, s_scr, *, nq, nk):
    tq, tk = MLA_TQ, MLA_TK
    ones = jnp.ones((MLA_VROWS - MLA_V, MLA_KC), bf16)
    chains = [(h, c) for h in range(2) for c in range(tq // MLA_CW)]
    nch = len(chains)

    def scores(t, slot):
        qoff = pl.multiple_of((t // nk) * tq, tq)
        koff = pl.multiple_of((t % nk) * tk, tk)
        qt = qt_ref[:, pl.ds(qoff, tq)]
        kb = k_ref[pl.ds(koff, tk), :]
        cmax = []
        for n, (h, c) in enumerate(chains):
            s = jnp.dot(kb[:, h * MLA_PAD:(h + 1) * MLA_PAD],
                        qt[h * MLA_PAD:(h + 1) * MLA_PAD, c * MLA_CW:(c + 1) * MLA_CW],
                        preferred_element_type=f32)
            s_scr[slot, n] = s
            cmax.append(jnp.max(s, axis=0, keepdims=True))
        return tuple(cmax)

    nkc = tk // MLA_KC

    def step(t_s, slot_s, t_a, slot_a, cmax, state):
        if t_s is not None:
            qoff_s = pl.multiple_of((t_s // nk) * tq, tq)
            koff_s = pl.multiple_of((t_s % nk) * tk, tk)
            qt = qt_ref[:, pl.ds(qoff_s, tq)]
        j = t_a % nk
        qoff = pl.multiple_of((t_a // nk) * tq, tq)
        koff = pl.multiple_of(j * tk, tk)
        tile_start = j == 0
        mns, als = [], []
        for n in range(nch):
            m = jnp.where(tile_start, -jnp.inf, state[2 * n])
            mns.append(jnp.maximum(m, cmax[n]))
            als.append(jnp.exp2(m - mns[n]))
        parts, cnew = [None] * nch, [None] * nch
        for kc in range(nkc):
            rows = slice(kc * MLA_KC, (kc + 1) * MLA_KC)
            if t_s is not None:
                kb = k_ref[pl.ds(koff_s + kc * MLA_KC, MLA_KC), :]
                for n, (h, c) in enumerate(chains):
                    s = jnp.dot(kb[:, h * MLA_PAD:(h + 1) * MLA_PAD],
                                qt[h * MLA_PAD:(h + 1) * MLA_PAD, c * MLA_CW:(c + 1) * MLA_CW],
                                preferred_element_type=f32)
                    s_scr[slot_s, n, rows, :] = s
                    cm = jnp.max(s, axis=0, keepdims=True)
                    cnew[n] = cm if cnew[n] is None else jnp.maximum(cnew[n], cm)
            vt = vt_ref[:, pl.ds(koff + kc * MLA_KC, MLA_KC)]
            for n, (h, c) in enumerate(chains):
                p = jnp.exp2(s_scr[slot_a, n, rows, :] - mns[n]).astype(bf16)
                vte = jnp.concatenate([vt[h * MLA_V:(h + 1) * MLA_V], ones], axis=0)
                d = jnp.dot(vte, p, preferred_element_type=f32)
                parts[n] = d if parts[n] is None else parts[n] + d
        new = []
        for n, (h, c) in enumerate(chains):
            acc = als[n] * state[2 * n + 1] + parts[n]
            o_ref[h * MLA_V:(h + 1) * MLA_V, pl.ds(qoff + c * MLA_CW, MLA_CW)] = (
                acc[:MLA_V] / acc[MLA_V:MLA_V + 1]).astype(bf16)
            new += [mns[n], acc]
        return (tuple(cnew) if t_s is not None else ()), tuple(new)

    nt = nq * nk
    ahead, ns = MLA_AHEAD, MLA_SLOTS

    def trip(t0, carry, last):
        pend, state = list(carry[:nch * ahead]), carry[nch * ahead:]
        for u in range(MLA_TRIP):
            t_s = None if (last and u + ahead >= MLA_TRIP) else t0 + u + ahead
            cn, state = step(t_s, (u + ahead) % ns, t0 + u, u % ns, tuple(pend[:nch]), state)
            pend = pend[nch:] + list(cn)
        return tuple(pend) + state

    m0 = jnp.full((1, MLA_CW), -jnp.inf, f32)
    a0 = jnp.zeros((MLA_VROWS, MLA_CW), f32)
    carry = ()
    for t in range(ahead):
        carry += scores(t, t)
    carry = lax.fori_loop(0, nt // MLA_TRIP - 1, lambda i, c: trip(i * MLA_TRIP, c, False), carry + (m0, a0) * nch)
    trip(nt - MLA_TRIP, carry, True)


def _mla_attention(mqt, mk, mvt):
    B, L, _ = mk.shape
    npair = MLA_HEADS // 2
    nq, nk = L // MLA_TQ, L // MLA_TK
    assert (nq * nk) % MLA_TRIP == 0 and MLA_TRIP % MLA_SLOTS == 0
    mode = dict(pipeline_mode=pl.Buffered(1)) if L * 2 * MLA_PAD * mk.dtype.itemsize > SINGLE_BUFFER_BYTES else {}
    return pl.pallas_call(
        functools.partial(_mla_body, nq=nq, nk=nk),
        grid=(B, npair),
        in_specs=[pl.BlockSpec((None, 2 * MLA_PAD, L), lambda b, p: (b, p, 0), **mode),
                  pl.BlockSpec((None, L, 2 * MLA_PAD), lambda b, p: (b, 0, p), **mode),
                  pl.BlockSpec((None, 2 * MLA_V, L), lambda b, p: (b, p, 0), **mode)],
        out_specs=pl.BlockSpec((None, 2 * MLA_V, L), lambda b, p: (b, p, 0)),
        out_shape=jax.ShapeDtypeStruct((B, MLA_HEADS * MLA_V, L), bf16),
        scratch_shapes=[pltpu.VMEM((MLA_SLOTS, 2 * MLA_TQ // MLA_CW, MLA_TK, MLA_CW), f32)],
        compiler_params=_cparams(("parallel", "parallel")),
        name="mla_attn",
    )(mqt, mk, mvt)


def _toeplitz_body(g_ref, pf_ref, o_ref):
    kern = jnp.dot(g_ref[...], pf_ref[...], precision=lax.Precision.HIGHEST, preferred_element_type=f32)
    bits = lax.bitcast_convert_type(kern.astype(bf16).astype(f32), jnp.int32)
    for cp in range(SSM_GC):
        for c in range(0, SSM_GC, 2):
            r = cp * SSM_GC + c
            packed = lax.shift_right_logical(bits[r:r + 1, :], 16) | bits[r + 1:r + 2, :]
            rowb = jnp.broadcast_to(packed, (LANES, 2 * LANES))
            toe = pltpu.roll(rowb, LANES + 1, 1, stride=1, stride_axis=0)[:, :LANES]
            even = lax.bitcast_convert_type(lax.shift_left(toe, 16), f32)
            odd = lax.bitcast_convert_type(toe & jnp.int32(-65536), f32)
            o_ref[cp * LANES:(cp + 1) * LANES, c * LANES:(c + 1) * LANES] = even.astype(bf16)
            o_ref[cp * LANES:(cp + 1) * LANES, (c + 1) * LANES:(c + 2) * LANES] = odd.astype(bf16)


def _toeplitz_gen(gm, pf):
    dg = pf.shape[0]
    n = SSM_GC * LANES
    k = 4 * SSM_STATE
    return pl.pallas_call(
        _toeplitz_body,
        grid=(dg,),
        in_specs=[pl.BlockSpec((None, SSM_GC * SSM_GC, k), lambda g: (g, 0, 0)),
                  pl.BlockSpec((None, k, 2 * LANES), lambda g: (g, 0, 0))],
        out_specs=pl.BlockSpec((None, n, n), lambda g: (g, 0, 0)),
        out_shape=jax.ShapeDtypeStruct((dg, n, n), bf16),
        compiler_params=_cparams(("parallel",)),
        name="s5_toeplitz",
    )(gm, pf)


def _ssm_body(*refs, segs):
    ns = len(segs)
    u_refs, (t_ref, wv_ref, wc_ref, a_ref, d_ref) = refs[:ns], refs[ns:ns + 5]
    o_refs = refs[ns + 5:2 * ns + 5]
    vre, vim, xfre, xfim, xbre, xbim = refs[2 * ns + 5:]
    u32 = jnp.concatenate([jnp.concatenate([u_ref[:, c, :] for c in range(SSM_GC)], axis=1) for u_ref in u_refs],
                          axis=0)
    m = u32.shape[0]
    ub = u32.astype(bf16)
    y = jnp.dot(ub, t_ref[...], preferred_element_type=f32)
    v = jnp.dot(ub, wv_ref[...], preferred_element_type=f32)
    vre[...] = v[:, :LANES]
    vim[...] = v[:, LANES:]
    are = a_ref[0:1, :]
    aim = a_ref[1:2, :]
    row0 = 0
    for nb, nj in segs:
        is_fwd = lax.broadcasted_iota(jnp.int32, (nb, LANES), 1) < SSM_STATE

        def rows(j, nb=nb, nj=nj, row0=row0):
            return pl.ds(row0 + j, nb, stride=nj) if nb > 1 else pl.ds(row0 + j, 1)

        def step(k, carry, nj=nj, rows=rows, is_fwd=is_fwd):
            xr, xi = carry
            rf, rb = rows(k), rows(nj - 1 - k)
            xfre[rf, :] = xr
            xfim[rf, :] = xi
            xbre[rb, :] = xr
            xbim[rb, :] = xi
            vr = jnp.where(is_fwd, vre[rf, :], vre[rb, :])
            vi = jnp.where(is_fwd, vim[rf, :], vim[rb, :])
            return are * xr - aim * xi + vr, are * xi + aim * xr + vi

        z = jnp.zeros((nb, LANES), f32)
        lax.fori_loop(0, nj, step, (z, z))
        row0 += nb * nj
    fwd_m = lax.broadcasted_iota(jnp.int32, (m, LANES), 1) < SSM_STATE
    xin = jnp.concatenate([jnp.where(fwd_m, xfre[...], xbre[...]),
                           jnp.where(fwd_m, xfim[...], xbim[...])], axis=1).astype(bf16)
    y = y + jnp.dot(xin, wc_ref[...], preferred_element_type=f32) + u32 * d_ref[...]
    row0 = 0
    for (nb, nj), o_ref in zip(segs, o_refs):
        for c in range(SSM_GC):
            o_ref[:, c, :] = y[row0:row0 + nb * nj, c * LANES:(c + 1) * LANES]
        row0 += nb * nj


def _ssm(uts, layer, S, segs):
    n = SSM_GC * LANES
    ms = [nb * nj for nb, nj in segs]
    gspec = lambda tail: pl.BlockSpec((None,) + tail, lambda g: (layer * SSM_GROUPS + g,) + (0,) * len(tail))
    io = [pl.BlockSpec((m, None, SSM_GC, LANES), lambda g: (0, g, 0, 0)) for m in ms]
    return pl.pallas_call(
        functools.partial(_ssm_body, segs=tuple(segs)),
        grid=(SSM_GROUPS,),
        in_specs=io + [gspec((n, n)), gspec((n, 4 * SSM_STATE)), gspec((4 * SSM_STATE, n)),
                       gspec((2, LANES)), gspec((1, n))],
        out_specs=io,
        out_shape=[jax.ShapeDtypeStruct((m, SSM_GROUPS, SSM_GC, LANES), f32) for m in ms],
        scratch_shapes=[pltpu.VMEM((sum(ms), LANES), f32)] * 6,
        compiler_params=_cparams(("parallel",)),
        name="s5_conv",
    )(*uts, S["toe"], S["wv"], S["wc"], S["a"], S["d"])


def _ssm_tables(a_re, a_im, b_re, b_im, c_re, c_im, log_dt, ssm_d):
    depth = a_re.shape[0]
    dg = depth * SSM_GROUPS
    lam = lax.complex(a_re.astype(f32), a_im.astype(f32))
    dt = jnp.exp(log_dt.astype(f32))[..., None]
    lam_dt = lam * dt
    lam_bar = jnp.exp(lam_dt)
    b_bar = ((lam_bar - 1.0) / lam)[..., None] * lax.complex(b_re.astype(f32), b_im.astype(f32))
    cmat = lax.complex(c_re.astype(f32), c_im.astype(f32))
    steps = jnp.arange(LANES + 1, dtype=f32)
    pw = jnp.exp(lam_dt[..., None] * steps)
    pwf, pwb = pw[:, 0], pw[:, 1]

    e = np.arange(2 * LANES)
    f_ok = (e >= LANES - 1) & (e <= 2 * LANES - 2)
    b_ok = e <= LANES - 1
    pf_f = jnp.where(f_ok, pwf[..., np.clip(e - (LANES - 1), 0, LANES - 1)], 0.0)
    pf_b = jnp.where(b_ok, pwb[..., np.clip(LANES - 1 - e, 0, LANES - 1)], 0.0)
    pf = jnp.stack([pf_f.real, pf_f.imag, pf_b.real, pf_b.imag], axis=2)
    pf = pf.reshape(dg, 4 * SSM_STATE, 2 * LANES)

    bbf, bbb = b_bar[:, 0], b_bar[:, 1]
    cf, cb = cmat[:, 0], cmat[:, 1]
    gf = cf[:, :, None, :, :] * jnp.swapaxes(bbf, -1, -2)[:, :, :, None, :]
    gb = cb[:, :, None, :, :] * jnp.swapaxes(bbb, -1, -2)[:, :, :, None, :]
    gm = jnp.concatenate([gf.real, -gf.imag, gb.real, -gb.imag], axis=-1)
    gm = gm.reshape(dg, SSM_GC * SSM_GC, 4 * SSM_STATE)

    s_idx = np.arange(LANES)
    pvf = jnp.swapaxes(pwf[..., LANES - 1 - s_idx], -1, -2)
    pvb = jnp.swapaxes(pwb[..., s_idx], -1, -2)
    pv1 = jnp.concatenate([pvf.real, pvb.real, pvf.imag, pvb.imag], axis=-1).reshape(dg, LANES, 4 * SSM_STATE)
    pv2 = jnp.concatenate([pvf.imag, pvb.imag, pvf.real, pvb.real], axis=-1).reshape(dg, LANES, 4 * SSM_STATE)
    bft, bbt = jnp.swapaxes(bbf, -1, -2), jnp.swapaxes(bbb, -1, -2)
    bv = jnp.stack([jnp.concatenate([bft.real, bbt.real, bft.real, bbt.real], axis=-1),
                    jnp.concatenate([-bft.imag, -bbt.imag, bft.imag, bbt.imag], axis=-1)], axis=3)
    bv = bv.reshape(dg, SSM_GC, 2, 4 * SSM_STATE)
    t_idx = np.arange(LANES)
    pcf, pcb = pwf[..., t_idx + 1], pwb[..., LANES - t_idx]
    pc = jnp.stack([pcf.real, pcf.imag, pcb.real, pcb.imag], axis=2).reshape(dg, 4, SSM_STATE, LANES)
    cft, cbt = jnp.swapaxes(cf, -1, -2), jnp.swapaxes(cb, -1, -2)
    ct = jnp.stack([cft.real, cft.imag, cbt.real, cbt.imag], axis=2).reshape(dg, 4, SSM_STATE, SSM_GC)

    a128 = jnp.concatenate([pwf[..., LANES], pwb[..., LANES]], axis=-1)
    a = jnp.stack([a128.real, a128.imag], axis=2).reshape(dg, 2, LANES)
    d = jnp.repeat(ssm_d.astype(f32).reshape(dg, SSM_GC), LANES, axis=-1).reshape(dg, 1, SSM_GC * LANES)
    return dict(pf=pf, gm=gm, pv1=pv1, pv2=pv2, bv=bv, pc=pc, ct=ct, a=a, d=d)


def _state_tables_body(pv1_ref, pv2_ref, bv_ref, pc_ref, ct_ref, wv_ref, wc_ref):
    pv1, pv2 = pv1_ref[...], pv2_ref[...]
    for cp in range(SSM_GC):
        wv_ref[cp * LANES:(cp + 1) * LANES, :] = (pv1 * bv_ref[cp, 0:1, :] + pv2 * bv_ref[cp, 1:2, :]).astype(bf16)
    pfr, pfi, pbr, pbi = pc_ref[0], pc_ref[1], pc_ref[2], pc_ref[3]
    cfr, cfi, cbr, cbi = ct_ref[0], ct_ref[1], ct_ref[2], ct_ref[3]
    for c in range(SSM_GC):
        col = slice(c, c + 1)
        tile = jnp.concatenate([cfr[:, col] * pfr - cfi[:, col] * pfi,
                                cbr[:, col] * pbr - cbi[:, col] * pbi,
                                -(cfr[:, col] * pfi + cfi[:, col] * pfr),
                                -(cbr[:, col] * pbi + cbi[:, col] * pbr)], axis=0)
        wc_ref[:, c * LANES:(c + 1) * LANES] = tile.astype(bf16)


def _state_tables(S):
    dg = S["pf"].shape[0]
    n, k = SSM_GC * LANES, 4 * SSM_STATE
    g3 = lambda a, b: pl.BlockSpec((None, a, b), lambda g: (g, 0, 0))
    g4 = lambda a, b, c: pl.BlockSpec((None, a, b, c), lambda g: (g, 0, 0, 0))
    return pl.pallas_call(
        _state_tables_body,
        grid=(dg,),
        in_specs=[g3(LANES, k), g3(LANES, k), g4(SSM_GC, 2, k), g4(4, SSM_STATE, LANES), g4(4, SSM_STATE, SSM_GC)],
        out_specs=(g3(n, k), g3(k, n)),
        out_shape=(jax.ShapeDtypeStruct((dg, n, k), bf16), jax.ShapeDtypeStruct((dg, k, n), bf16)),
        compiler_params=_cparams(("parallel",)),
        name="s5_state_tables",
    )(S["pv1"], S["pv2"], S["bv"], S["pc"], S["ct"])


def _mixffn_body(x_ref, mod_ref, yna_ref, ymlat_ref, yst_ref, gluw_ref, glub_ref, wo_ref, g2_ref,
                 wg_ref, wu_ref, wd_ref, o_ref, act_ref):
    ys = jnp.concatenate([yst_ref[c].reshape(SSM_WIDTH, LANES).T for c in range(yst_ref.shape[0])],
                         axis=0)
    ys = 0.5 * ys * (1.0 + jnp.tanh(math.sqrt(2.0 / math.pi) * (ys + 0.044715 * (ys * ys * ys))))
    gate = jax.nn.sigmoid(jnp.dot(ys.astype(bf16), gluw_ref[...], preferred_element_type=f32) + glub_ref[...])
    yssm = (ys * gate).astype(bf16)
    o1 = NA_WIDTH
    o2 = NA_WIDTH + MLA_HEADS * MLA_V
    mix = (jnp.dot(yna_ref[...], wo_ref[0:o1, :], preferred_element_type=f32)
           + lax.dot_general(ymlat_ref[...], wo_ref[o1:o2, :], _TN, preferred_element_type=f32)
           + jnp.dot(yssm, wo_ref[o2:, :], preferred_element_type=f32))
    x1 = x_ref[...] + mod_ref[2:3, :] * mix
    o_ref[...] = x1
    ms = jnp.mean(x1 * x1, axis=-1, keepdims=True)
    h2 = ((x1 * lax.rsqrt(ms + RMS_EPS) * g2_ref[...]) * (1.0 + mod_ref[4:5, :]) + mod_ref[3:4, :]).astype(bf16)

    th = MXU_TILE
    for c in range(FFN_HIDDEN // th):
        cs = slice(c * th, (c + 1) * th)
        g = jnp.dot(h2, wg_ref[:, cs], preferred_element_type=f32)
        u = jnp.dot(h2, wu_ref[:, cs], preferred_element_type=f32)
        act_ref[:, cs] = (g * jax.nn.sigmoid(g) * u).astype(bf16)
    ffn = jnp.dot(act_ref[...], wd_ref[...], preferred_element_type=f32)
    o_ref[...] = o_ref[...] + mod_ref[5:6, :] * ffn


def _mixffn(x, mod, yna, ymlat, yst, layer, P):
    B, L, D = x.shape
    tm = TOKEN_TILE
    nt = L // tm
    cpt = tm // LANES
    tok = lambda w: pl.BlockSpec((None, tm, w), lambda b, i: (b, i, 0))

    def wspec(name):
        tail = P[name].shape[1:]
        return pl.BlockSpec((None,) + tail, lambda b, i: (layer,) + (0,) * len(tail), pipeline_mode=pl.Buffered(1))

    return pl.pallas_call(
        _mixffn_body,
        grid=(B, nt),
        in_specs=[tok(D), pl.BlockSpec((None, 6, D), lambda b, i: (b, 0, 0)), tok(NA_WIDTH),
                  pl.BlockSpec((None, MLA_HEADS * MLA_V, tm), lambda b, i: (b, 0, i)),
                  pl.BlockSpec((cpt, SSM_GROUPS, SSM_GC, LANES), lambda b, i: (b * nt + i, 0, 0, 0)),
                  wspec("glu_w"), wspec("glu_b"), wspec("w_out"), wspec("g2"),
                  wspec("w_gate"), wspec("w_up"), wspec("w_down")],
        out_specs=tok(D),
        out_shape=jax.ShapeDtypeStruct((B, L, D), f32),
        scratch_shapes=[pltpu.VMEM((tm, FFN_HIDDEN), bf16)],
        compiler_params=_cparams(("parallel", "parallel")),
        name="mix_ffn",
    )(x, mod, yna, ymlat, yst, P["glu_w"], P["glu_b"], P["w_out"], P["g2"], P["w_gate"], P["w_up"], P["w_down"])


def _prep_params(norm1_g, w_in, na_q_g, na_k_g, mla_cq_g, mla_ckv_g, mla_w_uq, mla_w_ukv, mla_qn_g, mla_kn_g,
                 mla_qr_g, mla_kr_g, glu_w, glu_b, w_out, norm2_g, ffn_w_gate, ffn_w_up, ffn_w_down):
    depth = w_in.shape[0]
    o1 = 3 * NA_WIDTH
    o2 = o1 + MLA_Q_LORA
    o3 = o2 + MLA_KV_LORA
    o4 = o3 + MLA_ROPE
    row = lambda a: a.astype(f32)[:, None, :]
    zeros = lambda *s: jnp.zeros(s, f32)
    P = {}
    P["g1"] = row(norm1_g)
    P["g2"] = row(norm2_g)
    P["w_na"] = w_in[:, :, :o1].astype(bf16)
    P["w_c"] = jnp.concatenate([w_in[:, :, o1:o3], zeros(depth, D_MODEL, MLA_NOPE), w_in[:, :, o3:o4],
                                zeros(depth, D_MODEL, MLA_PAD - MLA_NOPE - MLA_ROPE)], axis=-1).astype(bf16)
    P["w_ut"] = jnp.swapaxes(w_in[:, :, o4:], 1, 2).astype(bf16)
    wq = mla_w_uq.reshape(depth, MLA_Q_LORA, MLA_HEADS, MLA_NOPE + MLA_ROPE)
    wq = jnp.concatenate([wq, zeros(depth, MLA_Q_LORA, MLA_HEADS, MLA_PAD - MLA_NOPE - MLA_ROPE)], axis=-1)
    P["w_uq"] = wq.reshape(depth, MLA_Q_LORA, MLA_HEADS * MLA_PAD).astype(bf16)
    wkv = mla_w_ukv.reshape(depth, MLA_KV_LORA, MLA_HEADS, MLA_NOPE + MLA_V)
    wk = jnp.concatenate([wkv[..., :MLA_NOPE], zeros(depth, MLA_KV_LORA, MLA_HEADS, MLA_PAD - MLA_NOPE)], axis=-1)
    P["w_ukv_k"] = wk.reshape(depth, MLA_KV_LORA, MLA_HEADS * MLA_PAD).astype(bf16)
    P["w_ukv_vt"] = jnp.swapaxes(wkv[..., MLA_NOPE:].reshape(depth, MLA_KV_LORA, MLA_HEADS * MLA_V), 1, 2).astype(bf16)
    P["gq_na"] = row(jnp.tile(na_q_g, (1, NA_HEADS))) * (NA_HEAD_DIM ** -0.5 * math.log2(math.e))
    P["gk_na"] = row(jnp.tile(na_k_g, (1, NA_HEADS)))
    P["g_cq"] = row(mla_cq_g)
    P["g_ckv"] = row(mla_ckv_g)
    scale = (MLA_NOPE + MLA_ROPE) ** -0.5 * math.log2(math.e)
    pad = MLA_PAD - MLA_NOPE - MLA_ROPE
    gq = jnp.concatenate([mla_qn_g, mla_qr_g, zeros(depth, pad)], axis=-1) * scale
    P["gq"] = row(jnp.tile(gq, (1, MLA_HEADS)))
    gk = jnp.concatenate([mla_kn_g, zeros(depth, MLA_PAD - MLA_NOPE)], axis=-1)
    P["gk"] = row(jnp.tile(gk, (1, MLA_HEADS)))
    P["gkr"] = row(jnp.concatenate([zeros(depth, MLA_NOPE), mla_kr_g, zeros(depth, pad)], axis=-1))
    invq = np.concatenate([np.full(MLA_NOPE, 1.0 / MLA_NOPE), np.full(MLA_ROPE, 1.0 / MLA_ROPE), np.zeros(pad)])
    invk = np.concatenate([np.full(MLA_NOPE, 1.0 / MLA_NOPE), np.zeros(MLA_PAD - MLA_NOPE)])
    P["invq"] = jnp.asarray(np.tile(invq, MLA_HEADS)[None, :], f32)
    P["invk"] = jnp.asarray(np.tile(invk, MLA_HEADS)[None, :], f32)
    lane = np.arange(NA_WIDTH)
    P["G_na"] = jnp.asarray((lane[:, None] // NA_HEAD_DIM) == (lane[None, :] // NA_HEAD_DIM), bf16)
    lane = np.arange(2 * MLA_PAD)
    grp = np.where(lane % MLA_PAD < MLA_NOPE, 0, np.where(lane % MLA_PAD < MLA_NOPE + MLA_ROPE, 1, 2))
    same = (lane[:, None] // MLA_PAD == lane[None, :] // MLA_PAD) & (grp[:, None] == grp[None, :]) & (grp[:, None] < 2)
    P["G_qm"] = jnp.asarray(same, bf16)
    P["glu_w"] = glu_w.astype(bf16)
    P["glu_b"] = row(glu_b)
    P["w_out"] = w_out.astype(bf16)
    P["w_gate"] = ffn_w_gate.astype(bf16)
    P["w_up"] = ffn_w_up.astype(bf16)
    P["w_down"] = ffn_w_down.astype(bf16)
    return P


def _rope_tables(length):
    inv = 1.0 / (ROPE_BASE ** (jnp.arange(0, MLA_ROPE, 2, dtype=f32) / MLA_ROPE))
    ang = jnp.arange(length, dtype=f32)[:, None] * inv[None, :]
    cos, sin = jnp.cos(ang), jnp.sin(ang)
    half = MLA_ROPE // 2
    z = lambda w: jnp.zeros((length, w), f32)
    pad = MLA_PAD - MLA_NOPE - MLA_ROPE
    cos_t = jnp.concatenate([jnp.ones((length, MLA_NOPE), f32), cos, cos, z(pad)], axis=-1)
    sin_lo = jnp.concatenate([z(MLA_NOPE), -sin, z(half), z(pad)], axis=-1)
    sin_hi = jnp.concatenate([z(MLA_NOPE), z(half), sin, z(pad)], axis=-1)
    return cos_t, sin_lo, sin_hi


def _trunks(xs, mods, P, S, bias):
    xs = list(xs)
    ropes = [_rope_tables(x.shape[1]) for x in xs]
    segs = [(x.shape[0], x.shape[1] // LANES) for x in xs]
    for layer in range(len(mods[0])):
        heads, uts = [], []
        for x, mod, rope in zip(xs, mods, ropes):
            naq, nak, nav, mqt, mk, mvt, ut = _inproj(x, mod[layer], layer, P, rope)
            heads.append((_na_attention(naq, nak, nav, bias, layer), _mla_attention(mqt, mk, mvt)))
            uts.append(ut)
        ysts = _ssm(uts, layer, S, segs)
        xs = [_mixffn(x, mod[layer], yna, ymla, yst, layer, P)
              for x, mod, (yna, ymla), yst in zip(xs, mods, heads, ysts)]
    return xs


def kernel(x_prompt, x_sample, c_prompt, c_sample, ada_w, ada_b, norm1_g, w_in, na_q_g, na_k_g, na_rpb, mla_cq_g,
           mla_ckv_g, mla_w_uq, mla_w_ukv, mla_qn_g, mla_kn_g, mla_qr_g, mla_kr_g, ssm_a_re, ssm_a_im, ssm_b_re,
           ssm_b_im, ssm_c_re, ssm_c_im, ssm_log_dt, ssm_d, glu_w, glu_b, w_out, norm2_g, ffn_w_gate, ffn_w_up,
           ffn_w_down):
    depth = w_in.shape[0]
    nbp, nbs = c_prompt.shape[0], c_sample.shape[0]
    rows = -(-(nbp + nbs) // 8) * 8
    c_all = jnp.concatenate([c_prompt, c_sample, jnp.zeros((rows - nbp - nbs, D_MODEL), f32)], axis=0)
    mod = _modulation(c_all, ada_w, ada_b).reshape(depth, rows, 6, D_MODEL)
    mods_p = [mod[l, :nbp] for l in range(depth)]
    mods_s = [mod[l, nbp:nbp + nbs] for l in range(depth)]

    P = _prep_params(norm1_g, w_in, na_q_g, na_k_g, mla_cq_g, mla_ckv_g, mla_w_uq, mla_w_ukv, mla_qn_g, mla_kn_g,
                     mla_qr_g, mla_kr_g, glu_w, glu_b, w_out, norm2_g, ffn_w_gate, ffn_w_up, ffn_w_down)
    S = _ssm_tables(ssm_a_re, ssm_a_im, ssm_b_re, ssm_b_im, ssm_c_re, ssm_c_im, ssm_log_dt, ssm_d)
    S["toe"] = _toeplitz_gen(S["gm"], S["pf"])
    S["wv"], S["wc"] = _state_tables(S)
    bias = _na_bias_tables(na_rpb)

    y_prompt, y_sample = _trunks((x_prompt, x_sample), (mods_p, mods_s), P, S, bias)
    return (y_prompt, y_sample)
```

```python
import functools
import math

import numpy as np
import jax
import jax.numpy as jnp
from jax import lax
from jax.experimental import pallas as pl
from jax.experimental.pallas import tpu as pltpu

f32 = jnp.float32
bf16 = jnp.bfloat16

D_MODEL = 1024
GRID_W = 64
NA_HEADS = 4
NA_HEAD_DIM = 64
NA_WIDTH = NA_HEADS * NA_HEAD_DIM
NA_ROWS = 8
NA_COLS = 16
MLA_HEADS = 8
MLA_NOPE = 64
MLA_ROPE = 32
MLA_V = 64
MLA_Q_LORA = 384
MLA_KV_LORA = 256
ROPE_BASE = 10000.0
SSM_GROUPS = 16
SSM_GC = 16
SSM_STATE = 64
SSM_WIDTH = SSM_GROUPS * SSM_GC
FFN_HIDDEN = 2816
RMS_EPS = 1e-6
NEG_INF = -1e30

LANES = 128
MLA_PAD = 128
NA_QCOLS = 16
NA_KCOLS = 2 * NA_COLS
NA_QROWS = 4
NA_SUB = 4
NA_WROWS = NA_QROWS + NA_ROWS
NA_BIAS_LANE0 = 200
TOKEN_TILE = 512
MXU_TILE = 256
BF16_SUBLANES = 16
MLA_CW = MXU_TILE
MLA_TQ = MLA_CW
MLA_TK = 1024
MLA_KC = MXU_TILE
MLA_AHEAD = 2
MLA_SLOTS = 2 * MLA_AHEAD
MLA_TRIP = 16
MLA_VROWS = MLA_V + BF16_SUBLANES
MOD_TILE = 1536
VMEM_LIMIT = 48 * 1024 * 1024
SINGLE_BUFFER_BYTES = 4 * 1024 * 1024

_NT = (((1,), (1,)), ((), ()))
_TN = (((0,), (0,)), ((), ()))


def _cparams(sem):
    return pltpu.CompilerParams(dimension_semantics=sem, vmem_limit_bytes=VMEM_LIMIT)


def _layer_spec(tail, layer):
    n = len(tail)
    return pl.BlockSpec((None,) + tuple(tail), lambda *_: (layer,) + (0,) * n)


def _mod_body(c_ref, w_ref, b_ref, o_ref):
    c = c_ref[...]
    s = c * jax.nn.sigmoid(c)
    o_ref[...] = jnp.dot(s, w_ref[...], precision=lax.Precision.HIGHEST, preferred_element_type=f32) + b_ref[...]


def _modulation(c_all, ada_w, ada_b):
    depth, d, n = ada_w.shape
    tn = MOD_TILE
    rows = c_all.shape[0]
    return pl.pallas_call(
        _mod_body,
        grid=(depth, n // tn),
        in_specs=[
            pl.BlockSpec((rows, d), lambda l, j: (0, 0)),
            pl.BlockSpec((None, d, tn), lambda l, j: (l, 0, j)),
            pl.BlockSpec((None, 1, tn), lambda l, j: (l, 0, j)),
        ],
        out_specs=pl.BlockSpec((None, rows, tn), lambda l, j: (l, 0, j)),
        out_shape=jax.ShapeDtypeStruct((depth, rows, n), f32),
        compiler_params=_cparams(("arbitrary", "arbitrary")),
        name="adaln_mod",
    )(c_all, ada_w, ada_b.reshape(depth, 1, n))


def _rope(x, cos, sin_lo, sin_hi):
    return x * cos + pltpu.roll(x, LANES - 16, 1) * sin_lo + pltpu.roll(x, 16, 1) * sin_hi


def _inproj_body(x_ref, mod_ref, g1_ref, wna_ref, wc_ref, wut_ref, wuq_ref, wukvk_ref, wukvvt_ref,
                 gqna_ref, gkna_ref, gcq_ref, gckv_ref, gq_ref, gk_ref, gkr_ref, invq_ref, invk_ref,
                 gna_ref, gqm_ref, cos_ref, sinlo_ref, sinhi_ref,
                 naq_ref, nak_ref, nav_ref, mqt_ref, mk_ref, mvt_ref, ut_ref):
    x = x_ref[...]
    shift1 = mod_ref[0:1, :]
    scale1 = mod_ref[1:2, :]
    ms = jnp.mean(x * x, axis=-1, keepdims=True)
    h = (x * lax.rsqrt(ms + RMS_EPS) * g1_ref[...]) * (1.0 + scale1) + shift1
    hb = h.astype(bf16)

    z = jnp.dot(hb, wna_ref[...], preferred_element_type=f32)
    q = z[:, :NA_WIDTH]
    k = z[:, NA_WIDTH:2 * NA_WIDTH]
    gna = gna_ref[...]
    ssq = jnp.dot((q * q).astype(bf16), gna, preferred_element_type=f32) * (1.0 / NA_HEAD_DIM)
    ssk = jnp.dot((k * k).astype(bf16), gna, preferred_element_type=f32) * (1.0 / NA_HEAD_DIM)
    naq_ref[...] = (q * lax.rsqrt(ssq + RMS_EPS) * gqna_ref[...]).astype(bf16)
    nak_ref[...] = (k * lax.rsqrt(ssk + RMS_EPS) * gkna_ref[...]).astype(bf16)
    nav_ref[...] = z[:, 2 * NA_WIDTH:].astype(bf16)

    zc = jnp.dot(hb, wc_ref[...], preferred_element_type=f32)
    cq = zc[:, :MLA_Q_LORA]
    ckv = zc[:, MLA_Q_LORA:MLA_Q_LORA + MLA_KV_LORA]
    cqn = (cq * lax.rsqrt(jnp.mean(cq * cq, axis=-1, keepdims=True) + RMS_EPS) * gcq_ref[...]).astype(bf16)
    ckvn = (ckv * lax.rsqrt(jnp.mean(ckv * ckv, axis=-1, keepdims=True) + RMS_EPS) * gckv_ref[...]).astype(bf16)

    cos = cos_ref[...]
    sin_lo = sinlo_ref[...]
    sin_hi = sinhi_ref[...]
    gqm = gqm_ref[...]

    kr = zc[:, MLA_Q_LORA + MLA_KV_LORA:]
    kr_ms = jnp.sum(kr * kr, axis=-1, keepdims=True) * (1.0 / MLA_ROPE)
    kr = _rope(kr * lax.rsqrt(kr_ms + RMS_EPS) * gkr_ref[...], cos, sin_lo, sin_hi)

    qraw = jnp.dot(cqn, wuq_ref[...], preferred_element_type=f32)
    kraw = jnp.dot(ckvn, wukvk_ref[...], preferred_element_type=f32)
    for p in range(MLA_HEADS // 2):
        sl = slice(2 * p * MLA_PAD, 2 * (p + 1) * MLA_PAD)
        qs = qraw[:, sl]
        ss = jnp.dot((qs * qs).astype(bf16), gqm, preferred_element_type=f32) * invq_ref[:, sl]
        qn = qs * lax.rsqrt(ss + RMS_EPS) * gq_ref[:, sl]
        ks = kraw[:, sl]
        ss = jnp.dot((ks * ks).astype(bf16), gqm, preferred_element_type=f32) * invk_ref[:, sl]
        kn = ks * lax.rsqrt(ss + RMS_EPS) * gk_ref[:, sl]
        for hh in range(2):
            lo = (2 * p + hh) * MLA_PAD
            piece = qn[:, hh * MLA_PAD:(hh + 1) * MLA_PAD]
            mqt_ref[lo:lo + MLA_PAD, :] = _rope(piece, cos, sin_lo, sin_hi).T.astype(bf16)
            mk_ref[:, lo:lo + MLA_PAD] = (kn[:, hh * MLA_PAD:(hh + 1) * MLA_PAD] + kr).astype(bf16)
    mvt_ref[...] = lax.dot_general(wukvvt_ref[...], ckvn, _NT, preferred_element_type=f32).astype(bf16)

    ut = lax.dot_general(wut_ref[...], hb, _NT, preferred_element_type=f32)
    for c in range(ut_ref.shape[0]):
        ut_ref[c] = ut[:, c * LANES:(c + 1) * LANES].reshape(SSM_GROUPS, SSM_GC, LANES)


def _inproj(x, mod, layer, P, rope_tabs):
    B, L, D = x.shape
    tm = TOKEN_TILE
    nt = L // tm
    cpt = tm // LANES
    tok = lambda w: pl.BlockSpec((None, tm, w), lambda b, i: (b, i, 0))
    pos = pl.BlockSpec((tm, LANES), lambda b, i: (i, 0))
    const2 = lambda a: pl.BlockSpec(a.shape, lambda b, i: (0, 0))
    lw = lambda name: _layer_spec(P[name].shape[1:], layer)
    names = ["g1", "w_na", "w_c", "w_ut", "w_uq", "w_ukv_k", "w_ukv_vt",
             "gq_na", "gk_na", "g_cq", "g_ckv", "gq", "gk", "gkr"]
    consts = ["invq", "invk", "G_na", "G_qm"]
    in_specs = ([tok(D), pl.BlockSpec((None, 6, D), lambda b, i: (b, 0, 0))]
                + [lw(n) for n in names] + [const2(P[n]) for n in consts] + [pos, pos, pos])
    out_shapes = (
        jax.ShapeDtypeStruct((B, L, NA_WIDTH), bf16),
        jax.ShapeDtypeStruct((B, L, NA_WIDTH), bf16),
        jax.ShapeDtypeStruct((B, L, NA_WIDTH), bf16),
        jax.ShapeDtypeStruct((B, MLA_HEADS * MLA_PAD, L), bf16),
        jax.ShapeDtypeStruct((B, L, MLA_HEADS * MLA_PAD), bf16),
        jax.ShapeDtypeStruct((B, MLA_HEADS * MLA_V, L), bf16),
        jax.ShapeDtypeStruct((B * L // LANES, SSM_GROUPS, SSM_GC, LANES), f32),
    )
    chan = lambda w: pl.BlockSpec((None, w, tm), lambda b, i: (b, 0, i))
    out_specs = (tok(NA_WIDTH), tok(NA_WIDTH), tok(NA_WIDTH), chan(MLA_HEADS * MLA_PAD), tok(MLA_HEADS * MLA_PAD),
                 chan(MLA_HEADS * MLA_V),
                 pl.BlockSpec((cpt, SSM_GROUPS, SSM_GC, LANES), lambda b, i: (b * nt + i, 0, 0, 0)))
    return pl.pallas_call(
        _inproj_body,
        grid=(B, nt),
        in_specs=in_specs,
        out_specs=out_specs,
        out_shape=out_shapes,
        compiler_params=_cparams(("parallel", "parallel")),
        name="in_proj",
    )(x, mod, *[P[n] for n in names], *[P[n] for n in consts], *rope_tabs)


def _na_key_col0(cb):
    return min(max(cb * NA_QCOLS - NA_COLS // 2, 0), GRID_W - NA_KCOLS)


def _na_body(q_ref, k_ref, v_ref, b_ref, o_ref, *, rows):
    nblk = rows // NA_QROWS
    lane = lax.broadcasted_iota(jnp.int32, (1, NA_WIDTH), 1)
    hms = [(lane >= h * NA_HEAD_DIM) & (lane < (h + 1) * NA_HEAD_DIM) for h in range(NA_HEADS)]
    nq = NA_QROWS * NA_QCOLS
    for sb in range(NA_SUB):
        blk = pl.program_id(1) * NA_SUB + sb
        variant = jnp.where(blk == 0, 0, jnp.where(blk == nblk - 1, 2, 1))
        ws = jnp.clip(NA_QROWS * blk - NA_ROWS // 2, 0, rows - NA_WROWS)
        base = pl.multiple_of(ws * GRID_W, GRID_W)
        q0 = sb * NA_QROWS * GRID_W
        for cb in range(GRID_W // NA_QCOLS):
            kc0 = _na_key_col0(cb)
            kw = jnp.concatenate([k_ref[pl.ds(base + kl * GRID_W + kc0, NA_KCOLS), :] for kl in range(NA_WROWS)],
                                 axis=0)
            vw = jnp.concatenate([v_ref[pl.ds(base + kl * GRID_W + kc0, NA_KCOLS), :] for kl in range(NA_WROWS)],
                                 axis=0)
            qrows = [slice(q0 + rl * GRID_W + cb * NA_QCOLS, q0 + rl * GRID_W + (cb + 1) * NA_QCOLS)
                     for rl in range(NA_QROWS)]
            qb = jnp.concatenate([q_ref[r, :] for r in qrows], axis=0)
            qs = jnp.concatenate([jnp.where(hm, qb, jnp.zeros_like(qb)) for hm in hms], axis=0)
            s = lax.dot_general(qs, kw, _NT, preferred_element_type=f32) + b_ref[variant, cb].reshape(NA_HEADS * nq, -1)
            m = jnp.max(s, axis=-1, keepdims=True)
            p = jnp.exp2(s - m)
            l = jnp.sum(p, axis=-1, keepdims=True)
            o = jnp.dot(p.astype(bf16), vw, preferred_element_type=f32) / l
            ob = o[:nq]
            for h in range(1, NA_HEADS):
                ob = jnp.where(hms[h], o[h * nq:(h + 1) * nq], ob)
            for rl, r in enumerate(qrows):
                o_ref[r, :] = ob[rl * NA_QCOLS:(rl + 1) * NA_QCOLS].astype(bf16)


def _na_attention(q, k, v, bias, layer):
    B, L, W = q.shape
    rows = L // GRID_W
    tq = NA_SUB * NA_QROWS * GRID_W
    ncb = GRID_W // NA_QCOLS
    mode = dict(pipeline_mode=pl.Buffered(1)) if L * W * q.dtype.itemsize > SINGLE_BUFFER_BYTES else {}
    full = pl.BlockSpec((None, L, W), lambda b, i: (b, 0, 0), **mode)
    return pl.pallas_call(
        functools.partial(_na_body, rows=rows),
        grid=(B, L // tq),
        in_specs=[pl.BlockSpec((None, tq, W), lambda b, i: (b, i, 0)), full, full,
                  pl.BlockSpec((None, 3, ncb, NA_HEADS, NA_QROWS * NA_QCOLS, NA_WROWS * NA_KCOLS),
                               lambda b, i: (layer, 0, 0, 0, 0, 0), pipeline_mode=pl.Buffered(1))],
        out_specs=pl.BlockSpec((None, tq, W), lambda b, i: (b, i, 0)),
        out_shape=jax.ShapeDtypeStruct((B, L, W), bf16),
        compiler_params=_cparams(("parallel", "arbitrary")),
        name="na_attn",
    )(q, k, v, bias)


def _na_bias_body(r_ref, o_ref):
    r = r_ref[...]
    qc = lax.broadcasted_iota(jnp.int32, (NA_QCOLS, LANES), 0)
    lane = lax.broadcasted_iota(jnp.int32, (NA_QCOLS, LANES), 1)
    piece = lane // NA_KCOLS
    neg = jnp.full((NA_QCOLS, LANES), NEG_INF, f32)
    npc = LANES // NA_KCOLS
    for cb in range(GRID_W // NA_QCOLS):
        kc0 = _na_key_col0(cb)
        qcol = cb * NA_QCOLS + qc
        kcol = kc0 + lane - piece * NA_KCOLS
        cs = jnp.clip(qcol - NA_COLS // 2, 0, GRID_W - NA_COLS)
        cvalid = (kcol >= cs) & (kcol < cs + NA_COLS)
        tiles = []
        for dr in range(2 * NA_ROWS - 1):
            rowb = jnp.broadcast_to(r[dr:dr + 1, :], (NA_QCOLS, 2 * LANES))
            t = neg
            for k in range(npc):
                base = 2 * LANES - (NA_COLS - 1) - NA_BIAS_LANE0 + cb * NA_QCOLS - kc0 + k * NA_KCOLS
                rolled = pltpu.roll(rowb, base, 1, stride=1, stride_axis=0)[:, :LANES]
                t = jnp.where(piece == k, rolled, t)
            tiles.append(jnp.where(cvalid, t, neg))
        for v, (off, lo) in enumerate(((0, 0), (-(NA_ROWS // 2), None), (-NA_ROWS, NA_ROWS // 2))):
            for rl in range(NA_QROWS):
                vregs = []
                for j in range(NA_WROWS // npc):
                    t = neg
                    for k in range(npc):
                        kl = j * npc + k
                        d = off + kl - rl
                        ok = (-(NA_ROWS // 2) <= d < NA_ROWS // 2) if lo is None else (lo <= kl < lo + NA_ROWS)
                        if ok:
                            t = jnp.where(piece == k, tiles[d + NA_ROWS - 1], t)
                    vregs.append(t)
                o_ref[v, cb, rl * NA_QCOLS:(rl + 1) * NA_QCOLS, :] = jnp.concatenate(vregs, axis=1)


def _na_bias_tables(rpb):
    depth, nh, nr, nc = rpb.shape
    rp = jnp.pad(rpb.astype(f32) * math.log2(math.e),
                 ((0, 0), (0, 0), (0, 16 - nr), (NA_BIAS_LANE0, 2 * LANES - NA_BIAS_LANE0 - nc)))
    ncb, nq, nk = GRID_W // NA_QCOLS, NA_QROWS * NA_QCOLS, NA_WROWS * NA_KCOLS
    return pl.pallas_call(
        _na_bias_body,
        grid=(depth, nh),
        in_specs=[pl.BlockSpec((None, None, 16, 2 * LANES), lambda l, h: (l, h, 0, 0))],
        out_specs=pl.BlockSpec((None, 3, ncb, None, nq, nk), lambda l, h: (l, 0, 0, h, 0, 0)),
        out_shape=jax.ShapeDtypeStruct((depth, 3, ncb, nh, nq, nk), f32),
        compiler_params=_cparams(("parallel", "parallel")),
        name="na_bias",
    )(rp)


def _mla_body(qt_ref, k_ref, vt_ref, o_ref, s_scr, *, nq, nk):
    tq, tk = MLA_TQ, MLA_TK
    ones = jnp.ones((MLA_VROWS - MLA_V, MLA_KC), bf16)
    chains = [(h, c) for h in range(2) for c in range(tq // MLA_CW)]
    nch = len(chains)

    def scores(t, slot):
        qoff = pl.multiple_of((t // nk) * tq, tq)
        koff = pl.multiple_of((t % nk) * tk, tk)
        qt = qt_ref[:, pl.ds(qoff, tq)]
        kb = k_ref[pl.ds(koff, tk), :]
        cmax = []
        for n, (h, c) in enumerate(chains):
            s = jnp.dot(kb[:, h * MLA_PAD:(h + 1) * MLA_PAD],
                        qt[h * MLA_PAD:(h + 1) * MLA_PAD, c * MLA_CW:(c + 1) * MLA_CW],
                        preferred_element_type=f32)
            s_scr[slot, n] = s
            cmax.append(jnp.max(s, axis=0, keepdims=True))
        return tuple(cmax)

    nkc = tk // MLA_KC

    def step(t_s, slot_s, t_a, slot_a, cmax, state):
        if t_s is not None:
            qoff_s = pl.multiple_of((t_s // nk) * tq, tq)
            koff_s = pl.multiple_of((t_s % nk) * tk, tk)
            qt = qt_ref[:, pl.ds(qoff_s, tq)]
        j = t_a % nk
        qoff = pl.multiple_of((t_a // nk) * tq, tq)
        koff = pl.multiple_of(j * tk, tk)
        tile_start = j == 0
        mns, als = [], []
        for n in range(nch):
            m = jnp.where(tile_start, -jnp.inf, state[2 * n])
            mns.append(jnp.maximum(m, cmax[n]))
            als.append(jnp.exp2(m - mns[n]))
        parts, cnew = [None] * nch, [None] * nch
        for kc in range(nkc):
            rows = slice(kc * MLA_KC, (kc + 1) * MLA_KC)
            if t_s is not None:
                kb = k_ref[pl.ds(koff_s + kc * MLA_KC, MLA_KC), :]
                for n, (h, c) in enumerate(chains):
                    s = jnp.dot(kb[:, h * MLA_PAD:(h + 1) * MLA_PAD],
                                qt[h * MLA_PAD:(h + 1) * MLA_PAD, c * MLA_CW:(c + 1) * MLA_CW],
                                preferred_element_type=f32)
                    s_scr[slot_s, n, rows, :] = s
                    cm = jnp.max(s, axis=0, keepdims=True)
                    cnew[n] = cm if cnew[n] is None else jnp.maximum(cnew[n], cm)
            vt = vt_ref[:, pl.ds(koff + kc * MLA_KC, MLA_KC)]
            for n, (h, c) in enumerate(chains):
                p = jnp.exp2(s_scr[slot_a, n, rows, :] - mns[n]).astype(bf16)
                vte = jnp.concatenate([vt[h * MLA_V:(h + 1) * MLA_V], ones], axis=0)
                d = jnp.dot(vte, p, preferred_element_type=f32)
                parts[n] = d if parts[n] is None else parts[n] + d
        new = []
        for n, (h, c) in enumerate(chains):
            acc = als[n] * state[2 * n + 1] + parts[n]
            o_ref[h * MLA_V:(h + 1) * MLA_V, pl.ds(qoff + c * MLA_CW, MLA_CW)] = (
                acc[:MLA_V] / acc[MLA_V:MLA_V + 1]).astype(bf16)
            new += [mns[n], acc]
        return (tuple(cnew) if t_s is not None else ()), tuple(new)

    nt = nq * nk
    ahead, ns = MLA_AHEAD, MLA_SLOTS

    def trip(i, carry):
        t0 = i * MLA_TRIP
        pend, state = list(carry[:nch * ahead]), carry[nch * ahead:]
        for u in range(MLA_TRIP):
            t_s = jnp.minimum(t0 + u + ahead, nt - 1)
            cn, state = step(t_s, (u + ahead) % ns, t0 + u, u % ns, tuple(pend[:nch]), state)
            pend = pend[nch:] + list(cn)
        return tuple(pend) + state

    m0 = jnp.full((1, MLA_CW), -jnp.inf, f32)
    a0 = jnp.zeros((MLA_VROWS, MLA_CW), f32)
    carry = ()
    for t in range(ahead):
        carry += scores(t, t)
    lax.fori_loop(0, nt // MLA_TRIP, trip, carry + (m0, a0) * nch)


def _mla_attention(mqt, mk, mvt):
    B, L, _ = mk.shape
    npair = MLA_HEADS // 2
    nq, nk = L // MLA_TQ, L // MLA_TK
    assert (nq * nk) % MLA_TRIP == 0 and MLA_TRIP % MLA_SLOTS == 0
    mode = dict(pipeline_mode=pl.Buffered(1)) if L * 2 * MLA_PAD * mk.dtype.itemsize > SINGLE_BUFFER_BYTES else {}
    return pl.pallas_call(
        functools.partial(_mla_body, nq=nq, nk=nk),
        grid=(B, npair),
        in_specs=[pl.BlockSpec((None, 2 * MLA_PAD, L), lambda b, p: (b, p, 0), **mode),
                  pl.BlockSpec((None, L, 2 * MLA_PAD), lambda b, p: (b, 0, p), **mode),
                  pl.BlockSpec((None, 2 * MLA_V, L), lambda b, p: (b, p, 0), **mode)],
        out_specs=pl.BlockSpec((None, 2 * MLA_V, L), lambda b, p: (b, p, 0)),
        out_shape=jax.ShapeDtypeStruct((B, MLA_HEADS * MLA_V, L), bf16),
        scratch_shapes=[pltpu.VMEM((MLA_SLOTS, 2 * MLA_TQ // MLA_CW, MLA_TK, MLA_CW), f32)],
        compiler_params=_cparams(("parallel", "parallel")),
        name="mla_attn",
    )(mqt, mk, mvt)


def _toeplitz_body(g_ref, pf_ref, o_ref):
    kern = jnp.dot(g_ref[...], pf_ref[...], precision=lax.Precision.HIGHEST, preferred_element_type=f32)
    bits = lax.bitcast_convert_type(kern.astype(bf16).astype(f32), jnp.int32)
    for cp in range(SSM_GC):
        for c in range(0, SSM_GC, 2):
            r = cp * SSM_GC + c
            packed = lax.shift_right_logical(bits[r:r + 1, :], 16) | bits[r + 1:r + 2, :]
            rowb = jnp.broadcast_to(packed, (LANES, 2 * LANES))
            toe = pltpu.roll(rowb, LANES + 1, 1, stride=1, stride_axis=0)[:, :LANES]
            even = lax.bitcast_convert_type(lax.shift_left(toe, 16), f32)
            odd = lax.bitcast_convert_type(toe & jnp.int32(-65536), f32)
            o_ref[cp * LANES:(cp + 1) * LANES, c * LANES:(c + 1) * LANES] = even.astype(bf16)
            o_ref[cp * LANES:(cp + 1) * LANES, (c + 1) * LANES:(c + 2) * LANES] = odd.astype(bf16)


def _toeplitz_gen(gm, pf):
    dg = pf.shape[0]
    n = SSM_GC * LANES
    k = 4 * SSM_STATE
    return pl.pallas_call(
        _toeplitz_body,
        grid=(dg,),
        in_specs=[pl.BlockSpec((None, SSM_GC * SSM_GC, k), lambda g: (g, 0, 0)),
                  pl.BlockSpec((None, k, 2 * LANES), lambda g: (g, 0, 0))],
        out_specs=pl.BlockSpec((None, n, n), lambda g: (g, 0, 0)),
        out_shape=jax.ShapeDtypeStruct((dg, n, n), bf16),
        compiler_params=_cparams(("parallel",)),
        name="s5_toeplitz",
    )(gm, pf)


def _ssm_body(*refs, segs):
    ns = len(segs)
    u_refs, (t_ref, wv_ref, wc_ref, a_ref, d_ref) = refs[:ns], refs[ns:ns + 5]
    o_refs = refs[ns + 5:2 * ns + 5]
    vre, vim, xfre, xfim, xbre, xbim = refs[2 * ns + 5:]
    u32 = jnp.concatenate([jnp.concatenate([u_ref[:, c, :] for c in range(SSM_GC)], axis=1) for u_ref in u_refs],
                          axis=0)
    m = u32.shape[0]
    ub = u32.astype(bf16)
    y = jnp.dot(ub, t_ref[...], preferred_element_type=f32)
    v = jnp.dot(ub, wv_ref[...], preferred_element_type=f32)
    vre[...] = v[:, :LANES]
    vim[...] = v[:, LANES:]
    are = a_ref[0:1, :]
    aim = a_ref[1:2, :]
    row0 = 0
    for nb, nj in segs:
        is_fwd = lax.broadcasted_iota(jnp.int32, (nb, LANES), 1) < SSM_STATE

        def rows(j, nb=nb, nj=nj, row0=row0):
            return pl.ds(row0 + j, nb, stride=nj) if nb > 1 else pl.ds(row0 + j, 1)

        def step(k, carry, nj=nj, rows=rows, is_fwd=is_fwd):
            xr, xi = carry
            rf, rb = rows(k), rows(nj - 1 - k)
            xfre[rf, :] = xr
            xfim[rf, :] = xi
            xbre[rb, :] = xr
            xbim[rb, :] = xi
            vr = jnp.where(is_fwd, vre[rf, :], vre[rb, :])
            vi = jnp.where(is_fwd, vim[rf, :], vim[rb, :])
            return are * xr - aim * xi + vr, are * xi + aim * xr + vi

        z = jnp.zeros((nb, LANES), f32)
        lax.fori_loop(0, nj, step, (z, z))
        row0 += nb * nj
    fwd_m = lax.broadcasted_iota(jnp.int32, (m, LANES), 1) < SSM_STATE
    xin = jnp.concatenate([jnp.where(fwd_m, xfre[...], xbre[...]),
                           jnp.where(fwd_m, xfim[...], xbim[...])], axis=1).astype(bf16)
    y = y + jnp.dot(xin, wc_ref[...], preferred_element_type=f32) + u32 * d_ref[...]
    row0 = 0
    for (nb, nj), o_ref in zip(segs, o_refs):
        for c in range(SSM_GC):
            o_ref[:, c, :] = y[row0:row0 + nb * nj, c * LANES:(c + 1) * LANES]
        row0 += nb * nj


def _ssm(uts, layer, S, segs):
    n = SSM_GC * LANES
    ms = [nb * nj for nb, nj in segs]
    gspec = lambda tail: pl.BlockSpec((None,) + tail, lambda g: (layer * SSM_GROUPS + g,) + (0,) * len(tail))
    io = [pl.BlockSpec((m, None, SSM_GC, LANES), lambda g: (0, g, 0, 0)) for m in ms]
    return pl.pallas_call(
        functools.partial(_ssm_body, segs=tuple(segs)),
        grid=(SSM_GROUPS,),
        in_specs=io + [gspec((n, n)), gspec((n, 4 * SSM_STATE)), gspec((4 * SSM_STATE, n)),
                       gspec((2, LANES)), gspec((1, n))],
        out_specs=io,
        out_shape=[jax.ShapeDtypeStruct((m, SSM_GROUPS, SSM_GC, LANES), f32) for m in ms],
        scratch_shapes=[pltpu.VMEM((sum(ms), LANES), f32)] * 6,
        compiler_params=_cparams(("parallel",)),
        name="s5_conv",
    )(*uts, S["toe"], S["wv"], S["wc"], S["a"], S["d"])


def _ssm_tables(a_re, a_im, b_re, b_im, c_re, c_im, log_dt, ssm_d):
    depth = a_re.shape[0]
    dg = depth * SSM_GROUPS
    lam = lax.complex(a_re.astype(f32), a_im.astype(f32))
    dt = jnp.exp(log_dt.astype(f32))[..., None]
    lam_dt = lam * dt
    lam_bar = jnp.exp(lam_dt)
    b_bar = ((lam_bar - 1.0) / lam)[..., None] * lax.complex(b_re.astype(f32), b_im.astype(f32))
    cmat = lax.complex(c_re.astype(f32), c_im.astype(f32))
    steps = jnp.arange(LANES + 1, dtype=f32)
    pw = jnp.exp(lam_dt[..., None] * steps)
    pwf, pwb = pw[:, 0], pw[:, 1]

    e = np.arange(2 * LANES)
    f_ok = (e >= LANES - 1) & (e <= 2 * LANES - 2)
    b_ok = e <= LANES - 1
    pf_f = jnp.where(f_ok, pwf[..., np.clip(e - (LANES - 1), 0, LANES - 1)], 0.0)
    pf_b = jnp.where(b_ok, pwb[..., np.clip(LANES - 1 - e, 0, LANES - 1)], 0.0)
    pf = jnp.stack([pf_f.real, pf_f.imag, pf_b.real, pf_b.imag], axis=2)
    pf = pf.reshape(dg, 4 * SSM_STATE, 2 * LANES)

    bbf, bbb = b_bar[:, 0], b_bar[:, 1]
    cf, cb = cmat[:, 0], cmat[:, 1]
    gf = cf[:, :, None, :, :] * jnp.swapaxes(bbf, -1, -2)[:, :, :, None, :]
    gb = cb[:, :, None, :, :] * jnp.swapaxes(bbb, -1, -2)[:, :, :, None, :]
    gm = jnp.concatenate([gf.real, -gf.imag, gb.real, -gb.imag], axis=-1)
    gm = gm.reshape(dg, SSM_GC * SSM_GC, 4 * SSM_STATE)

    s_idx = np.arange(LANES)
    pvf = jnp.swapaxes(pwf[..., LANES - 1 - s_idx], -1, -2)
    pvb = jnp.swapaxes(pwb[..., s_idx], -1, -2)
    pv1 = jnp.concatenate([pvf.real, pvb.real, pvf.imag, pvb.imag], axis=-1).reshape(dg, LANES, 4 * SSM_STATE)
    pv2 = jnp.concatenate([pvf.imag, pvb.imag, pvf.real, pvb.real], axis=-1).reshape(dg, LANES, 4 * SSM_STATE)
    bft, bbt = jnp.swapaxes(bbf, -1, -2), jnp.swapaxes(bbb, -1, -2)
    bv = jnp.stack([jnp.concatenate([bft.real, bbt.real, bft.real, bbt.real], axis=-1),
                    jnp.concatenate([-bft.imag, -bbt.imag, bft.imag, bbt.imag], axis=-1)], axis=3)
    bv = bv.reshape(dg, SSM_GC, 2, 4 * SSM_STATE)
    t_idx = np.arange(LANES)
    pcf, pcb = pwf[..., t_idx + 1], pwb[..., LANES - t_idx]
    pc = jnp.stack([pcf.real, pcf.imag, pcb.real, pcb.imag], axis=2).reshape(dg, 4, SSM_STATE, LANES)
    cft, cbt = jnp.swapaxes(cf, -1, -2), jnp.swapaxes(cb, -1, -2)
    ct = jnp.stack([cft.real, cft.imag, cbt.real, cbt.imag], axis=2).reshape(dg, 4, SSM_STATE, SSM_GC)

    a128 = jnp.concatenate([pwf[..., LANES], pwb[..., LANES]], axis=-1)
    a = jnp.stack([a128.real, a128.imag], axis=2).reshape(dg, 2, LANES)
    d = jnp.repeat(ssm_d.astype(f32).reshape(dg, SSM_GC), LANES, axis=-1).reshape(dg, 1, SSM_GC * LANES)
    return dict(pf=pf, gm=gm, pv1=pv1, pv2=pv2, bv=bv, pc=pc, ct=ct, a=a, d=d)


def _state_tables_body(pv1_ref, pv2_ref, bv_ref, pc_ref, ct_ref, wv_ref, wc_ref):
    pv1, pv2 = pv1_ref[...], pv2_ref[...]
    for cp in range(SSM_GC):
        wv_ref[cp * LANES:(cp + 1) * LANES, :] = (pv1 * bv_ref[cp, 0:1, :] + pv2 * bv_ref[cp, 1:2, :]).astype(bf16)
    pfr, pfi, pbr, pbi = pc_ref[0], pc_ref[1], pc_ref[2], pc_ref[3]
    cfr, cfi, cbr, cbi = ct_ref[0], ct_ref[1], ct_ref[2], ct_ref[3]
    for c in range(SSM_GC):
        col = slice(c, c + 1)
        tile = jnp.concatenate([cfr[:, col] * pfr - cfi[:, col] * pfi,
                                cbr[:, col] * pbr - cbi[:, col] * pbi,
                                -(cfr[:, col] * pfi + cfi[:, col] * pfr),
                                -(cbr[:, col] * pbi + cbi[:, col] * pbr)], axis=0)
        wc_ref[:, c * LANES:(c + 1) * LANES] = tile.astype(bf16)


def _state_tables(S):
    dg = S["pf"].shape[0]
    n, k = SSM_GC * LANES, 4 * SSM_STATE
    g3 = lambda a, b: pl.BlockSpec((None, a, b), lambda g: (g, 0, 0))
    g4 = lambda a, b, c: pl.BlockSpec((None, a, b, c), lambda g: (g, 0, 0, 0))
    return pl.pallas_call(
        _state_tables_body,
        grid=(dg,),
        in_specs=[g3(LANES, k), g3(LANES, k), g4(SSM_GC, 2, k), g4(4, SSM_STATE, LANES), g4(4, SSM_STATE, SSM_GC)],
        out_specs=(g3(n, k), g3(k, n)),
        out_shape=(jax.ShapeDtypeStruct((dg, n, k), bf16), jax.ShapeDtypeStruct((dg, k, n), bf16)),
        compiler_params=_cparams(("parallel",)),
        name="s5_state_tables",
    )(S["pv1"], S["pv2"], S["bv"], S["pc"], S["ct"])


def _mixffn_body(x_ref, mod_ref, yna_ref, ymlat_ref, yst_ref, gluw_ref, glub_ref, wo_ref, g2_ref,
                 wg_ref, wu_ref, wd_ref, o_ref, act_ref):
    ys = jnp.concatenate([yst_ref[c].reshape(SSM_WIDTH, LANES).T for c in range(yst_ref.shape[0])],
                         axis=0)
    ys = 0.5 * ys * (1.0 + jnp.tanh(math.sqrt(2.0 / math.pi) * (ys + 0.044715 * (ys * ys * ys))))
    gate = jax.nn.sigmoid(jnp.dot(ys.astype(bf16), gluw_ref[...], preferred_element_type=f32) + glub_ref[...])
    yssm = (ys * gate).astype(bf16)
    o1 = NA_WIDTH
    o2 = NA_WIDTH + MLA_HEADS * MLA_V
    mix = (jnp.dot(yna_ref[...], wo_ref[0:o1, :], preferred_element_type=f32)
           + lax.dot_general(ymlat_ref[...], wo_ref[o1:o2, :], _TN, preferred_element_type=f32)
           + jnp.dot(yssm, wo_ref[o2:, :], preferred_element_type=f32))
    x1 = x_ref[...] + mod_ref[2:3, :] * mix
    o_ref[...] = x1
    ms = jnp.mean(x1 * x1, axis=-1, keepdims=True)
    h2 = ((x1 * lax.rsqrt(ms + RMS_EPS) * g2_ref[...]) * (1.0 + mod_ref[4:5, :]) + mod_ref[3:4, :]).astype(bf16)

    th = MXU_TILE
    for c in range(FFN_HIDDEN // th):
        cs = slice(c * th, (c + 1) * th)
        g = jnp.dot(h2, wg_ref[:, cs], preferred_element_type=f32)
        u = jnp.dot(h2, wu_ref[:, cs], preferred_element_type=f32)
        act_ref[:, cs] = (g * jax.nn.sigmoid(g) * u).astype(bf16)
    ffn = jnp.dot(act_ref[...], wd_ref[...], preferred_element_type=f32)
    o_ref[...] = o_ref[...] + mod_ref[5:6, :] * ffn


def _mixffn(x, mod, yna, ymlat, yst, layer, P):
    B, L, D = x.shape
    tm = TOKEN_TILE
    nt = L // tm
    cpt = tm // LANES
    tok = lambda w: pl.BlockSpec((None, tm, w), lambda b, i: (b, i, 0))

    def wspec(name):
        tail = P[name].shape[1:]
        return pl.BlockSpec((None,) + tail, lambda b, i: (layer,) + (0,) * len(tail), pipeline_mode=pl.Buffered(1))

    return pl.pallas_call(
        _mixffn_body,
        grid=(B, nt),
        in_specs=[tok(D), pl.BlockSpec((None, 6, D), lambda b, i: (b, 0, 0)), tok(NA_WIDTH),
                  pl.BlockSpec((None, MLA_HEADS * MLA_V, tm), lambda b, i: (b, 0, i)),
                  pl.BlockSpec((cpt, SSM_GROUPS, SSM_GC, LANES), lambda b, i: (b * nt + i, 0, 0, 0)),
                  wspec("glu_w"), wspec("glu_b"), wspec("w_out"), wspec("g2"),
                  wspec("w_gate"), wspec("w_up"), wspec("w_down")],
        out_specs=tok(D),
        out_shape=jax.ShapeDtypeStruct((B, L, D), f32),
        scratch_shapes=[pltpu.VMEM((tm, FFN_HIDDEN), bf16)],
        compiler_params=_cparams(("parallel", "parallel")),
        name="mix_ffn",
    )(x, mod, yna, ymlat, yst, P["glu_w"], P["glu_b"], P["w_out"], P["g2"], P["w_gate"], P["w_up"], P["w_down"])


def _prep_params(norm1_g, w_in, na_q_g, na_k_g, mla_cq_g, mla_ckv_g, mla_w_uq, mla_w_ukv, mla_qn_g, mla_kn_g,
                 mla_qr_g, mla_kr_g, glu_w, glu_b, w_out, norm2_g, ffn_w_gate, ffn_w_up, ffn_w_down):
    depth = w_in.shape[0]
    o1 = 3 * NA_WIDTH
    o2 = o1 + MLA_Q_LORA
    o3 = o2 + MLA_KV_LORA
    o4 = o3 + MLA_ROPE
    row = lambda a: a.astype(f32)[:, None, :]
    zeros = lambda *s: jnp.zeros(s, f32)
    P = {}
    P["g1"] = row(norm1_g)
    P["g2"] = row(norm2_g)
    P["w_na"] = w_in[:, :, :o1].astype(bf16)
    P["w_c"] = jnp.concatenate([w_in[:, :, o1:o3], zeros(depth, D_MODEL, MLA_NOPE), w_in[:, :, o3:o4],
                                zeros(depth, D_MODEL, MLA_PAD - MLA_NOPE - MLA_ROPE)], axis=-1).astype(bf16)
    P["w_ut"] = jnp.swapaxes(w_in[:, :, o4:], 1, 2).astype(bf16)
    wq = mla_w_uq.reshape(depth, MLA_Q_LORA, MLA_HEADS, MLA_NOPE + MLA_ROPE)
    wq = jnp.concatenate([wq, zeros(depth, MLA_Q_LORA, MLA_HEADS, MLA_PAD - MLA_NOPE - MLA_ROPE)], axis=-1)
    P["w_uq"] = wq.reshape(depth, MLA_Q_LORA, MLA_HEADS * MLA_PAD).astype(bf16)
    wkv = mla_w_ukv.reshape(depth, MLA_KV_LORA, MLA_HEADS, MLA_NOPE + MLA_V)
    wk = jnp.concatenate([wkv[..., :MLA_NOPE], zeros(depth, MLA_KV_LORA, MLA_HEADS, MLA_PAD - MLA_NOPE)], axis=-1)
    P["w_ukv_k"] = wk.reshape(depth, MLA_KV_LORA, MLA_HEADS * MLA_PAD).astype(bf16)
    P["w_ukv_vt"] = jnp.swapaxes(wkv[..., MLA_NOPE:].reshape(depth, MLA_KV_LORA, MLA_HEADS * MLA_V), 1, 2).astype(bf16)
    P["gq_na"] = row(jnp.tile(na_q_g, (1, NA_HEADS))) * (NA_HEAD_DIM ** -0.5 * math.log2(math.e))
    P["gk_na"] = row(jnp.tile(na_k_g, (1, NA_HEADS)))
    P["g_cq"] = row(mla_cq_g)
    P["g_ckv"] = row(mla_ckv_g)
    scale = (MLA_NOPE + MLA_ROPE) ** -0.5 * math.log2(math.e)
    pad = MLA_PAD - MLA_NOPE - MLA_ROPE
    gq = jnp.concatenate([mla_qn_g, mla_qr_g, zeros(depth, pad)], axis=-1) * scale
    P["gq"] = row(jnp.tile(gq, (1, MLA_HEADS)))
    gk = jnp.concatenate([mla_kn_g, zeros(depth, MLA_PAD - MLA_NOPE)], axis=-1)
    P["gk"] = row(jnp.tile(gk, (1, MLA_HEADS)))
    P["gkr"] = row(jnp.concatenate([zeros(depth, MLA_NOPE), mla_kr_g, zeros(depth, pad)], axis=-1))
    invq = np.concatenate([np.full(MLA_NOPE, 1.0 / MLA_NOPE), np.full(MLA_ROPE, 1.0 / MLA_ROPE), np.zeros(pad)])
    invk = np.concatenate([np.full(MLA_NOPE, 1.0 / MLA_NOPE), np.zeros(MLA_PAD - MLA_NOPE)])
    P["invq"] = jnp.asarray(np.tile(invq, MLA_HEADS)[None, :], f32)
    P["invk"] = jnp.asarray(np.tile(invk, MLA_HEADS)[None, :], f32)
    lane = np.arange(NA_WIDTH)
    P["G_na"] = jnp.asarray((lane[:, None] // NA_HEAD_DIM) == (lane[None, :] // NA_HEAD_DIM), bf16)
    lane = np.arange(2 * MLA_PAD)
    grp = np.where(lane % MLA_PAD < MLA_NOPE, 0, np.where(lane % MLA_PAD < MLA_NOPE + MLA_ROPE, 1, 2))
    same = (lane[:, None] // MLA_PAD == lane[None, :] // MLA_PAD) & (grp[:, None] == grp[None, :]) & (grp[:, None] < 2)
    P["G_qm"] = jnp.asarray(same, bf16)
    P["glu_w"] = glu_w.astype(bf16)
    P["glu_b"] = row(glu_b)
    P["w_out"] = w_out.astype(bf16)
    P["w_gate"] = ffn_w_gate.astype(bf16)
    P["w_up"] = ffn_w_up.astype(bf16)
    P["w_down"] = ffn_w_down.astype(bf16)
    return P


def _rope_tables(length):
    inv = 1.0 / (ROPE_BASE ** (jnp.arange(0, MLA_ROPE, 2, dtype=f32) / MLA_ROPE))
    ang = jnp.arange(length, dtype=f32)[:, None] * inv[None, :]
    cos, sin = jnp.cos(ang), jnp.sin(ang)
    half = MLA_ROPE // 2
    z = lambda w: jnp.zeros((length, w), f32)
    pad = MLA_PAD - MLA_NOPE - MLA_ROPE
    cos_t = jnp.concatenate([jnp.ones((length, MLA_NOPE), f32), cos, cos, z(pad)], axis=-1)
    sin_lo = jnp.concatenate([z(MLA_NOPE), -sin, z(half), z(pad)], axis=-1)
    sin_hi = jnp.concatenate([z(MLA_NOPE), z(half), sin, z(pad)], axis=-1)
    return cos_t, sin_lo, sin_hi


def _trunks(xs, mods, P, S, bias):
    xs = list(xs)
    ropes = [_rope_tables(x.shape[1]) for x in xs]
    segs = [(x.shape[0], x.shape[1] // LANES) for x in xs]
    for layer in range(len(mods[0])):
        heads, uts = [], []
        for x, mod, rope in zip(xs, mods, ropes):
            naq, nak, nav, mqt, mk, mvt, ut = _inproj(x, mod[layer], layer, P, rope)
            heads.append((_na_attention(naq, nak, nav, bias, layer), _mla_attention(mqt, mk, mvt)))
            uts.append(ut)
        ysts = _ssm(uts, layer, S, segs)
        xs = [_mixffn(x, mod[layer], yna, ymla, yst, layer, P)
              for x, mod, (yna, ymla), yst in zip(xs, mods, heads, ysts)]
    return xs


def kernel(x_prompt, x_sample, c_prompt, c_sample, ada_w, ada_b, norm1_g, w_in, na_q_g, na_k_g, na_rpb, mla_cq_g,
           mla_ckv_g, mla_w_uq, mla_w_ukv, mla_qn_g, mla_kn_g, mla_qr_g, mla_kr_g, ssm_a_re, ssm_a_im, ssm_b_re,
           ssm_b_im, ssm_c_re, ssm_c_im, ssm_log_dt, ssm_d, glu_w, glu_b, w_out, norm2_g, ffn_w_gate, ffn_w_up,
           ffn_w_down):
    depth = w_in.shape[0]
    nbp, nbs = c_prompt.shape[0], c_sample.shape[0]
    rows = -(-(nbp + nbs) // 8) * 8
    c_all = jnp.concatenate([c_prompt, c_sample, jnp.zeros((rows - nbp - nbs, D_MODEL), f32)], axis=0)
    mod = _modulation(c_all, ada_w, ada_b).reshape(depth, rows, 6, D_MODEL)
    mods_p = [mod[l, :nbp] for l in range(depth)]
    mods_s = [mod[l, nbp:nbp + nbs] for l in range(depth)]

    P = _prep_params(norm1_g, w_in, na_q_g, na_k_g, mla_cq_g, mla_ckv_g, mla_w_uq, mla_w_ukv, mla_qn_g, mla_kn_g,
                     mla_qr_g, mla_kr_g, glu_w, glu_b, w_out, norm2_g, ffn_w_gate, ffn_w_up, ffn_w_down)
    S = _ssm_tables(ssm_a_re, ssm_a_im, ssm_b_re, ssm_b_im, ssm_c_re, ssm_c_im, ssm_log_dt, ssm_d)
    S["toe"] = _toeplitz_gen(S["gm"], S["pf"])
    S["wv"], S["wc"] = _state_tables(S)
    bias = _na_bias_tables(na_rpb)

    y_prompt, y_sample = _trunks((x_prompt, x_sample), (mods_p, mods_s), P, S, bias)
    return (y_prompt, y_sample)
```
